```python
import jax, jax.numpy as jnp
from jax import lax
import numpy as np

D_MODEL = 2048
BATCH = 1
SEQ = 8192
DEPTH = 1
DEC_BATCH = 128
DEC_SEQ = 4
PAST_LEN = 8192
PAGE_SIZE = 128

N_HEADS = 32
HEAD_DIM = 64
N_KV_HEADS = 4
GROUP = N_HEADS // N_KV_HEADS
D_ATTN = N_HEADS * HEAD_DIM
D_KV = N_KV_HEADS * HEAD_DIM
WINDOW = 128
BLOCK = WINDOW
D_CONV = D_MODEL
CONV_WIDTH = 3
D_FF = -(-8 * D_MODEL // (3 * 256)) * 256
ROPE_THETA = 10000.0
EPS = 1e-6
NEG_INF = -1e30
SPLIT_SIZES = (D_ATTN, D_KV, D_KV, D_CONV, D_CONV, D_CONV, D_ATTN, D_CONV)
D_IN = sum(SPLIT_SIZES)

kernel_name = 'hybrid_shortconv_swa_sink_decoder_step'


def _rms_norm(x, g):
    x32 = x.astype(jnp.float32)
    y = x32 * lax.rsqrt(jnp.mean(x32 * x32, axis=-1, keepdims=True) + EPS)
    return (y * g.astype(jnp.float32)).astype(x.dtype)


def _rope(x, pos):
    inv = ROPE_THETA ** (-jnp.arange(0, HEAD_DIM, 2, dtype=jnp.float32) / HEAD_DIM)
    ang = pos.astype(jnp.float32)[:, None] * inv[None, :]
    cos = jnp.cos(ang)[:, None, :]
    sin = jnp.sin(ang)[:, None, :]
    x32 = x.astype(jnp.float32)
    x1, x2 = x32[..., : HEAD_DIM // 2], x32[..., HEAD_DIM // 2:]
    return jnp.concatenate([x1 * cos - x2 * sin, x2 * cos + x1 * sin], axis=-1).astype(x.dtype)


def _project(x, norm_g, w_in, q_g, k_g, pos):
    xn = _rms_norm(x, norm_g)
    z = xn @ w_in
    idx = [int(i) for i in np.cumsum(SPLIT_SIZES)[:-1]]
    q, k, v, h, b, c, ga, gc = jnp.split(z, idx, axis=-1)
    lead = x.shape[:-1]
    q = q.reshape(*lead, N_HEADS, HEAD_DIM)
    k = k.reshape(*lead, N_KV_HEADS, HEAD_DIM)
    v = v.reshape(*lead, N_KV_HEADS, HEAD_DIM)
    q = _rope(_rms_norm(q, q_g), pos)
    k = _rope(_rms_norm(k, k_g), pos)
    u = c * h
    return q, k, v, u, b, ga, gc


def _sink_attention(q, k, v, q_pos, k_pos, sinks):
    s = jnp.einsum('...qkgd,...skd->...kgqs', q, k).astype(jnp.float32) * (HEAD_DIM ** -0.5)
    rel = q_pos[..., :, None] - k_pos[..., None, :]
    valid = (rel >= 0) & (rel < WINDOW) & (k_pos[..., None, :] >= 0)
    s = jnp.where(valid[..., None, None, :, :], s, NEG_INF)
    sink = sinks.astype(jnp.float32).reshape(N_KV_HEADS, GROUP)[:, :, None, None]
    m = jnp.maximum(jnp.max(s, axis=-1, keepdims=True), sink)
    p = jnp.exp(s - m)
    p = p / (jnp.sum(p, axis=-1, keepdims=True) + jnp.exp(sink - m))
    return jnp.einsum('...kgqs,...skd->...qkgd', p.astype(v.dtype), v)


def _causal_conv(u_ext, w):
    t = u_ext.shape[1] - (CONV_WIDTH - 1)
    out = w[0] * u_ext[:, 0:t]
    for j in range(1, CONV_WIDTH):
        out = out + w[j] * u_ext[:, j:j + t]
    return out


def _merge_out(attn, conv, b, ga, gc, w_out):
    mix = jax.nn.sigmoid(ga) * attn + jax.nn.sigmoid(gc) * (b * conv)
    return mix @ w_out


def _ffn(x, norm_g, w_gate_up, w_down):
    h = _rms_norm(x, norm_g) @ w_gate_up
    g, u = jnp.split(h, 2, axis=-1)
    return (jax.nn.silu(g) * u) @ w_down


def _prompt_layer(x, norm_mix, w_in, q_norm, k_norm, sinks, conv_w, w_out, norm_ffn, w_gate_up, w_down):
    bsz, t, _ = x.shape
    pos = jnp.arange(t, dtype=jnp.int32)
    q, k, v, u, b, ga, gc = _project(x, norm_mix, w_in, q_norm, k_norm, pos)
    nb = t // BLOCK
    qb = q.reshape(bsz, nb, BLOCK, N_KV_HEADS, GROUP, HEAD_DIM)

    def band(a):
        ab = a.reshape(bsz, nb, BLOCK, N_KV_HEADS, HEAD_DIM)
        prev = jnp.pad(ab[:, :-1], ((0, 0), (1, 0), (0, 0), (0, 0), (0, 0)))
        return jnp.concatenate([prev, ab], axis=2)

    q_pos = pos.reshape(nb, BLOCK)
    k_pos = jnp.concatenate([q_pos - BLOCK, q_pos], axis=1)
    attn = _sink_attention(qb, band(k), band(v), q_pos, k_pos, sinks).reshape(bsz, t, D_ATTN)
    u_ext = jnp.pad(u, ((0, 0), (CONV_WIDTH - 1, 0), (0, 0)))
    conv = _causal_conv(u_ext, conv_w)
    y = x + _merge_out(attn, conv, b, ga, gc, w_out)
    y = y + _ffn(y, norm_ffn, w_gate_up, w_down)
    w_buf = min(WINDOW, t)
    return y, k[:, t - w_buf:], v[:, t - w_buf:], u_ext[:, -(CONV_WIDTH - 1):]


def _sample_layer(x, k_buf, v_buf, conv_buf, norm_mix, w_in, q_norm, k_norm, sinks, conv_w, w_out, norm_ffn, w_gate_up, w_down):
    nseq, t, _ = x.shape
    w_buf = k_buf.shape[1]
    pos = PAST_LEN + jnp.arange(t, dtype=jnp.int32)
    q, k, v, u, b, ga, gc = _project(x, norm_mix, w_in, q_norm, k_norm, pos)
    k_all = jnp.concatenate([k_buf.astype(k.dtype), k], axis=1)
    v_all = jnp.concatenate([v_buf.astype(v.dtype), v], axis=1)
    k_pos = jnp.concatenate([PAST_LEN - w_buf + jnp.arange(w_buf, dtype=jnp.int32), pos])
    qg = q.reshape(nseq, t, N_KV_HEADS, GROUP, HEAD_DIM)
    attn = _sink_attention(qg, k_all, v_all, pos, k_pos, sinks).reshape(nseq, t, D_ATTN)
    u_ext = jnp.concatenate([conv_buf.astype(u.dtype), u], axis=1)
    conv = _causal_conv(u_ext, conv_w)
    y = x + _merge_out(attn, conv, b, ga, gc, w_out)
    y = y + _ffn(y, norm_ffn, w_gate_up, w_down)
    return y, k_all[:, -w_buf:], v_all[:, -w_buf:], u_ext[:, -(CONV_WIDTH - 1):]


def setup_inputs(seed: int = 0) -> dict:
    key = jax.random.key(seed)
    ks = jax.random.split(key, 16)
    f32 = jnp.float32
    w_buf = min(WINDOW, PAST_LEN)
    nrm = lambda k, shape: jax.random.normal(k, shape, dtype=f32)
    return {
        'x_prompt': nrm(ks[0], (BATCH, SEQ, D_MODEL)),
        'x_sample': nrm(ks[1], (DEC_BATCH, DEC_SEQ, D_MODEL)),
        'cache_k_win': nrm(ks[2], (DEPTH, DEC_BATCH, w_buf, N_KV_HEADS, HEAD_DIM)),
        'cache_v_win': nrm(ks[3], (DEPTH, DEC_BATCH, w_buf, N_KV_HEADS, HEAD_DIM)),
        'state_conv': nrm(ks[4], (DEPTH, DEC_BATCH, CONV_WIDTH - 1, D_CONV)),
        'norm_mix': 1.0 + 0.1 * nrm(ks[5], (DEPTH, D_MODEL)),
        'w_in': nrm(ks[6], (DEPTH, D_MODEL, D_IN)) * D_MODEL ** -0.5,
        'q_norm': 1.0 + 0.1 * nrm(ks[7], (DEPTH, HEAD_DIM)),
        'k_norm': 1.0 + 0.1 * nrm(ks[8], (DEPTH, HEAD_DIM)),
        'sinks': 0.5 * nrm(ks[9], (DEPTH, N_HEADS)),
        'conv_w': nrm(ks[10], (DEPTH, CONV_WIDTH, D_CONV)) * CONV_WIDTH ** -0.5,
        'w_out': nrm(ks[11], (DEPTH, D_ATTN, D_MODEL)) * D_ATTN ** -0.5,
        'norm_ffn': 1.0 + 0.1 * nrm(ks[12], (DEPTH, D_MODEL)),
        'w_gate_up': nrm(ks[13], (DEPTH, D_MODEL, 2 * D_FF)) * D_MODEL ** -0.5,
        'w_down': nrm(ks[14], (DEPTH, D_FF, D_MODEL)) * D_FF ** -0.5,
    }


def reference(x_prompt, x_sample, cache_k_win, cache_v_win, state_conv, norm_mix, w_in, q_norm, k_norm, sinks, conv_w, w_out, norm_ffn, w_gate_up, w_down):
    yp, ys = x_prompt, x_sample
    pk, pv, pc, sk, sv, sc = [], [], [], [], [], []
    for l in range(DEPTH):
        params = (norm_mix[l], w_in[l], q_norm[l], k_norm[l], sinks[l], conv_w[l],
                  w_out[l], norm_ffn[l], w_gate_up[l], w_down[l])
        yp, k_new, v_new, c_new = _prompt_layer(yp, *params)
        pk.append(k_new); pv.append(v_new); pc.append(c_new)
        ys, k_s, v_s, c_s = _sample_layer(ys, cache_k_win[l], cache_v_win[l], state_conv[l], *params)
        sk.append(k_s); sv.append(v_s); sc.append(c_s)
    return (yp, ys, jnp.stack(pk), jnp.stack(pv), jnp.stack(pc), jnp.stack(sk), jnp.stack(sv), jnp.stack(sc))
```

```python
import functools

import jax
import jax.numpy as jnp
from jax import lax
from jax.experimental import pallas as pl
from jax.experimental.pallas import tpu as pltpu

F32 = jnp.float32
BF16 = jnp.bfloat16

D_MODEL = 2048
SEQ = 8192
DEC_BATCH = 128
DEC_SEQ = 4
PAST_LEN = 8192
N_HEADS = 32
HEAD_DIM = 64
N_KV_HEADS = 4
GROUP = N_HEADS // N_KV_HEADS
D_ATTN = N_HEADS * HEAD_DIM
D_KV = N_KV_HEADS * HEAD_DIM
WINDOW = 128
D_CONV = D_MODEL
CONV_WIDTH = 3
D_FF = 5632
ROPE_THETA = 10000.0
EPS = 1e-6
NEG_INF = -1e30
D_IN = 2 * D_ATTN + 2 * D_KV + 3 * D_CONV + D_CONV
D_REST = 5 * D_MODEL

N_P = SEQ
N_S = DEC_BATCH * DEC_SEQ
N_ALL = N_P + N_S

LANES = 128
VMEM_LIMIT = 56 * 1024 * 1024

ROW_TILE_SMALL = 512
ROW_TILE_BIG = 1088
SEQ_BLOCK = 8


def _params(n_axes):
    return pltpu.CompilerParams(
        dimension_semantics=("arbitrary",) * n_axes, vmem_limit_bytes=VMEM_LIMIT)


def _rmsnorm_kernel(x_ref, g_ref, *rest):
    o_ref = rest[-1]
    x = x_ref[...]
    ms = jnp.mean(x * x, axis=-1, keepdims=True)
    o_ref[...] = (x * lax.rsqrt(ms + EPS) * g_ref[...]).astype(o_ref.dtype)


def _rmsnorm(x, g, out_rows, row_block_offset, into=None):
    rows, d = x.shape
    tm = ROW_TILE_SMALL
    in_specs = [pl.BlockSpec((tm, d), lambda i: (i, 0)),
                pl.BlockSpec((1, d), lambda i: (0, 0))]
    args = [x, g]
    aliases = {}
    if into is not None:
        in_specs.append(pl.BlockSpec(memory_space=pl.ANY))
        args.append(into)
        aliases = {2: 0}
    return pl.pallas_call(
        _rmsnorm_kernel,
        grid=(rows // tm,),
        in_specs=in_specs,
        out_specs=pl.BlockSpec((tm, d), lambda i: (i + row_block_offset, 0)),
        out_shape=jax.ShapeDtypeStruct((out_rows, d), BF16),
        input_output_aliases=aliases,
        compiler_params=_params(1),
        name="rmsnorm",
    )(*args)


def _cast_weight_once(w_ref, wbf_ref):
    @pl.when(pl.program_id(1) == 0)
    def _():
        wbf_ref[...] = w_ref[...].astype(BF16)


def _headnorm_rope(zc, g128, cos_t, sin_t, bd, scale):
    sq = zc * zc
    hi = sq.astype(BF16)
    lo = (sq - hi.astype(F32)).astype(BF16)
    ms = (jnp.dot(hi, bd, preferred_element_type=F32)
          + jnp.dot(lo, bd, preferred_element_type=F32))
    y = zc * lax.rsqrt(ms + EPS) * g128
    lane = lax.broadcasted_iota(jnp.int32, y.shape, 1)
    first_half = (lane & (HEAD_DIM // 2)) == 0
    partner = jnp.where(first_half,
                        pltpu.roll(y, LANES - HEAD_DIM // 2, 1),
                        pltpu.roll(y, HEAD_DIM // 2, 1))
    out = y * cos_t + partner * sin_t
    if scale != 1.0:
        out = out * scale
    return out


def _proj_plain_kernel(x_ref, w_ref, o_ref, wbf_ref):
    _cast_weight_once(w_ref, wbf_ref)
    o_ref[...] = jnp.dot(x_ref[...], wbf_ref[...],
                         preferred_element_type=F32).astype(o_ref.dtype)


def _proj_q_kernel(x_ref, w_ref, g_ref, cos_ref, sin_ref, bd_ref, o_ref, wbf_ref):
    _cast_weight_once(w_ref, wbf_ref)
    z = jnp.dot(x_ref[...], wbf_ref[...], preferred_element_type=F32)
    g128 = g_ref[...]
    cos_t = cos_ref[...]
    sin_t = sin_ref[...]
    bd = bd_ref[...]
    for c in range(z.shape[1] // LANES):
        zc = z[:, c * LANES:(c + 1) * LANES]
        out = _headnorm_rope(zc, g128, cos_t, sin_t, bd, HEAD_DIM ** -0.5)
        o_ref[:, c * LANES:(c + 1) * LANES] = out.astype(o_ref.dtype)


def _proj_kv_kernel(x_ref, w_ref, g_ref, cos_ref, sin_ref, bd_ref, o_ref, wbf_ref):
    _cast_weight_once(w_ref, wbf_ref)
    z = jnp.dot(x_ref[...], wbf_ref[...], preferred_element_type=F32)
    g128 = g_ref[...]
    cos_t = cos_ref[...]
    sin_t = sin_ref[...]
    bd = bd_ref[...]
    for c in range(D_KV // LANES):
        zc = z[:, c * LANES:(c + 1) * LANES]
        o_ref[:, c * LANES:(c + 1) * LANES] = _headnorm_rope(zc, g128, cos_t, sin_t, bd, 1.0)
    o_ref[:, D_KV:] = z[:, D_KV:]


def _in_proj(xn, w_in, q_g128, k_g128, cos_t, sin_t, bd):
    tm = ROW_TILE_BIG
    n_row = N_ALL // tm
    k = D_MODEL
    x_spec = pl.BlockSpec((tm, k), lambda j, i: (i, 0))
    rope_specs = [pl.BlockSpec((1, LANES), lambda j, i: (0, 0)),
                  pl.BlockSpec((tm, LANES), lambda j, i: (i, 0)),
                  pl.BlockSpec((tm, LANES), lambda j, i: (i, 0)),
                  pl.BlockSpec((LANES, LANES), lambda j, i: (0, 0))]

    tn_q = 1024
    q = pl.pallas_call(
        _proj_q_kernel,
        grid=(D_ATTN // tn_q, n_row),
        in_specs=[x_spec, pl.BlockSpec((k, tn_q), lambda j, i: (0, j))] + rope_specs,
        out_specs=pl.BlockSpec((tm, tn_q), lambda j, i: (i, j)),
        out_shape=jax.ShapeDtypeStruct((N_ALL, D_ATTN), BF16),
        scratch_shapes=[pltpu.VMEM((k, tn_q), BF16)],
        compiler_params=_params(2),
        name="proj_q",
    )(xn, w_in, q_g128, cos_t, sin_t, bd)

    tn_kv = 2 * D_KV
    kv = pl.pallas_call(
        _proj_kv_kernel,
        grid=(1, n_row),
        in_specs=[x_spec, pl.BlockSpec((k, tn_kv), lambda j, i: (0, D_ATTN // tn_kv))] + rope_specs,
        out_specs=pl.BlockSpec((tm, tn_kv), lambda j, i: (i, 0)),
        out_shape=jax.ShapeDtypeStruct((N_ALL, tn_kv), F32),
        scratch_shapes=[pltpu.VMEM((k, tn_kv), BF16)],
        compiler_params=_params(2),
        name="proj_kv",
    )(xn, w_in, k_g128, cos_t, sin_t, bd)

    tn_r = 1280
    rest_start = D_ATTN + 2 * D_KV
    rest = pl.pallas_call(
        _proj_plain_kernel,
        grid=(D_REST // tn_r, n_row),
        in_specs=[x_spec, pl.BlockSpec((k, tn_r), lambda j, i: (0, j + rest_start // tn_r))],
        out_specs=pl.BlockSpec((tm, tn_r), lambda j, i: (i, j)),
        out_shape=jax.ShapeDtypeStruct((N_ALL, D_REST), BF16),
        scratch_shapes=[pltpu.VMEM((k, tn_r), BF16)],
        compiler_params=_params(2),
        name="proj_rest",
    )(xn, w_in)
    return q, kv, rest


def _gated_merge(attn, u, u1, u2, b_ref, ga_ref, gc_ref, cw_ref):
    conv = cw_ref[0:1, :] * u2 + cw_ref[1:2, :] * u1 + cw_ref[2:3, :] * u
    ga = ga_ref[...].astype(F32)
    gc = gc_ref[...].astype(F32)
    b = b_ref[...].astype(F32)
    return jax.nn.sigmoid(ga) * attn + jax.nn.sigmoid(gc) * (b * conv)


def _prompt_mix_kernel(sinks_ref, q_ref, kvp_ref, kvc_ref, h_ref, b_ref, c_ref, ga_ref, gc_ref,
                       cw_ref, mix_ref, tail_ref, tail_scr, attn_scr):
    j = pl.program_id(0)

    @pl.when(j == 0)
    def _():
        tail_scr[...] = jnp.zeros_like(tail_scr)

    kv = jnp.concatenate([kvp_ref[...], kvc_ref[...]], axis=0).astype(BF16)
    row = lax.broadcasted_iota(jnp.int32, (WINDOW, 2 * WINDOW), 0)
    col = lax.broadcasted_iota(jnp.int32, (WINDOW, 2 * WINDOW), 1)
    first_key = jnp.where(j > 0, 0, WINDOW)
    valid = (col > row) & (col <= row + WINDOW) & (col >= first_key)

    for g in range(N_KV_HEADS):
        kg = kv[:, g * HEAD_DIM:(g + 1) * HEAD_DIM]
        vg = kv[:, D_KV + g * HEAD_DIM:D_KV + (g + 1) * HEAD_DIM]
        for i in range(GROUP):
            hd = g * GROUP + i
            qh = q_ref[:, hd * HEAD_DIM:(hd + 1) * HEAD_DIM]
            s = lax.dot_general(qh, kg, (((1,), (1,)), ((), ())), preferred_element_type=F32)
            s = jnp.where(valid, s, NEG_INF)
            sink = sinks_ref[0, hd]
            m = jnp.maximum(jnp.max(s, axis=1, keepdims=True), sink)
            p = jnp.exp(s - m)
            l = jnp.sum(p, axis=1, keepdims=True) + jnp.exp(sink - m)
            o = jnp.dot(p.astype(BF16), vg, preferred_element_type=F32) / l
            attn_scr[:, hd * HEAD_DIM:(hd + 1) * HEAD_DIM] = o

    u = c_ref[...].astype(F32) * h_ref[...].astype(F32)
    r = lax.broadcasted_iota(jnp.int32, u.shape, 0)
    t_last = tail_scr[7:8, :]
    t_prev = tail_scr[6:7, :]
    u1 = jnp.where(r == 0, t_last, pltpu.roll(u, 1, 0))
    u2 = jnp.where(r == 0, t_prev, jnp.where(r == 1, t_last, pltpu.roll(u, 2, 0)))
    mix = _gated_merge(attn_scr[...], u, u1, u2, b_ref, ga_ref, gc_ref, cw_ref)
    mix_ref[...] = mix.astype(mix_ref.dtype)
    tail = u[WINDOW - 8:, :]
    tail_scr[...] = tail
    tail_ref[...] = tail


def _prompt_mix(sinks, q, kv, rest, conv_w):
    nb = N_P // WINDOW
    blk = lambda c: pl.BlockSpec((WINDOW, D_MODEL), lambda j, c=c: (j, c))
    return pl.pallas_call(
        _prompt_mix_kernel,
        grid=(nb,),
        in_specs=[pl.BlockSpec(memory_space=pltpu.SMEM),
                  pl.BlockSpec((WINDOW, D_ATTN), lambda j: (j, 0)),
                  pl.BlockSpec((WINDOW, 2 * D_KV), lambda j: (jnp.maximum(j - 1, 0), 0)),
                  pl.BlockSpec((WINDOW, 2 * D_KV), lambda j: (j, 0)),
                  blk(0), blk(1), blk(2), blk(3), blk(4),
                  pl.BlockSpec((CONV_WIDTH, D_CONV), lambda j: (0, 0))],
        out_specs=[pl.BlockSpec((WINDOW, D_MODEL), lambda j: (j, 0)),
                   pl.BlockSpec((8, D_CONV), lambda j: (0, 0))],
        out_shape=[jax.ShapeDtypeStruct((N_ALL, D_MODEL), BF16),
                   jax.ShapeDtypeStruct((8, D_CONV), F32)],
        scratch_shapes=[pltpu.VMEM((8, D_CONV), F32), pltpu.VMEM((WINDOW, D_ATTN), F32)],
        compiler_params=_params(1),
        name="prompt_mix",
    )(sinks, q, kv, kv, rest, rest, rest, rest, rest, conv_w)


def _sample_attn_kernel(q_ref, kvn_ref, ck_ref, cv_ref, sink_ref, rep_ref,
                        o_ref, kw_ref, vw_ref, kext, vext):
    n_new = SEQ_BLOCK * DEC_SEQ
    rows = DEC_SEQ * N_HEADS
    kext[WINDOW:WINDOW + n_new, :] = kvn_ref[:, 0:D_KV]
    vext[WINDOW:WINDOW + n_new, :] = kvn_ref[:, D_KV:]
    kext[WINDOW + n_new:, :] = jnp.zeros((WINDOW - n_new, D_KV), F32)
    vext[WINDOW + n_new:, :] = jnp.zeros((WINDOW - n_new, D_KV), F32)

    row = lax.broadcasted_iota(jnp.int32, (rows, 2 * WINDOW), 0)
    col = lax.broadcasted_iota(jnp.int32, (rows, 2 * WINDOW), 1)
    t = row >> 5
    same_group = ((row & (N_HEADS - 1)) >> 3) == (col >> 6)
    cn = col - WINDOW
    sink = sink_ref[:, 0:1]
    rep = rep_ref[...]

    for s in range(SEQ_BLOCK):
        kext[0:WINDOW, :] = ck_ref[s]
        vext[0:WINDOW, :] = cv_ref[s]
        q4 = jnp.dot(q_ref[s], rep, preferred_element_type=F32)
        q4 = jnp.where(same_group, q4, 0.0).astype(BF16)
        sc = lax.dot_general(q4, kext[...].astype(BF16), (((1,), (1,)), ((), ())),
                             preferred_element_type=F32)
        valid_new = (cn >= 0) & (cn < n_new) & ((cn >> 2) == s) & ((cn & 3) <= t)
        valid = ((col < WINDOW) & (col > t)) | valid_new
        sc = jnp.where(valid, sc, NEG_INF)
        m = jnp.maximum(jnp.max(sc, axis=1, keepdims=True), sink)
        p = jnp.exp(sc - m)
        l = jnp.sum(p, axis=1, keepdims=True) + jnp.exp(sink - m)
        o4 = jnp.dot(p.astype(BF16), vext[...].astype(BF16), preferred_element_type=F32)
        o4 = jnp.where(same_group, o4, 0.0)
        a = o4[:, 0:LANES] + o4[:, LANES:]
        o = a + pltpu.roll(a, HEAD_DIM, 1)
        o_ref[s] = o[:, 0:HEAD_DIM] / l

        keep = WINDOW - DEC_SEQ
        kw_ref[s, 0:keep, :] = ck_ref[s, DEC_SEQ:WINDOW, :]
        vw_ref[s, 0:keep, :] = cv_ref[s, DEC_SEQ:WINDOW, :]
        kw_ref[s, keep:WINDOW, :] = kvn_ref[s * DEC_SEQ:(s + 1) * DEC_SEQ, 0:D_KV]
        vw_ref[s, keep:WINDOW, :] = kvn_ref[s * DEC_SEQ:(s + 1) * DEC_SEQ, D_KV:]


def _sample_attn(q_s, kv, cache_k, cache_v, sink_col, rep):
    rows = DEC_SEQ * N_HEADS
    n_new = SEQ_BLOCK * DEC_SEQ
    cache_spec = pl.BlockSpec((SEQ_BLOCK, WINDOW, D_KV), lambda i: (i, 0, 0))
    return pl.pallas_call(
        _sample_attn_kernel,
        grid=(DEC_BATCH // SEQ_BLOCK,),
        in_specs=[pl.BlockSpec((SEQ_BLOCK, rows, HEAD_DIM), lambda i: (i, 0, 0)),
                  pl.BlockSpec((n_new, 2 * D_KV), lambda i: (i + N_P // n_new, 0)),
                  cache_spec, cache_spec,
                  pl.BlockSpec((rows, LANES), lambda i: (0, 0)),
                  pl.BlockSpec((HEAD_DIM, D_KV), lambda i: (0, 0))],
        out_specs=[pl.BlockSpec((SEQ_BLOCK, rows, HEAD_DIM), lambda i: (i, 0, 0)),
                   cache_spec, cache_spec],
        out_shape=[jax.ShapeDtypeStruct((DEC_BATCH, rows, HEAD_DIM), F32),
                   jax.ShapeDtypeStruct((DEC_BATCH, WINDOW, D_KV), F32),
                   jax.ShapeDtypeStruct((DEC_BATCH, WINDOW, D_KV), F32)],
        scratch_shapes=[pltpu.VMEM((2 * WINDOW, D_KV), F32), pltpu.VMEM((2 * WINDOW, D_KV), F32)],
        compiler_params=_params(1),
        name="sample_attn",
    )(q_s, kv, cache_k, cache_v, sink_col, rep)


def _sample_merge_kernel(attn_ref, h_ref, b_ref, c_ref, ga_ref, gc_ref, st_ref, cw_ref, buf_ref,
                         mix_ref, u_ref):
    del buf_ref
    u = c_ref[...].astype(F32) * h_ref[...].astype(F32)
    u_ref[...] = u
    rows = u.shape[0]
    t = lax.broadcasted_iota(jnp.int32, u.shape, 0) & (DEC_SEQ - 1)
    st = st_ref[...]
    u1 = jnp.where(t == 0, pltpu.roll(st, rows - 1, 0), pltpu.roll(u, 1, 0))
    u2 = jnp.where(t < 2, st, pltpu.roll(u, 2, 0))
    mix = _gated_merge(attn_ref[...], u, u1, u2, b_ref, ga_ref, gc_ref, cw_ref)
    mix_ref[...] = mix.astype(mix_ref.dtype)


def _sample_merge(attn_s, rest, st_rows, conv_w, mix_buf):
    tm = WINDOW
    off = N_P // tm
    blk = lambda c: pl.BlockSpec((tm, D_MODEL), lambda i, c=c: (i + off, c))
    return pl.pallas_call(
        _sample_merge_kernel,
        grid=(N_S // tm,),
        in_specs=[pl.BlockSpec((tm, D_ATTN), lambda i: (i, 0)),
                  blk(0), blk(1), blk(2), blk(3), blk(4),
                  pl.BlockSpec((tm, D_CONV), lambda i: (i, 0)),
                  pl.BlockSpec((CONV_WIDTH, D_CONV), lambda i: (0, 0)),
                  pl.BlockSpec(memory_space=pl.ANY)],
        out_specs=[pl.BlockSpec((tm, D_MODEL), lambda i: (i + off, 0)),
                   pl.BlockSpec((tm, D_CONV), lambda i: (i, 0))],
        out_shape=[jax.ShapeDtypeStruct((N_ALL, D_MODEL), BF16),
                   jax.ShapeDtypeStruct((N_S, D_CONV), F32)],
        input_output_aliases={8: 0},
        compiler_params=_params(1),
        name="sample_merge",
    )(attn_s, rest, rest, rest, rest, rest, st_rows, conv_w, mix_buf)


def _out_proj_kernel(m_ref, w_ref, xp_ref, xs_ref, o_ref, wbf_ref):
    _cast_weight_once(w_ref, wbf_ref)
    is_prompt = pl.program_id(1) < N_P // ROW_TILE_SMALL
    res = jnp.where(is_prompt, xp_ref[...], xs_ref[...])
    o_ref[...] = res + jnp.dot(m_ref[...], wbf_ref[...], preferred_element_type=F32)


def _out_proj(mix, w_out, x_p, x_s):
    tm = ROW_TILE_SMALL
    tn = 1024
    last_p = N_P // tm - 1
    return pl.pallas_call(
        _out_proj_kernel,
        grid=(D_MODEL // tn, N_ALL // tm),
        in_specs=[pl.BlockSpec((tm, D_ATTN), lambda j, i: (i, 0)),
                  pl.BlockSpec((D_ATTN, tn), lambda j, i: (0, j)),
                  pl.BlockSpec((tm, tn), lambda j, i: (jnp.minimum(i, last_p), j)),
                  pl.BlockSpec((tm, tn), lambda j, i: (0, j))],
        out_specs=pl.BlockSpec((tm, tn), lambda j, i: (i, j)),
        out_shape=jax.ShapeDtypeStruct((N_ALL, D_MODEL), F32),
        scratch_shapes=[pltpu.VMEM((D_ATTN, tn), BF16)],
        compiler_params=_params(2),
        name="out_proj",
    )(mix, w_out, x_p, x_s)


def _ffn_up_kernel(x_ref, wg_ref, wu_ref, o_ref, wg_bf, wu_bf):
    @pl.when(pl.program_id(1) == 0)
    def _():
        wg_bf[...] = wg_ref[...].astype(BF16)
        wu_bf[...] = wu_ref[...].astype(BF16)

    x = x_ref[...]
    g = jnp.dot(x, wg_bf[...], preferred_element_type=F32)
    u = jnp.dot(x, wu_bf[...], preferred_element_type=F32)
    o_ref[...] = ((g * jax.nn.sigmoid(g)) * u).astype(o_ref.dtype)


def _ffn_up(yn, w_gate_up):
    tm = ROW_TILE_BIG
    tn = 512
    nt = D_FF // tn
    return pl.pallas_call(
        _ffn_up_kernel,
        grid=(nt, N_ALL // tm),
        in_specs=[pl.BlockSpec((tm, D_MODEL), lambda j, i: (i, 0)),
                  pl.BlockSpec((D_MODEL, tn), lambda j, i: (0, j)),
                  pl.BlockSpec((D_MODEL, tn), lambda j, i: (0, j + nt))],
        out_specs=pl.BlockSpec((tm, tn), lambda j, i: (i, j)),
        out_shape=jax.ShapeDtypeStruct((N_ALL, D_FF), BF16),
        scratch_shapes=[pltpu.VMEM((D_MODEL, tn), BF16), pltpu.VMEM((D_MODEL, tn), BF16)],
        compiler_params=_params(2),
        name="ffn_up",
    )(yn, w_gate_up, w_gate_up)


def _ffn_down_kernel(a_ref, w_ref, r_ref, yp_ref, ys_ref, wbf_ref):
    _cast_weight_once(w_ref, wbf_ref)
    i = pl.program_id(1)
    n_prompt_tiles = N_P // ROW_TILE_SMALL
    y = r_ref[...] + jnp.dot(a_ref[...], wbf_ref[...], preferred_element_type=F32)

    @pl.when(i < n_prompt_tiles)
    def _():
        yp_ref[...] = y

    @pl.when(i >= n_prompt_tiles)
    def _():
        ys_ref[...] = y


def _ffn_down(act, w_down, y1):
    tm = ROW_TILE_SMALL
    tn = 512
    last_p = N_P // tm - 1
    return pl.pallas_call(
        _ffn_down_kernel,
        grid=(D_MODEL // tn, N_ALL // tm),
        in_specs=[pl.BlockSpec((tm, D_FF), lambda j, i: (i, 0)),
                  pl.BlockSpec((D_FF, tn), lambda j, i: (0, j)),
                  pl.BlockSpec((tm, tn), lambda j, i: (i, j))],
        out_specs=[pl.BlockSpec((tm, tn), lambda j, i: (jnp.minimum(i, last_p), j)),
                   pl.BlockSpec((tm, tn), lambda j, i: (0, j))],
        out_shape=[jax.ShapeDtypeStruct((N_P, D_MODEL), F32),
                   jax.ShapeDtypeStruct((N_S, D_MODEL), F32)],
        scratch_shapes=[pltpu.VMEM((D_FF, tn), BF16)],
        compiler_params=_params(2),
        name="ffn_down",
    )(act, w_down, y1)


def _rope_tables():
    inv = ROPE_THETA ** (-jnp.arange(0, HEAD_DIM, 2, dtype=F32) / HEAD_DIM)
    pos = jnp.concatenate([jnp.arange(N_P, dtype=jnp.int32),
                           PAST_LEN + (jnp.arange(N_S, dtype=jnp.int32) % DEC_SEQ)])
    ang = pos.astype(F32)[:, None] * inv[None, :]
    cos = jnp.cos(ang)
    sin = jnp.sin(ang)
    return (jnp.concatenate([cos, cos, cos, cos], axis=-1),
            jnp.concatenate([-sin, sin, -sin, sin], axis=-1))


def kernel(x_prompt, x_sample, cache_k_win, cache_v_win, state_conv, norm_mix, w_in, q_norm, k_norm,
           sinks, conv_w, w_out, norm_ffn, w_gate_up, w_down):
    assert x_prompt.shape == (1, SEQ, D_MODEL) and x_sample.shape == (DEC_BATCH, DEC_SEQ, D_MODEL)
    assert w_in.shape == (1, D_MODEL, D_IN) and cache_k_win.shape == (1, DEC_BATCH, WINDOW, N_KV_HEADS, HEAD_DIM)

    x_p = x_prompt.reshape(N_P, D_MODEL)
    x_s = x_sample.reshape(N_S, D_MODEL)
    w_in2 = w_in.reshape(D_MODEL, D_IN)
    w_out2 = w_out.reshape(D_ATTN, D_MODEL)
    w_gu2 = w_gate_up.reshape(D_MODEL, 2 * D_FF)
    w_dn2 = w_down.reshape(D_FF, D_MODEL)
    conv_w2 = conv_w.reshape(CONV_WIDTH, D_CONV)
    cache_k = cache_k_win.reshape(DEC_BATCH, WINDOW, D_KV)
    cache_v = cache_v_win.reshape(DEC_BATCH, WINDOW, D_KV)

    cos_t, sin_t = _rope_tables()
    q_g128 = jnp.tile(q_norm.reshape(1, HEAD_DIM), (1, LANES // HEAD_DIM))
    k_g128 = jnp.tile(k_norm.reshape(1, HEAD_DIM), (1, LANES // HEAD_DIM))
    head_of_lane = jnp.arange(LANES) // HEAD_DIM
    bd = jnp.where(head_of_lane[:, None] == head_of_lane[None, :], 1.0 / HEAD_DIM, 0.0).astype(BF16)
    sinks2 = sinks.reshape(1, N_HEADS).astype(F32)
    sink_col = jnp.broadcast_to(jnp.tile(sinks2.reshape(N_HEADS), DEC_SEQ)[:, None],
                                (DEC_SEQ * N_HEADS, LANES))
    rep = jnp.tile(jnp.eye(HEAD_DIM, dtype=BF16), (1, N_KV_HEADS))
    st_rows = jnp.pad(state_conv.reshape(DEC_BATCH, CONV_WIDTH - 1, D_CONV),
                      ((0, 0), (0, DEC_SEQ - (CONV_WIDTH - 1)), (0, 0))).reshape(N_S, D_CONV)

    xn = _rmsnorm(x_p, norm_mix.reshape(1, D_MODEL), N_ALL, 0)
    xn = _rmsnorm(x_s, norm_mix.reshape(1, D_MODEL), N_ALL, N_P // ROW_TILE_SMALL, into=xn)
    q, kv, rest = _in_proj(xn, w_in2, q_g128, k_g128, cos_t, sin_t, bd)

    mix, tail = _prompt_mix(sinks2, q, kv, rest, conv_w2)
    q_s = q[N_P:].reshape(DEC_BATCH, DEC_SEQ * N_HEADS, HEAD_DIM)
    attn_s, k_win, v_win = _sample_attn(q_s, kv, cache_k, cache_v, sink_col, rep)
    mix, u_s = _sample_merge(attn_s.reshape(N_S, D_ATTN), rest, st_rows, conv_w2, mix)

    y1 = _out_proj(mix, w_out2, x_p, x_s)
    yn = _rmsnorm(y1, norm_ffn.reshape(1, D_MODEL), N_ALL, 0)
    act = _ffn_up(yn, w_gu2)
    y_p, y_s = _ffn_down(act, w_dn2, y1)

    kv_tail = kv[N_P - WINDOW:N_P]
    return (y_p.reshape(1, SEQ, D_MODEL),
            y_s.reshape(DEC_BATCH, DEC_SEQ, D_MODEL),
            kv_tail[:, :D_KV].reshape(1, 1, WINDOW, N_KV_HEADS, HEAD_DIM),
            kv_tail[:, D_KV:].reshape(1, 1, WINDOW, N_KV_HEADS, HEAD_DIM),
            tail[8 - (CONV_WIDTH - 1):].reshape(1, 1, CONV_WIDTH - 1, D_CONV),
            k_win.reshape(1, DEC_BATCH, WINDOW, N_KV_HEADS, HEAD_DIM),
            v_win.reshape(1, DEC_BATCH, WINDOW, N_KV_HEADS, HEAD_DIM),
            u_s.reshape(DEC_BATCH, DEC_SEQ, D_CONV)[:, DEC_SEQ - (CONV_WIDTH - 1):][None])
```

```python
import math

import numpy as np
import jax
import jax.numpy as jnp
from jax import lax
from jax.experimental import pallas as pl
from jax.experimental.pallas import tpu as pltpu

F32 = jnp.float32
BF16 = jnp.bfloat16

D_MODEL = 2048
SEQ = 8192
DEC_BATCH = 128
DEC_SEQ = 4
PAST_LEN = 8192
N_HEADS = 32
HEAD_DIM = 64
N_KV_HEADS = 4
GROUP = N_HEADS // N_KV_HEADS
D_ATTN = N_HEADS * HEAD_DIM
D_KV = N_KV_HEADS * HEAD_DIM
WINDOW = 128
D_CONV = D_MODEL
CONV_WIDTH = 3
D_FF = 5632
ROPE_THETA = 10000.0
EPS = 1e-6
NEG_INF = -1e30
D_IN = 2 * D_ATTN + 2 * D_KV + 3 * D_CONV + D_CONV
D_REST = 5 * D_MODEL
LOG2E = math.log2(math.e)
Q_SCALE = HEAD_DIM ** -0.5 * LOG2E

N_P = SEQ
N_S = DEC_BATCH * DEC_SEQ
N_ALL = N_P + N_S

LANES = 128
VMEM_LIMIT = 56 * 1024 * 1024

ROW_TILE_SMALL = 512
ROW_TILE_BIG = 1088
OUT_PROJ_ROWS = 256
SEQ_BLOCK = 8
PAIRS_PER_GROUP = GROUP // 2


def _params(n_axes):
    return pltpu.CompilerParams(
        dimension_semantics=("arbitrary",) * n_axes, vmem_limit_bytes=VMEM_LIMIT)


def _rmsnorm_rows(x, g):
    ms = jnp.mean(x * x, axis=-1, keepdims=True)
    return x * lax.rsqrt(ms + EPS) * g


def _cast_weight_once(w_ref, wbf_ref, axis):
    @pl.when(pl.program_id(axis) == 0)
    def _():
        wbf_ref[...] = w_ref[...].astype(BF16)


def _headnorm_rope(zc, g128, cos_t, sin_t, bd):
    sq = zc * zc
    hi = sq.astype(BF16)
    lo = (sq - hi.astype(F32)).astype(BF16)
    ms = (jnp.dot(hi, bd, preferred_element_type=F32)
          + jnp.dot(lo, bd, preferred_element_type=F32))
    y = zc * lax.rsqrt(ms + EPS) * g128
    lane = lax.broadcasted_iota(jnp.int32, y.shape, 1)
    first_half = (lane & (HEAD_DIM // 2)) == 0
    partner = jnp.where(first_half,
                        pltpu.roll(y, LANES - HEAD_DIM // 2, 1),
                        pltpu.roll(y, HEAD_DIM // 2, 1))
    return y * cos_t + partner * sin_t


def _norm_proj_kv_kernel(xp_ref, xs_ref, gm_ref, w_ref, g_ref, cos_ref, sin_ref, bd_ref,
                         xn_ref, kv_ref, wbf_ref):
    _cast_weight_once(w_ref, wbf_ref, 0)
    is_prompt = pl.program_id(0) < N_P // ROW_TILE_SMALL
    x = jnp.where(is_prompt, xp_ref[...], xs_ref[...])
    xn = _rmsnorm_rows(x, gm_ref[...]).astype(BF16)
    xn_ref[...] = xn
    z = jnp.dot(xn, wbf_ref[...], preferred_element_type=F32)
    g128 = g_ref[...]
    cos_t = cos_ref[...]
    sin_t = sin_ref[...]
    bd = bd_ref[...]
    for c in range(D_KV // LANES):
        zc = z[:, c * LANES:(c + 1) * LANES]
        kv_ref[:, c * LANES:(c + 1) * LANES] = _headnorm_rope(zc, g128, cos_t, sin_t, bd)
    kv_ref[:, D_KV:] = z[:, D_KV:]


def _norm_proj_kv(x_p, x_s, g_mix, w_in, k_g128, cos_t, sin_t, bd):
    tm = ROW_TILE_SMALL
    tn = 2 * D_KV
    last_p = N_P // tm - 1
    const = lambda shape: pl.BlockSpec(shape, lambda i: (0, 0))
    return pl.pallas_call(
        _norm_proj_kv_kernel,
        grid=(N_ALL // tm,),
        in_specs=[pl.BlockSpec((tm, D_MODEL), lambda i: (jnp.minimum(i, last_p), 0)),
                  const((tm, D_MODEL)),
                  const((1, D_MODEL)),
                  pl.BlockSpec((D_MODEL, tn), lambda i: (0, D_ATTN // tn)),
                  const((1, LANES)),
                  pl.BlockSpec((tm, LANES), lambda i: (i, 0)),
                  pl.BlockSpec((tm, LANES), lambda i: (i, 0)),
                  const((LANES, LANES))],
        out_specs=[pl.BlockSpec((tm, D_MODEL), lambda i: (i, 0)),
                   pl.BlockSpec((tm, tn), lambda i: (i, 0))],
        out_shape=[jax.ShapeDtypeStruct((N_ALL, D_MODEL), BF16),
                   jax.ShapeDtypeStruct((N_ALL, tn), F32)],
        scratch_shapes=[pltpu.VMEM((D_MODEL, tn), BF16)],
        compiler_params=_params(1),
        name="norm_proj_kv",
    )(x_p, x_s, g_mix, w_in, k_g128, cos_t, sin_t, bd)


def _proj_plain_kernel(x_ref, w_ref, o_ref, wbf_ref):
    _cast_weight_once(w_ref, wbf_ref, 1)
    o_ref[...] = jnp.dot(x_ref[...], wbf_ref[...],
                         preferred_element_type=F32).astype(o_ref.dtype)


def _proj_q_kernel(x_ref, w_ref, g_ref, cos_ref, sin_ref, bd_ref, o_ref, wbf_ref):
    _cast_weight_once(w_ref, wbf_ref, 1)
    z = jnp.dot(x_ref[...], wbf_ref[...], preferred_element_type=F32)
    g128 = g_ref[...]
    cos_t = cos_ref[...]
    sin_t = sin_ref[...]
    bd = bd_ref[...]
    for c in range(z.shape[1] // LANES):
        zc = z[:, c * LANES:(c + 1) * LANES]
        out = _headnorm_rope(zc, g128, cos_t, sin_t, bd)
        o_ref[:, c * LANES:(c + 1) * LANES] = out.astype(o_ref.dtype)


def _proj_q_rest(xn, w_in, q_g128, cos_q, sin_q, bd):
    tm = ROW_TILE_BIG
    n_row = N_ALL // tm
    k = D_MODEL
    x_spec = pl.BlockSpec((tm, k), lambda j, i: (i, 0))
    rope_specs = [pl.BlockSpec((1, LANES), lambda j, i: (0, 0)),
                  pl.BlockSpec((tm, LANES), lambda j, i: (i, 0)),
                  pl.BlockSpec((tm, LANES), lambda j, i: (i, 0)),
                  pl.BlockSpec((LANES, LANES), lambda j, i: (0, 0))]

    tn_q = 1024
    q = pl.pallas_call(
        _proj_q_kernel,
        grid=(D_ATTN // tn_q, n_row),
        in_specs=[x_spec, pl.BlockSpec((k, tn_q), lambda j, i: (0, j))] + rope_specs,
        out_specs=pl.BlockSpec((tm, tn_q), lambda j, i: (i, j)),
        out_shape=jax.ShapeDtypeStruct((N_ALL, D_ATTN), BF16),
        scratch_shapes=[pltpu.VMEM((k, tn_q), BF16)],
        compiler_params=_params(2),
        name="proj_q",
    )(xn, w_in, q_g128, cos_q, sin_q, bd)

    tn_r = 1280
    rest_start = D_ATTN + 2 * D_KV
    rest = pl.pallas_call(
        _proj_plain_kernel,
        grid=(D_REST // tn_r, n_row),
        in_specs=[x_spec, pl.BlockSpec((k, tn_r), lambda j, i: (0, j + rest_start // tn_r))],
        out_specs=pl.BlockSpec((tm, tn_r), lambda j, i: (i, j)),
        out_shape=jax.ShapeDtypeStruct((N_ALL, D_REST), BF16),
        scratch_shapes=[pltpu.VMEM((k, tn_r), BF16)],
        compiler_params=_params(2),
        name="proj_rest",
    )(xn, w_in)
    return q, rest


def _sigmoid(x):
    return 0.5 * jnp.tanh(0.5 * x) + 0.5


def _gated_merge(attn, u, u1, u2, b_ref, ga_ref, gc_ref, cw_ref):
    conv = cw_ref[0:1, :] * u2 + cw_ref[1:2, :] * u1 + cw_ref[2:3, :] * u
    ga = ga_ref[...].astype(F32)
    gc = gc_ref[...].astype(F32)
    b = b_ref[...].astype(F32)
    return _sigmoid(ga) * attn + _sigmoid(gc) * (b * conv)


def _prompt_mix_kernel(sinks_ref, q_ref, kvp_ref, kvc_ref, h_ref, b_ref, c_ref, ga_ref, gc_ref,
                       cw_ref, mix_ref, tail_ref, tail_scr, attn_scr):
    j = pl.program_id(0)

    @pl.when(j == 0)
    def _():
        tail_scr[...] = jnp.zeros_like(tail_scr)

    n_keys = 2 * WINDOW
    kv = jnp.concatenate([kvp_ref[...], kvc_ref[...]], axis=0)
    key_row = lax.broadcasted_iota(jnp.int32, (n_keys, LANES), 0)
    low = lax.broadcasted_iota(jnp.int32, (n_keys, LANES), 1) < HEAD_DIM
    low_v = low & (key_row > 0)
    high_v = jnp.logical_not(low) & (key_row > 0)
    ones_low = jnp.where(low, 1.0, 0.0)
    ones_high = jnp.where(low, 0.0, 1.0)

    r = lax.broadcasted_iota(jnp.int32, (WINDOW, n_keys), 0)
    c = lax.broadcasted_iota(jnp.int32, (WINDOW, n_keys), 1)
    first_key = jnp.where(j > 0, 0, WINDOW)
    bias = jnp.where((c > r) & (c <= r + WINDOW) & (c >= first_key), 0.0, NEG_INF)
    sink_lane = lax.broadcasted_iota(jnp.int32, (WINDOW, LANES), 1) == 0

    for g in range(N_KV_HEADS):
        chunk = (g // 2) * LANES
        kc = kv[:, chunk:chunk + LANES]
        vc = kv[:, D_KV + chunk:D_KV + chunk + LANES]
        if g % 2 == 0:
            k_low, v_low = jnp.where(low, kc, 0.0), jnp.where(low_v, vc, 0.0)
            k_high, v_high = pltpu.roll(k_low, HEAD_DIM, 1), pltpu.roll(v_low, HEAD_DIM, 1)
        else:
            k_high, v_high = jnp.where(low, 0.0, kc), jnp.where(high_v, vc, 0.0)
            k_low, v_low = pltpu.roll(k_high, HEAD_DIM, 1), pltpu.roll(v_high, HEAD_DIM, 1)
        k_bd = jnp.concatenate([k_low, k_high], axis=0).astype(BF16)
        v_bd = jnp.concatenate(
            [jnp.concatenate([v_low, ones_low], axis=1),
             jnp.concatenate([v_high, ones_high], axis=1)], axis=0).astype(BF16)

        for k in range(PAIRS_PER_GROUP):
            pair = g * PAIRS_PER_GROUP + k
            cols = slice(pair * LANES, (pair + 1) * LANES)
            s = lax.dot_general(q_ref[:, cols], k_bd, (((1,), (1,)), ((), ())),
                                preferred_element_type=F32)
            halves = []
            for half in range(2):
                sh = s[:, half * n_keys:(half + 1) * n_keys] + bias
                sink = sinks_ref[0, 2 * pair + half] * LOG2E
                sh = jnp.concatenate([jnp.where(sink_lane, sink, sh[:, :LANES]), sh[:, LANES:]], axis=1)
                halves.append(jnp.exp2(sh - jnp.max(sh, axis=1, keepdims=True)))
            p = jnp.concatenate(halves, axis=1).astype(BF16)
            o = jnp.dot(p, v_bd, preferred_element_type=F32)
            attn_scr[:, cols] = o[:, :LANES] / o[:, LANES:]

    u = c_ref[...].astype(F32) * h_ref[...].astype(F32)
    r1 = pltpu.roll(u, 1, 0)
    r2 = pltpu.roll(u, 2, 0)
    row8 = lax.broadcasted_iota(jnp.int32, (8, D_CONV), 0)
    t_last = tail_scr[7:8, :]
    t_prev = tail_scr[6:7, :]
    u1 = jnp.concatenate([jnp.where(row8 == 0, t_last, r1[:8]), r1[8:]], axis=0)
    u2 = jnp.concatenate(
        [jnp.where(row8 == 0, t_prev, jnp.where(row8 == 1, t_last, r2[:8])), r2[8:]], axis=0)
    mix = _gated_merge(attn_scr[...], u, u1, u2, b_ref, ga_ref, gc_ref, cw_ref)
    mix_ref[...] = mix.astype(mix_ref.dtype)
    tail = u[WINDOW - 8:, :]
    tail_scr[...] = tail
    tail_ref[...] = tail


def _prompt_mix(sinks, q, kv, rest, conv_w):
    nb = N_P // WINDOW
    blk = lambda c: pl.BlockSpec((WINDOW, D_MODEL), lambda j, c=c: (j, c))
    return pl.pallas_call(
        _prompt_mix_kernel,
        grid=(nb,),
        in_specs=[pl.BlockSpec(memory_space=pltpu.SMEM),
                  pl.BlockSpec((WINDOW, D_ATTN), lambda j: (j, 0)),
                  pl.BlockSpec((WINDOW, 2 * D_KV), lambda j: (jnp.maximum(j - 1, 0), 0)),
                  pl.BlockSpec((WINDOW, 2 * D_KV), lambda j: (j, 0)),
                  blk(0), blk(1), blk(2), blk(3), blk(4),
                  pl.BlockSpec((CONV_WIDTH, D_CONV), lambda j: (0, 0))],
        out_specs=[pl.BlockSpec((WINDOW, D_MODEL), lambda j: (j, 0)),
                   pl.BlockSpec((8, D_CONV), lambda j: (0, 0))],
        out_shape=[jax.ShapeDtypeStruct((N_ALL, D_MODEL), BF16),
                   jax.ShapeDtypeStruct((8, D_CONV), F32)],
        scratch_shapes=[pltpu.VMEM((8, D_CONV), F32), pltpu.VMEM((WINDOW, D_ATTN), F32)],
        compiler_params=_params(1),
        name="prompt_mix",
    )(sinks, q, kv, kv, rest, rest, rest, rest, rest, conv_w)


def _sample_attn_kernel(q_ref, kvn_ref, ckt_ref, cvt_ref, sink_ref, rep_ref,
                        o_ref, kwt_ref, vwt_ref):
    n_new = SEQ_BLOCK * DEC_SEQ
    rows = DEC_SEQ * N_HEADS
    keep = WINDOW - DEC_SEQ
    kvn = jnp.concatenate([kvn_ref[...], jnp.zeros((WINDOW - n_new, 2 * D_KV), F32)], axis=0)
    kn_t = kvn[:, :D_KV].T
    vn_t = kvn[:, D_KV:].T
    kn_t_bf = kn_t.astype(BF16)
    vn_bf = kvn[:, D_KV:].astype(BF16)

    row = lax.broadcasted_iota(jnp.int32, (rows, WINDOW), 0)
    col = lax.broadcasted_iota(jnp.int32, (rows, WINDOW), 1)
    t = row >> 5
    valid_cache = col > t
    row2 = lax.broadcasted_iota(jnp.int32, (rows, D_KV), 0)
    col2 = lax.broadcasted_iota(jnp.int32, (rows, D_KV), 1)
    same_group = ((row2 & (N_HEADS - 1)) >> 3) == (col2 >> 6)
    win_lane = lax.broadcasted_iota(jnp.int32, (D_KV, WINDOW), 1)
    sink = sink_ref[:, 0:1]
    rep = rep_ref[...]

    for s in range(SEQ_BLOCK):
        k_t = ckt_ref[s]
        v_t = cvt_ref[s]
        q4 = jnp.dot(q_ref[s], rep, preferred_element_type=F32)
        q4 = jnp.where(same_group, q4, 0.0).astype(BF16)
        s_c = jnp.dot(q4, k_t.astype(BF16), preferred_element_type=F32)
        s_n = jnp.dot(q4, kn_t_bf, preferred_element_type=F32)
        valid_new = (col < n_new) & ((col >> 2) == s) & ((col & 3) <= t)
        s_c = jnp.where(valid_cache, s_c, NEG_INF)
        s_n = jnp.where(valid_new, s_n, NEG_INF)
        m = jnp.maximum(jnp.maximum(jnp.max(s_c, axis=1, keepdims=True),
                                    jnp.max(s_n, axis=1, keepdims=True)), sink)
        p_c = jnp.exp2(s_c - m)
        p_n = jnp.exp2(s_n - m)
        l = (jnp.sum(p_c, axis=1, keepdims=True) + jnp.sum(p_n, axis=1, keepdims=True)
             + jnp.exp2(sink - m))
        o4 = (lax.dot_general(p_c.astype(BF16), v_t.astype(BF16), (((1,), (1,)), ((), ())),
                              preferred_element_type=F32)
              + jnp.dot(p_n.astype(BF16), vn_bf, preferred_element_type=F32))
        o4 = jnp.where(same_group, o4, 0.0)
        a = o4[:, 0:LANES] + o4[:, LANES:]
        o = a + pltpu.roll(a, HEAD_DIM, 1)
        o_ref[s] = o[:, 0:HEAD_DIM] / l

        new_shift = keep - s * DEC_SEQ
        kwt_ref[s] = jnp.where(win_lane >= keep, pltpu.roll(kn_t, new_shift, 1),
                               pltpu.roll(k_t, keep, 1))
        vwt_ref[s] = jnp.where(win_lane >= keep, pltpu.roll(vn_t, new_shift, 1),
                               pltpu.roll(v_t, keep, 1))


def _sample_attn(q_s, kv, cache_kt, cache_vt, sink_col, rep):
    rows = DEC_SEQ * N_HEADS
    n_new = SEQ_BLOCK * DEC_SEQ
    cache_spec = pl.BlockSpec((SEQ_BLOCK, D_KV, WINDOW), lambda i: (i, 0, 0))
    return pl.pallas_call(
        _sample_attn_kernel,
        grid=(DEC_BATCH // SEQ_BLOCK,),
        in_specs=[pl.BlockSpec((SEQ_BLOCK, rows, HEAD_DIM), lambda i: (i, 0, 0)),
                  pl.BlockSpec((n_new, 2 * D_KV), lambda i: (i + N_P // n_new, 0)),
                  cache_spec, cache_spec,
                  pl.BlockSpec((rows, LANES), lambda i: (0, 0)),
                  pl.BlockSpec((HEAD_DIM, D_KV), lambda i: (0, 0))],
        out_specs=[pl.BlockSpec((SEQ_BLOCK, rows, HEAD_DIM), lambda i: (i, 0, 0)),
                   cache_spec, cache_spec],
        out_shape=[jax.ShapeDtypeStruct((DEC_BATCH, rows, HEAD_DIM), F32),
                   jax.ShapeDtypeStruct((DEC_BATCH, D_KV, WINDOW), F32),
                   jax.ShapeDtypeStruct((DEC_BATCH, D_KV, WINDOW), F32)],
        compiler_params=_params(1),
        name="sample_attn",
    )(q_s, kv, cache_kt, cache_vt, sink_col, rep)


def _sample_merge_kernel(attn_ref, h_ref, b_ref, c_ref, ga_ref, gc_ref, st_ref, cw_ref, buf_ref,
                         mix_ref, u_ref):
    del buf_ref
    u = c_ref[...].astype(F32) * h_ref[...].astype(F32)
    u_ref[...] = u
    rows = u.shape[0]
    t = lax.broadcasted_iota(jnp.int32, u.shape, 0) & (DEC_SEQ - 1)
    st = st_ref[...]
    u1 = jnp.where(t == 0, pltpu.roll(st, rows - 1, 0), pltpu.roll(u, 1, 0))
    u2 = jnp.where(t < 2, st, pltpu.roll(u, 2, 0))
    mix = _gated_merge(attn_ref[...], u, u1, u2, b_ref, ga_ref, gc_ref, cw_ref)
    mix_ref[...] = mix.astype(mix_ref.dtype)


def _sample_merge(attn_s, rest, st_rows, conv_w, mix_buf):
    tm = WINDOW
    off = N_P // tm
    blk = lambda c: pl.BlockSpec((tm, D_MODEL), lambda i, c=c: (i + off, c))
    return pl.pallas_call(
        _sample_merge_kernel,
        grid=(N_S // tm,),
        in_specs=[pl.BlockSpec((tm, D_ATTN), lambda i: (i, 0)),
                  blk(0), blk(1), blk(2), blk(3), blk(4),
                  pl.BlockSpec((tm, D_CONV), lambda i: (i, 0)),
                  pl.BlockSpec((CONV_WIDTH, D_CONV), lambda i: (0, 0)),
                  pl.BlockSpec(memory_space=pl.ANY)],
        out_specs=[pl.BlockSpec((tm, D_MODEL), lambda i: (i + off, 0)),
                   pl.BlockSpec((tm, D_CONV), lambda i: (i, 0))],
        out_shape=[jax.ShapeDtypeStruct((N_ALL, D_MODEL), BF16),
                   jax.ShapeDtypeStruct((N_S, D_CONV), F32)],
        input_output_aliases={8: 0},
        compiler_params=_params(1),
        name="sample_merge",
    )(attn_s, rest, rest, rest, rest, rest, st_rows, conv_w, mix_buf)


def _out_proj_kernel(m_ref, w_ref, xp_ref, xs_ref, gf_ref, y_ref, yn_ref, wbf_ref):
    _cast_weight_once(w_ref, wbf_ref, 0)
    is_prompt = pl.program_id(0) < N_P // OUT_PROJ_ROWS
    res = jnp.where(is_prompt, xp_ref[...], xs_ref[...])
    y = res + jnp.dot(m_ref[...], wbf_ref[...], preferred_element_type=F32)
    y_ref[...] = y
    yn_ref[...] = _rmsnorm_rows(y, gf_ref[...]).astype(yn_ref.dtype)


def _out_proj(mix, w_out, x_p, x_s, g_ffn):
    tm = OUT_PROJ_ROWS
    n_p = N_P // tm
    once = pl.Buffered(1)
    return pl.pallas_call(
        _out_proj_kernel,
        grid=(N_ALL // tm,),
        in_specs=[pl.BlockSpec((tm, D_ATTN), lambda i: (i, 0)),
                  pl.BlockSpec((D_ATTN, D_MODEL), lambda i: (0, 0), pipeline_mode=once),
                  pl.BlockSpec((tm, D_MODEL), lambda i: (jnp.minimum(i, n_p - 1), 0)),
                  pl.BlockSpec((tm, D_MODEL), lambda i: (jnp.maximum(i - n_p, 0), 0)),
                  pl.BlockSpec((1, D_MODEL), lambda i: (0, 0))],
        out_specs=[pl.BlockSpec((tm, D_MODEL), lambda i: (i, 0)),
                   pl.BlockSpec((tm, D_MODEL), lambda i: (i, 0))],
        out_shape=[jax.ShapeDtypeStruct((N_ALL, D_MODEL), F32),
                   jax.ShapeDtypeStruct((N_ALL, D_MODEL), BF16)],
        scratch_shapes=[pltpu.VMEM((D_ATTN, D_MODEL), BF16)],
        compiler_params=_params(1),
        name="out_proj",
    )(mix, w_out, x_p, x_s, g_ffn)


def _ffn_up_kernel(x_ref, wg_ref, wu_ref, o_ref, wg_bf, wu_bf):
    @pl.when(pl.program_id(1) == 0)
    def _():
        wg_bf[...] = wg_ref[...].astype(BF16)
        wu_bf[...] = wu_ref[...].astype(BF16)

    x = x_ref[...]
    g = jnp.dot(x, wg_bf[...], preferred_element_type=F32)
    u = jnp.dot(x, wu_bf[...], preferred_element_type=F32)
    o_ref[...] = ((g * jax.nn.sigmoid(g)) * u).astype(o_ref.dtype)


def _ffn_up(yn, w_gate_up):
    tm = ROW_TILE_BIG
    tn = 512
    nt = D_FF // tn
    return pl.pallas_call(
        _ffn_up_kernel,
        grid=(nt, N_ALL // tm),
        in_specs=[pl.BlockSpec((tm, D_MODEL), lambda j, i: (i, 0)),
                  pl.BlockSpec((D_MODEL, tn), lambda j, i: (0, j)),
                  pl.BlockSpec((D_MODEL, tn), lambda j, i: (0, j + nt))],
        out_specs=pl.BlockSpec((tm, tn), lambda j, i: (i, j)),
        out_shape=jax.ShapeDtypeStruct((N_ALL, D_FF), BF16),
        scratch_shapes=[pltpu.VMEM((D_MODEL, tn), BF16), pltpu.VMEM((D_MODEL, tn), BF16)],
        compiler_params=_params(2),
        name="ffn_up",
    )(yn, w_gate_up, w_gate_up)


def _ffn_down_kernel(a_ref, w_ref, r_ref, yp_ref, ys_ref, wbf_ref):
    _cast_weight_once(w_ref, wbf_ref, 1)
    i = pl.program_id(1)
    n_prompt_tiles = N_P // ROW_TILE_SMALL
    y = r_ref[...] + jnp.dot(a_ref[...], wbf_ref[...], preferred_element_type=F32)

    @pl.when(i < n_prompt_tiles)
    def _():
        yp_ref[...] = y

    @pl.when(i >= n_prompt_tiles)
    def _():
        ys_ref[...] = y


def _ffn_down(act, w_down, y1):
    tm = ROW_TILE_SMALL
    tn = 512
    last_p = N_P // tm - 1
    return pl.pallas_call(
        _ffn_down_kernel,
        grid=(D_MODEL // tn, N_ALL // tm),
        in_specs=[pl.BlockSpec((tm, D_FF), lambda j, i: (i, 0)),
                  pl.BlockSpec((D_FF, tn), lambda j, i: (0, j)),
                  pl.BlockSpec((tm, tn), lambda j, i: (i, j))],
        out_specs=[pl.BlockSpec((tm, tn), lambda j, i: (jnp.minimum(i, last_p), j)),
                   pl.BlockSpec((tm, tn), lambda j, i: (0, j))],
        out_shape=[jax.ShapeDtypeStruct((N_P, D_MODEL), F32),
                   jax.ShapeDtypeStruct((N_S, D_MODEL), F32)],
        scratch_shapes=[pltpu.VMEM((D_FF, tn), BF16)],
        compiler_params=_params(2),
        name="ffn_down",
    )(act, w_down, y1)


def _rope_tables():
    inv = (np.float32(ROPE_THETA) ** (-np.arange(0, HEAD_DIM, 2, dtype=np.float32) / HEAD_DIM)
           ).astype(np.float32)
    pos = np.concatenate([np.arange(N_P), PAST_LEN + (np.arange(N_S) % DEC_SEQ)]).astype(np.float32)
    ang = (pos[:, None] * inv[None, :]).astype(np.float32)
    cos = np.cos(ang).astype(np.float32)
    sin = np.sin(ang).astype(np.float32)
    cos_t = np.concatenate([cos, cos, cos, cos], axis=-1)
    sin_t = np.concatenate([-sin, sin, -sin, sin], axis=-1)
    return cos_t, sin_t


def kernel(x_prompt, x_sample, cache_k_win, cache_v_win, state_conv, norm_mix, w_in, q_norm, k_norm,
           sinks, conv_w, w_out, norm_ffn, w_gate_up, w_down):
    assert x_prompt.shape == (1, SEQ, D_MODEL) and x_sample.shape == (DEC_BATCH, DEC_SEQ, D_MODEL)
    assert w_in.shape == (1, D_MODEL, D_IN)
    assert cache_k_win.shape == (1, DEC_BATCH, WINDOW, N_KV_HEADS, HEAD_DIM)

    x_p = x_prompt.reshape(N_P, D_MODEL)
    x_s = x_sample.reshape(N_S, D_MODEL)
    w_in2 = w_in.reshape(D_MODEL, D_IN)
    w_out2 = w_out.reshape(D_ATTN, D_MODEL)
    w_gu2 = w_gate_up.reshape(D_MODEL, 2 * D_FF)
    w_dn2 = w_down.reshape(D_FF, D_MODEL)
    conv_w2 = conv_w.reshape(CONV_WIDTH, D_CONV)
    cache_kt = jnp.transpose(cache_k_win.reshape(DEC_BATCH, WINDOW, D_KV), (0, 2, 1))
    cache_vt = jnp.transpose(cache_v_win.reshape(DEC_BATCH, WINDOW, D_KV), (0, 2, 1))

    cos_np, sin_np = _rope_tables()
    cos_k, sin_k = jnp.asarray(cos_np), jnp.asarray(sin_np)
    cos_q, sin_q = jnp.asarray(cos_np * np.float32(Q_SCALE)), jnp.asarray(sin_np * np.float32(Q_SCALE))
    q_g128 = jnp.tile(q_norm.reshape(1, HEAD_DIM), (1, LANES // HEAD_DIM))
    k_g128 = jnp.tile(k_norm.reshape(1, HEAD_DIM), (1, LANES // HEAD_DIM))
    head_of_lane = np.arange(LANES) // HEAD_DIM
    bd = jnp.asarray(np.where(head_of_lane[:, None] == head_of_lane[None, :], 1.0 / HEAD_DIM, 0.0),
                     dtype=BF16)
    sinks2 = sinks.reshape(1, N_HEADS).astype(F32)
    sink_col = jnp.broadcast_to(jnp.tile(sinks2.reshape(N_HEADS) * LOG2E, DEC_SEQ)[:, None],
                                (DEC_SEQ * N_HEADS, LANES))
    rep = jnp.asarray(np.tile(np.eye(HEAD_DIM), (1, N_KV_HEADS)), dtype=BF16)
    st_rows = jnp.pad(state_conv.reshape(DEC_BATCH, CONV_WIDTH - 1, D_CONV),
                      ((0, 0), (0, DEC_SEQ - (CONV_WIDTH - 1)), (0, 0))).reshape(N_S, D_CONV)

    xn, kv = _norm_proj_kv(x_p, x_s, norm_mix.reshape(1, D_MODEL), w_in2, k_g128, cos_k, sin_k, bd)
    q, rest = _proj_q_rest(xn, w_in2, q_g128, cos_q, sin_q, bd)

    mix, tail = _prompt_mix(sinks2, q, kv, rest, conv_w2)
    q_s = q[N_P:].reshape(DEC_BATCH, DEC_SEQ * N_HEADS, HEAD_DIM)
    attn_s, kwt, vwt = _sample_attn(q_s, kv, cache_kt, cache_vt, sink_col, rep)
    mix, u_s = _sample_merge(attn_s.reshape(N_S, D_ATTN), rest, st_rows, conv_w2, mix)

    y1, yn = _out_proj(mix, w_out2, x_p, x_s, norm_ffn.reshape(1, D_MODEL))
    act = _ffn_up(yn, w_gu2)
    y_p, y_s = _ffn_down(act, w_dn2, y1)

    kv_tail = kv[N_P - WINDOW:N_P]
    win_shape = (1, DEC_BATCH, WINDOW, N_KV_HEADS, HEAD_DIM)
    return (y_p.reshape(1, SEQ, D_MODEL),
            y_s.reshape(DEC_BATCH, DEC_SEQ, D_MODEL),
            kv_tail[:, :D_KV].reshape(1, 1, WINDOW, N_KV_HEADS, HEAD_DIM),
            kv_tail[:, D_KV:].reshape(1, 1, WINDOW, N_KV_HEADS, HEAD_DIM),
            tail[8 - (CONV_WIDTH - 1):].reshape(1, 1, CONV_WIDTH - 1, D_CONV),
            jnp.transpose(kwt, (0, 2, 1)).reshape(win_shape),
            jnp.transpose(vwt, (0, 2, 1)).reshape(win_shape),
            u_s.reshape(DEC_BATCH, DEC_SEQ, D_CONV)[:, DEC_SEQ - (CONV_WIDTH - 1):][None])
```

```python
import math

import numpy as np
import jax
import jax.numpy as jnp
from jax import lax
from jax.experimental import pallas as pl
from jax.experimental.pallas import tpu as pltpu

F32 = jnp.float32
BF16 = jnp.bfloat16

D_MODEL = 2048
SEQ = 8192
DEC_BATCH = 128
DEC_SEQ = 4
PAST_LEN = 8192
N_HEADS = 32
HEAD_DIM = 64
N_KV_HEADS = 4
GROUP = N_HEADS // N_KV_HEADS
D_ATTN = N_HEADS * HEAD_DIM
D_KV = N_KV_HEADS * HEAD_DIM
WINDOW = 128
D_CONV = D_MODEL
CONV_WIDTH = 3
D_FF = 5632
ROPE_THETA = 10000.0
EPS = 1e-6
NEG_INF = -1e30
D_IN = 2 * D_ATTN + 2 * D_KV + 3 * D_CONV + D_CONV
D_REST = 5 * D_MODEL
LOG2E = math.log2(math.e)
Q_SCALE = HEAD_DIM ** -0.5 * LOG2E

N_P = SEQ
N_S = DEC_BATCH * DEC_SEQ
N_ALL = N_P + N_S

LANES = 128
VMEM_LIMIT = 56 * 1024 * 1024

ROW_TILE_SMALL = 512
ROW_TILE_BIG = 1088
FFN_DOWN_ROWS = 256
SEQ_BLOCK = 8
PAIRS_PER_GROUP = GROUP // 2


def _params(n_axes):
    return pltpu.CompilerParams(
        dimension_semantics=("arbitrary",) * n_axes, vmem_limit_bytes=VMEM_LIMIT)


def _rmsnorm_rows(x, g):
    ms = jnp.mean(x * x, axis=-1, keepdims=True)
    return x * lax.rsqrt(ms + EPS) * g


def _cast_weight_once(w_ref, wbf_ref, axis):
    @pl.when(pl.program_id(axis) == 0)
    def _():
        wbf_ref[...] = w_ref[...].astype(BF16)


def _headnorm_rope(zc, g128, cos_t, sin_t, bd):
    sq = zc * zc
    hi = sq.astype(BF16)
    lo = (sq - hi.astype(F32)).astype(BF16)
    ms = (jnp.dot(hi, bd, preferred_element_type=F32)
          + jnp.dot(lo, bd, preferred_element_type=F32))
    y = zc * lax.rsqrt(ms + EPS) * g128
    lane = lax.broadcasted_iota(jnp.int32, y.shape, 1)
    first_half = (lane & (HEAD_DIM // 2)) == 0
    partner = jnp.where(first_half,
                        pltpu.roll(y, LANES - HEAD_DIM // 2, 1),
                        pltpu.roll(y, HEAD_DIM // 2, 1))
    return y * cos_t + partner * sin_t


def _norm_proj_kv_kernel(xp_ref, xs_ref, gm_ref, w_ref, g_ref, cos_ref, sin_ref, bd_ref,
                         xn_ref, kv_ref, wbf_ref):
    _cast_weight_once(w_ref, wbf_ref, 0)
    is_prompt = pl.program_id(0) < N_P // ROW_TILE_SMALL
    x = jnp.where(is_prompt, xp_ref[...], xs_ref[...])
    xn = _rmsnorm_rows(x, gm_ref[...]).astype(BF16)
    xn_ref[...] = xn
    z = jnp.dot(xn, wbf_ref[...], preferred_element_type=F32)
    g128 = g_ref[...]
    cos_t = cos_ref[...]
    sin_t = sin_ref[...]
    bd = bd_ref[...]
    for c in range(D_KV // LANES):
        zc = z[:, c * LANES:(c + 1) * LANES]
        kv_ref[:, c * LANES:(c + 1) * LANES] = _headnorm_rope(zc, g128, cos_t, sin_t, bd)
    kv_ref[:, D_KV:] = z[:, D_KV:]


def _norm_proj_kv(x_p, x_s, g_mix, w_in, k_g128, cos_t, sin_t, bd):
    tm = ROW_TILE_SMALL
    tn = 2 * D_KV
    last_p = N_P // tm - 1
    const = lambda shape: pl.BlockSpec(shape, lambda i: (0, 0))
    return pl.pallas_call(
        _norm_proj_kv_kernel,
        grid=(N_ALL // tm,),
        in_specs=[pl.BlockSpec((tm, D_MODEL), lambda i: (jnp.minimum(i, last_p), 0)),
                  const((tm, D_MODEL)),
                  const((1, D_MODEL)),
                  pl.BlockSpec((D_MODEL, tn), lambda i: (0, D_ATTN // tn)),
                  const((1, LANES)),
                  pl.BlockSpec((tm, LANES), lambda i: (i, 0)),
                  pl.BlockSpec((tm, LANES), lambda i: (i, 0)),
                  const((LANES, LANES))],
        out_specs=[pl.BlockSpec((tm, D_MODEL), lambda i: (i, 0)),
                   pl.BlockSpec((tm, tn), lambda i: (i, 0))],
        out_shape=[jax.ShapeDtypeStruct((N_ALL, D_MODEL), BF16),
                   jax.ShapeDtypeStruct((N_ALL, tn), F32)],
        scratch_shapes=[pltpu.VMEM((D_MODEL, tn), BF16)],
        compiler_params=_params(1),
        name="norm_proj_kv",
    )(x_p, x_s, g_mix, w_in, k_g128, cos_t, sin_t, bd)


def _proj_rest_kernel(x_ref, w_ref, scale_ref, wo_ref, o_ref, wo_bf_ref, wbf_ref):
    @pl.when(pl.program_id(1) == 0)
    def _():
        wbf_ref[...] = w_ref[...].astype(BF16)
        wo_bf_ref[...] = wo_ref[...].astype(BF16)

    z = jnp.dot(x_ref[...], wbf_ref[...], preferred_element_type=F32)
    o_ref[...] = (z * scale_ref[...]).astype(o_ref.dtype)


def _proj_q_kernel(x_ref, w_ref, g_ref, cos_ref, sin_ref, bd_ref, o_ref, wbf_ref):
    _cast_weight_once(w_ref, wbf_ref, 1)
    z = jnp.dot(x_ref[...], wbf_ref[...], preferred_element_type=F32)
    g128 = g_ref[...]
    cos_t = cos_ref[...]
    sin_t = sin_ref[...]
    bd = bd_ref[...]
    for c in range(z.shape[1] // LANES):
        zc = z[:, c * LANES:(c + 1) * LANES]
        out = _headnorm_rope(zc, g128, cos_t, sin_t, bd)
        o_ref[:, c * LANES:(c + 1) * LANES] = out.astype(o_ref.dtype)


def _proj_q_rest(xn, w_in, q_g128, cos_q, sin_q, bd, rest_scale, w_out):
    tm = ROW_TILE_BIG
    n_row = N_ALL // tm
    k = D_MODEL
    x_spec = pl.BlockSpec((tm, k), lambda j, i: (i, 0))
    rope_specs = [pl.BlockSpec((1, LANES), lambda j, i: (0, 0)),
                  pl.BlockSpec((tm, LANES), lambda j, i: (i, 0)),
                  pl.BlockSpec((tm, LANES), lambda j, i: (i, 0)),
                  pl.BlockSpec((LANES, LANES), lambda j, i: (0, 0))]

    tn_q = 1024
    q = pl.pallas_call(
        _proj_q_kernel,
        grid=(D_ATTN // tn_q, n_row),
        in_specs=[x_spec, pl.BlockSpec((k, tn_q), lambda j, i: (0, j))] + rope_specs,
        out_specs=pl.BlockSpec((tm, tn_q), lambda j, i: (i, j)),
        out_shape=jax.ShapeDtypeStruct((N_ALL, D_ATTN), BF16),
        scratch_shapes=[pltpu.VMEM((k, tn_q), BF16)],
        compiler_params=_params(2),
        name="proj_q",
    )(xn, w_in, q_g128, cos_q, sin_q, bd)

    tn_r = 1280
    n_col = D_REST // tn_r
    wo_rows = D_ATTN // n_col
    rest_start = D_ATTN + 2 * D_KV
    rest, w_out_bf = pl.pallas_call(
        _proj_rest_kernel,
        grid=(n_col, n_row),
        in_specs=[x_spec,
                  pl.BlockSpec((k, tn_r), lambda j, i: (0, j + rest_start // tn_r)),
                  pl.BlockSpec((1, tn_r), lambda j, i: (0, j)),
                  pl.BlockSpec((wo_rows, D_MODEL), lambda j, i: (j, 0))],
        out_specs=[pl.BlockSpec((tm, tn_r), lambda j, i: (i, j)),
                   pl.BlockSpec((wo_rows, D_MODEL), lambda j, i: (j, 0))],
        out_shape=[jax.ShapeDtypeStruct((N_ALL, D_REST), BF16),
                   jax.ShapeDtypeStruct((D_ATTN, D_MODEL), BF16)],
        scratch_shapes=[pltpu.VMEM((k, tn_r), BF16)],
        compiler_params=_params(2),
        name="proj_rest",
    )(xn, w_in, rest_scale, w_out)
    return q, rest, w_out_bf


def _gated_merge(attn_half, u, u1, u2, b_ref, ga_half_ref, gc_half_ref, cw_ref):
    cw_half = 0.5 * cw_ref[...]
    conv_half = cw_half[0:1, :] * u2 + cw_half[1:2, :] * u1 + cw_half[2:3, :] * u
    bc_half = b_ref[...].astype(F32) * conv_half
    ta = jnp.tanh(ga_half_ref[...].astype(F32))
    tc = jnp.tanh(gc_half_ref[...].astype(F32))
    return (ta * attn_half + attn_half) + (tc * bc_half + bc_half)


def _prompt_mix_kernel(sinks_ref, q_ref, kvp_ref, kvc_ref, h_ref, b_ref, c_ref, ga_ref, gc_ref,
                       cw_ref, mix_ref, tail_ref, u_scr, attn_scr):
    j = pl.program_id(0)

    @pl.when(j == 0)
    def _():
        u_scr[0:8, :] = jnp.zeros((8, D_CONV), F32)

    n_keys = 2 * WINDOW
    kv = jnp.concatenate([kvp_ref[...], kvc_ref[...]], axis=0)
    key_row = lax.broadcasted_iota(jnp.int32, (n_keys, LANES), 0)
    low = lax.broadcasted_iota(jnp.int32, (n_keys, LANES), 1) < HEAD_DIM
    low_v = low & (key_row > 0)
    high_v = jnp.logical_not(low) & (key_row > 0)
    ones_low = jnp.where(low, 2.0, 0.0)
    ones_high = jnp.where(low, 0.0, 2.0)

    r = lax.broadcasted_iota(jnp.int32, (WINDOW, n_keys), 0)
    c = lax.broadcasted_iota(jnp.int32, (WINDOW, n_keys), 1)
    first_key = jnp.where(j > 0, 0, WINDOW)
    mask = jnp.where((c > r) & (c <= r + WINDOW) & (c >= first_key), 0.0, NEG_INF).astype(BF16)
    mask2 = jnp.concatenate([mask, mask], axis=1)
    eye = (lax.broadcasted_iota(jnp.int32, (WINDOW, WINDOW), 0)
           == lax.broadcasted_iota(jnp.int32, (WINDOW, WINDOW), 1))
    eye = jnp.where(eye, 1.0, 0.0).astype(BF16)
    sink_lane = lax.broadcasted_iota(jnp.int32, (WINDOW, LANES), 1) == 0
    no_keys = jnp.zeros((HEAD_DIM, n_keys), F32)

    for g in range(N_KV_HEADS):
        chunk = (g // 2) * LANES
        kc_t = kv[:, chunk:chunk + LANES].T
        kg_t = kc_t[(g % 2) * HEAD_DIM:(g % 2 + 1) * HEAD_DIM]
        vc = kv[:, D_KV + chunk:D_KV + chunk + LANES]
        if g % 2 == 0:
            v_low = jnp.where(low_v, vc, 0.0)
            v_high = pltpu.roll(v_low, HEAD_DIM, 1)
        else:
            v_high = jnp.where(high_v, vc, 0.0)
            v_low = pltpu.roll(v_high, HEAD_DIM, 1)
        k_bd_t = jnp.concatenate(
            [jnp.concatenate([kg_t, no_keys], axis=1),
             jnp.concatenate([no_keys, kg_t], axis=1)], axis=0).astype(BF16)
        k_aug = jnp.concatenate([k_bd_t, mask2], axis=0)
        v_bd = jnp.concatenate(
            [jnp.concatenate([v_low, ones_low], axis=1),
             jnp.concatenate([v_high, ones_high], axis=1)], axis=0).astype(BF16)

        for k in range(PAIRS_PER_GROUP):
            pair = g * PAIRS_PER_GROUP + k
            cols = slice(pair * LANES, (pair + 1) * LANES)
            q_aug = jnp.concatenate([q_ref[:, cols], eye], axis=1)
            s = jnp.dot(q_aug, k_aug, preferred_element_type=F32)
            halves = []
            for half in range(2):
                sh = s[:, half * n_keys:(half + 1) * n_keys]
                sink = sinks_ref[0, 2 * pair + half] * LOG2E
                sh = jnp.concatenate([jnp.where(sink_lane, sink, sh[:, :LANES]), sh[:, LANES:]], axis=1)
                halves.append(jnp.exp2(sh - jnp.max(sh, axis=1, keepdims=True)))
            p = jnp.concatenate(halves, axis=1).astype(BF16)
            o = jnp.dot(p, v_bd, preferred_element_type=F32)
            attn_scr[:, cols] = o[:, :LANES] / o[:, LANES:]

    u = c_ref[...].astype(F32) * h_ref[...].astype(F32)
    u_scr[8:, :] = u
    u1 = u_scr[7:7 + WINDOW, :]
    u2 = u_scr[6:6 + WINDOW, :]
    mix = _gated_merge(attn_scr[...], u, u1, u2, b_ref, ga_ref, gc_ref, cw_ref)
    mix_ref[...] = mix.astype(mix_ref.dtype)
    tail = u[WINDOW - 8:, :]
    u_scr[0:8, :] = tail
    tail_ref[...] = tail


def _prompt_mix(sinks, q, kv, rest, conv_w):
    nb = N_P // WINDOW
    blk = lambda c: pl.BlockSpec((WINDOW, D_MODEL), lambda j, c=c: (j, c))
    return pl.pallas_call(
        _prompt_mix_kernel,
        grid=(nb,),
        in_specs=[pl.BlockSpec(memory_space=pltpu.SMEM),
                  pl.BlockSpec((WINDOW, D_ATTN), lambda j: (j, 0)),
                  pl.BlockSpec((WINDOW, 2 * D_KV), lambda j: (jnp.maximum(j - 1, 0), 0)),
                  pl.BlockSpec((WINDOW, 2 * D_KV), lambda j: (j, 0)),
                  blk(0), blk(1), blk(2), blk(3), blk(4),
                  pl.BlockSpec((CONV_WIDTH, D_CONV), lambda j: (0, 0))],
        out_specs=[pl.BlockSpec((WINDOW, D_MODEL), lambda j: (j, 0)),
                   pl.BlockSpec((8, D_CONV), lambda j: (0, 0))],
        out_shape=[jax.ShapeDtypeStruct((N_P, D_MODEL), BF16),
                   jax.ShapeDtypeStruct((8, D_CONV), F32)],
        scratch_shapes=[pltpu.VMEM((8 + WINDOW, D_CONV), F32), pltpu.VMEM((WINDOW, D_ATTN), F32)],
        compiler_params=_params(1),
        name="prompt_mix",
    )(sinks, q, kv, kv, rest, rest, rest, rest, rest, conv_w)


def _sample_attn_kernel(q_ref, kvn_ref, ckt_ref, cvt_ref, sink_ref, rep_ref,
                        o_ref, kwt_ref, vwt_ref):
    n_new = SEQ_BLOCK * DEC_SEQ
    rows = DEC_SEQ * N_HEADS
    keep = WINDOW - DEC_SEQ
    kvn = jnp.concatenate([kvn_ref[...], jnp.zeros((WINDOW - n_new, 2 * D_KV), F32)], axis=0)
    kn_t = kvn[:, :D_KV].T
    vn_t = kvn[:, D_KV:].T
    kn_t_bf = kn_t.astype(BF16)
    vn_bf = kvn[:, D_KV:].astype(BF16)

    row = lax.broadcasted_iota(jnp.int32, (rows, WINDOW), 0)
    col = lax.broadcasted_iota(jnp.int32, (rows, WINDOW), 1)
    t = row >> 5
    valid_cache = col > t
    row2 = lax.broadcasted_iota(jnp.int32, (rows, D_KV), 0)
    col2 = lax.broadcasted_iota(jnp.int32, (rows, D_KV), 1)
    same_group = ((row2 & (N_HEADS - 1)) >> 3) == (col2 >> 6)
    win_lane = lax.broadcasted_iota(jnp.int32, (D_KV, WINDOW), 1)
    sink = sink_ref[:, 0:1]
    rep = rep_ref[...]

    for s in range(SEQ_BLOCK):
        k_t = ckt_ref[s]
        v_t = cvt_ref[s]
        q4 = jnp.dot(q_ref[s], rep, preferred_element_type=F32)
        q4 = jnp.where(same_group, q4, 0.0).astype(BF16)
        s_c = jnp.dot(q4, k_t.astype(BF16), preferred_element_type=F32)
        s_n = jnp.dot(q4, kn_t_bf, preferred_element_type=F32)
        valid_new = (col < n_new) & ((col >> 2) == s) & ((col & 3) <= t)
        s_c = jnp.where(valid_cache, s_c, NEG_INF)
        s_n = jnp.where(valid_new, s_n, NEG_INF)
        m = jnp.maximum(jnp.maximum(jnp.max(s_c, axis=1, keepdims=True),
                                    jnp.max(s_n, axis=1, keepdims=True)), sink)
        p_c = jnp.exp2(s_c - m)
        p_n = jnp.exp2(s_n - m)
        l = (jnp.sum(p_c, axis=1, keepdims=True) + jnp.sum(p_n, axis=1, keepdims=True)
             + jnp.exp2(sink - m))
        o4 = (lax.dot_general(p_c.astype(BF16), v_t.astype(BF16), (((1,), (1,)), ((), ())),
                              preferred_element_type=F32)
              + jnp.dot(p_n.astype(BF16), vn_bf, preferred_element_type=F32))
        o4 = jnp.where(same_group, o4, 0.0)
        a = o4[:, 0:LANES] + o4[:, LANES:]
        o = a + pltpu.roll(a, HEAD_DIM, 1)
        o_ref[s] = o[:, 0:HEAD_DIM] / (2.0 * l)

        new_shift = keep - s * DEC_SEQ
        kwt_ref[s] = jnp.where(win_lane >= keep, pltpu.roll(kn_t, new_shift, 1),
                               pltpu.roll(k_t, keep, 1))
        vwt_ref[s] = jnp.where(win_lane >= keep, pltpu.roll(vn_t, new_shift, 1),
                               pltpu.roll(v_t, keep, 1))


def _sample_attn(q_s, kv, cache_kt, cache_vt, sink_col, rep):
    rows = DEC_SEQ * N_HEADS
    n_new = SEQ_BLOCK * DEC_SEQ
    cache_spec = pl.BlockSpec((SEQ_BLOCK, D_KV, WINDOW), lambda i: (i, 0, 0))
    return pl.pallas_call(
        _sample_attn_kernel,
        grid=(DEC_BATCH // SEQ_BLOCK,),
        in_specs=[pl.BlockSpec((SEQ_BLOCK, rows, HEAD_DIM), lambda i: (i, 0, 0)),
                  pl.BlockSpec((n_new, 2 * D_KV), lambda i: (i + N_P // n_new, 0)),
                  cache_spec, cache_spec,
                  pl.BlockSpec((rows, LANES), lambda i: (0, 0)),
                  pl.BlockSpec((HEAD_DIM, D_KV), lambda i: (0, 0))],
        out_specs=[pl.BlockSpec((SEQ_BLOCK, rows, HEAD_DIM), lambda i: (i, 0, 0)),
                   cache_spec, cache_spec],
        out_shape=[jax.ShapeDtypeStruct((DEC_BATCH, rows, HEAD_DIM), F32),
                   jax.ShapeDtypeStruct((DEC_BATCH, D_KV, WINDOW), F32),
                   jax.ShapeDtypeStruct((DEC_BATCH, D_KV, WINDOW), F32)],
        compiler_params=_params(1),
        name="sample_attn",
    )(q_s, kv, cache_kt, cache_vt, sink_col, rep)


def _sample_merge_kernel(attn_ref, h_ref, b_ref, c_ref, ga_ref, gc_ref, st_ref, cw_ref,
                         mix_ref, u_ref):
    u = c_ref[...].astype(F32) * h_ref[...].astype(F32)
    u_ref[...] = u
    rows = u.shape[0]
    t = lax.broadcasted_iota(jnp.int32, u.shape, 0) & (DEC_SEQ - 1)
    st = st_ref[...]
    u1 = jnp.where(t == 0, pltpu.roll(st, rows - 1, 0), pltpu.roll(u, 1, 0))
    u2 = jnp.where(t < 2, st, pltpu.roll(u, 2, 0))
    mix = _gated_merge(attn_ref[...], u, u1, u2, b_ref, ga_ref, gc_ref, cw_ref)
    mix_ref[...] = mix.astype(mix_ref.dtype)


def _sample_merge(attn_s, rest, st_rows, conv_w):
    tm = WINDOW
    off = N_P // tm
    blk = lambda c: pl.BlockSpec((tm, D_MODEL), lambda i, c=c: (i + off, c))
    return pl.pallas_call(
        _sample_merge_kernel,
        grid=(N_S // tm,),
        in_specs=[pl.BlockSpec((tm, D_ATTN), lambda i: (i, 0)),
                  blk(0), blk(1), blk(2), blk(3), blk(4),
                  pl.BlockSpec((tm, D_CONV), lambda i: (i, 0)),
                  pl.BlockSpec((CONV_WIDTH, D_CONV), lambda i: (0, 0))],
        out_specs=[pl.BlockSpec((tm, D_MODEL), lambda i: (i, 0)),
                   pl.BlockSpec((tm, D_CONV), lambda i: (i, 0))],
        out_shape=[jax.ShapeDtypeStruct((N_S, D_MODEL), BF16),
                   jax.ShapeDtypeStruct((N_S, D_CONV), F32)],
        compiler_params=_params(1),
        name="sample_merge",
    )(attn_s, rest, rest, rest, rest, rest, st_rows, conv_w)


def _out_proj_kernel(mp_ref, ms_ref, w_ref, xp_ref, xs_ref, gf_ref, y_ref, yn_ref):
    is_prompt = pl.program_id(0) < N_P // ROW_TILE_SMALL
    res = jnp.where(is_prompt, xp_ref[...], xs_ref[...])
    mix = jnp.where(is_prompt, mp_ref[...], ms_ref[...])
    y = res + jnp.dot(mix, w_ref[...], preferred_element_type=F32)
    y_ref[...] = y
    yn_ref[...] = _rmsnorm_rows(y, gf_ref[...]).astype(yn_ref.dtype)


def _out_proj(mix_p, mix_s, w_out_bf, x_p, x_s, g_ffn):
    tm = ROW_TILE_SMALL
    last_p = N_P // tm - 1
    once = pl.Buffered(1)
    return pl.pallas_call(
        _out_proj_kernel,
        grid=(N_ALL // tm,),
        in_specs=[pl.BlockSpec((tm, D_ATTN), lambda i: (jnp.minimum(i, last_p), 0)),
                  pl.BlockSpec((tm, D_ATTN), lambda i: (0, 0), pipeline_mode=once),
                  pl.BlockSpec((D_ATTN, D_MODEL), lambda i: (0, 0), pipeline_mode=once),
                  pl.BlockSpec((tm, D_MODEL), lambda i: (jnp.minimum(i, last_p), 0)),
                  pl.BlockSpec((tm, D_MODEL), lambda i: (0, 0), pipeline_mode=once),
                  pl.BlockSpec((1, D_MODEL), lambda i: (0, 0))],
        out_specs=[pl.BlockSpec((tm, D_MODEL), lambda i: (i, 0)),
                   pl.BlockSpec((tm, D_MODEL), lambda i: (i, 0))],
        out_shape=[jax.ShapeDtypeStruct((N_ALL, D_MODEL), F32),
                   jax.ShapeDtypeStruct((N_ALL, D_MODEL), BF16)],
        compiler_params=_params(1),
        name="out_proj",
    )(mix_p, mix_s, w_out_bf, x_p, x_s, g_ffn)


def _ffn_up_kernel(x_ref, wg_ref, wu_ref, wd_ref, o_ref, wd_bf_ref, wg_bf, wu_bf):
    @pl.when(pl.program_id(1) == 0)
    def _():
        wg_bf[...] = wg_ref[...].astype(BF16)
        wu_bf[...] = wu_ref[...].astype(BF16)
        wd_bf_ref[...] = wd_ref[...].astype(BF16)

    x = x_ref[...]
    g = jnp.dot(x, wg_bf[...], preferred_element_type=F32)
    u = jnp.dot(x, wu_bf[...], preferred_element_type=F32)
    o_ref[...] = ((g * jax.nn.sigmoid(g)) * u).astype(o_ref.dtype)


def _ffn_up(yn, w_gate_up, w_down):
    tm = ROW_TILE_BIG
    tn = 512
    nt = D_FF // tn
    return pl.pallas_call(
        _ffn_up_kernel,
        grid=(nt, N_ALL // tm),
        in_specs=[pl.BlockSpec((tm, D_MODEL), lambda j, i: (i, 0)),
                  pl.BlockSpec((D_MODEL, tn), lambda j, i: (0, j)),
                  pl.BlockSpec((D_MODEL, tn), lambda j, i: (0, j + nt)),
                  pl.BlockSpec((tn, D_MODEL), lambda j, i: (j, 0))],
        out_specs=[pl.BlockSpec((tm, tn), lambda j, i: (i, j)),
                   pl.BlockSpec((tn, D_MODEL), lambda j, i: (j, 0))],
        out_shape=[jax.ShapeDtypeStruct((N_ALL, D_FF), BF16),
                   jax.ShapeDtypeStruct((D_FF, D_MODEL), BF16)],
        scratch_shapes=[pltpu.VMEM((D_MODEL, tn), BF16), pltpu.VMEM((D_MODEL, tn), BF16)],
        compiler_params=_params(2),
        name="ffn_up",
    )(yn, w_gate_up, w_gate_up, w_down)


def _ffn_down_kernel(a_ref, w_ref, r_ref, yp_ref, ys_ref):
    i = pl.program_id(0)
    n_prompt_tiles = N_P // FFN_DOWN_ROWS
    y = r_ref[...] + jnp.dot(a_ref[...], w_ref[...], preferred_element_type=F32)

    @pl.when(i < n_prompt_tiles)
    def _():
        yp_ref[...] = y

    @pl.when(i >= n_prompt_tiles)
    def _():
        ys_ref[...] = y


def _ffn_down(act, w_down_bf, y1):
    tm = FFN_DOWN_ROWS
    n_p = N_P // tm
    return pl.pallas_call(
        _ffn_down_kernel,
        grid=(N_ALL // tm,),
        in_specs=[pl.BlockSpec((tm, D_FF), lambda i: (i, 0)),
                  pl.BlockSpec((D_FF, D_MODEL), lambda i: (0, 0), pipeline_mode=pl.Buffered(1)),
                  pl.BlockSpec((tm, D_MODEL), lambda i: (i, 0))],
        out_specs=[pl.BlockSpec((tm, D_MODEL), lambda i: (jnp.minimum(i, n_p - 1), 0)),
                   pl.BlockSpec((tm, D_MODEL), lambda i: (jnp.maximum(i - n_p, 0), 0))],
        out_shape=[jax.ShapeDtypeStruct((N_P, D_MODEL), F32),
                   jax.ShapeDtypeStruct((N_S, D_MODEL), F32)],
        compiler_params=_params(1),
        name="ffn_down",
    )(act, w_down_bf, y1)


def _rope_tables():
    inv = ROPE_THETA ** (-jnp.arange(0, HEAD_DIM, 2, dtype=F32) / HEAD_DIM)
    inv_t = jnp.tile(inv, LANES // (HEAD_DIM // 2))[None, :]
    sign = jnp.asarray(np.tile(np.repeat([-1.0, 1.0], HEAD_DIM // 2), LANES // HEAD_DIM), F32)[None, :]
    ang_a = (jnp.arange(N_P // LANES, dtype=jnp.int32) * LANES).astype(F32)[:, None] * inv_t
    ang_b = jnp.arange(LANES, dtype=jnp.int32).astype(F32)[:, None] * inv_t
    ca, sa = jnp.cos(ang_a)[:, None, :], jnp.sin(ang_a)[:, None, :]
    cb, sb = jnp.cos(ang_b)[None, :, :], jnp.sin(ang_b)[None, :, :]
    cos_p = (ca * cb - sa * sb).reshape(N_P, LANES)
    sin_p = (sa * cb + ca * sb).reshape(N_P, LANES)
    ang_s = (PAST_LEN + jnp.arange(DEC_SEQ, dtype=jnp.int32)).astype(F32)[:, None] * inv_t
    cos_s = jnp.tile(jnp.cos(ang_s), (DEC_BATCH, 1))
    sin_s = jnp.tile(jnp.sin(ang_s), (DEC_BATCH, 1))
    return (jnp.concatenate([cos_p, cos_s], axis=0),
            jnp.concatenate([sin_p, sin_s], axis=0) * sign)


def kernel(x_prompt, x_sample, cache_k_win, cache_v_win, state_conv, norm_mix, w_in, q_norm, k_norm,
           sinks, conv_w, w_out, norm_ffn, w_gate_up, w_down):
    assert x_prompt.shape == (1, SEQ, D_MODEL) and x_sample.shape == (DEC_BATCH, DEC_SEQ, D_MODEL)
    assert w_in.shape == (1, D_MODEL, D_IN)
    assert cache_k_win.shape == (1, DEC_BATCH, WINDOW, N_KV_HEADS, HEAD_DIM)

    x_p = x_prompt.reshape(N_P, D_MODEL)
    x_s = x_sample.reshape(N_S, D_MODEL)
    w_in2 = w_in.reshape(D_MODEL, D_IN)
    w_out2 = w_out.reshape(D_ATTN, D_MODEL)
    w_gu2 = w_gate_up.reshape(D_MODEL, 2 * D_FF)
    w_dn2 = w_down.reshape(D_FF, D_MODEL)
    conv_w2 = conv_w.reshape(CONV_WIDTH, D_CONV)
    cache_kt = jnp.transpose(cache_k_win.reshape(DEC_BATCH, WINDOW, D_KV), (0, 2, 1))
    cache_vt = jnp.transpose(cache_v_win.reshape(DEC_BATCH, WINDOW, D_KV), (0, 2, 1))

    cos_t, sin_t = _rope_tables()
    q_g128 = jnp.tile(q_norm.reshape(1, HEAD_DIM) * Q_SCALE, (1, LANES // HEAD_DIM))
    k_g128 = jnp.tile(k_norm.reshape(1, HEAD_DIM), (1, LANES // HEAD_DIM))
    rest_scale = jnp.asarray(np.repeat([1.0, 1.0, 1.0, 0.5, 0.5], D_MODEL)[None, :], F32)
    head_of_lane = np.arange(LANES) // HEAD_DIM
    bd = jnp.asarray(np.where(head_of_lane[:, None] == head_of_lane[None, :], 1.0 / HEAD_DIM, 0.0),
                     dtype=BF16)
    sinks2 = sinks.reshape(1, N_HEADS).astype(F32)
    sink_col = jnp.broadcast_to(jnp.tile(sinks2.reshape(N_HEADS) * LOG2E, DEC_SEQ)[:, None],
                                (DEC_SEQ * N_HEADS, LANES))
    rep = jnp.asarray(np.tile(np.eye(HEAD_DIM), (1, N_KV_HEADS)), dtype=BF16)
    st_rows = jnp.pad(state_conv.reshape(DEC_BATCH, CONV_WIDTH - 1, D_CONV),
                      ((0, 0), (0, DEC_SEQ - (CONV_WIDTH - 1)), (0, 0))).reshape(N_S, D_CONV)

    xn, kv = _norm_proj_kv(x_p, x_s, norm_mix.reshape(1, D_MODEL), w_in2, k_g128, cos_t, sin_t, bd)
    q, rest, w_out_bf = _proj_q_rest(xn, w_in2, q_g128, cos_t, sin_t, bd, rest_scale, w_out2)

    mix_p, tail = _prompt_mix(sinks2, q, kv, rest, conv_w2)
    q_s = q[N_P:].reshape(DEC_BATCH, DEC_SEQ * N_HEADS, HEAD_DIM)
    attn_s, kwt, vwt = _sample_attn(q_s, kv, cache_kt, cache_vt, sink_col, rep)
    mix_s, u_s = _sample_merge(attn_s.reshape(N_S, D_ATTN), rest, st_rows, conv_w2)

    y1, yn = _out_proj(mix_p, mix_s, w_out_bf, x_p, x_s, norm_ffn.reshape(1, D_MODEL))
    act, w_down_bf = _ffn_up(yn, w_gu2, w_dn2)
    y_p, y_s = _ffn_down(act, w_down_bf, y1)

    kv_tail = kv[N_P - WINDOW:N_P]
    win_shape = (1, DEC_BATCH, WINDOW, N_KV_HEADS, HEAD_DIM)
    return (y_p.reshape(1, SEQ, D_MODEL),
            y_s.reshape(DEC_BATCH, DEC_SEQ, D_MODEL),
            kv_tail[:, :D_KV].reshape(1, 1, WINDOW, N_KV_HEADS, HEAD_DIM),
            kv_tail[:, D_KV:].reshape(1, 1, WINDOW, N_KV_HEADS, HEAD_DIM),
            tail[8 - (CONV_WIDTH - 1):].reshape(1, 1, CONV_WIDTH - 1, D_CONV),
            jnp.transpose(kwt, (0, 2, 1)).reshape(win_shape),
            jnp.transpose(vwt, (0, 2, 1)).reshape(win_shape),
            u_s.reshape(DEC_BATCH, DEC_SEQ, D_CONV)[:, DEC_SEQ - (CONV_WIDTH - 1):][None])
```

```python
import math

import numpy as np
import jax
import jax.numpy as jnp
from jax import lax
from jax.experimental import pallas as pl
from jax.experimental.pallas import tpu as pltpu

F32 = jnp.float32
BF16 = jnp.bfloat16

D_MODEL = 2048
SEQ = 8192
DEC_BATCH = 128
DEC_SEQ = 4
PAST_LEN = 8192
N_HEADS = 32
HEAD_DIM = 64
N_KV_HEADS = 4
GROUP = N_HEADS // N_KV_HEADS
D_ATTN = N_HEADS * HEAD_DIM
D_KV = N_KV_HEADS * HEAD_DIM
WINDOW = 128
D_CONV = D_MODEL
CONV_WIDTH = 3
D_FF = 5632
ROPE_THETA = 10000.0
EPS = 1e-6
NEG_INF = -1e30
D_IN = 2 * D_ATTN + 2 * D_KV + 3 * D_CONV + D_CONV
D_REST = 5 * D_MODEL
LOG2E = math.log2(math.e)
Q_SCALE = HEAD_DIM ** -0.5 * LOG2E

N_P = SEQ
N_S = DEC_BATCH * DEC_SEQ
N_ALL = N_P + N_S

LANES = 128
VMEM_LIMIT = 56 * 1024 * 1024

ROW_TILE_SMALL = 512
ROW_TILE_BIG = 1088
FFN_DOWN_ROWS = 256
MERGE_CHUNK = (32, 256)
SEQ_BLOCK = 8
PAIRS_PER_GROUP = GROUP // 2


def _params(n_axes):
    return pltpu.CompilerParams(
        dimension_semantics=("arbitrary",) * n_axes, vmem_limit_bytes=VMEM_LIMIT)


def _rmsnorm_rows(x, g):
    ms = jnp.mean(x * x, axis=-1, keepdims=True)
    return x * lax.rsqrt(ms + EPS) * g


def _cast_weight_once(w_ref, wbf_ref, axis):
    @pl.when(pl.program_id(axis) == 0)
    def _():
        wbf_ref[...] = w_ref[...].astype(BF16)


def _headnorm_rope(zc, g128, cos_t, sin_t, bd):
    sq = zc * zc
    hi = sq.astype(BF16)
    lo = (sq - hi.astype(F32)).astype(BF16)
    ms = (jnp.dot(hi, bd, preferred_element_type=F32)
          + jnp.dot(lo, bd, preferred_element_type=F32))
    y = zc * lax.rsqrt(ms + EPS) * g128
    lane = lax.broadcasted_iota(jnp.int32, y.shape, 1)
    first_half = (lane & (HEAD_DIM // 2)) == 0
    partner = jnp.where(first_half,
                        pltpu.roll(y, LANES - HEAD_DIM // 2, 1),
                        pltpu.roll(y, HEAD_DIM // 2, 1))
    return y * cos_t + partner * sin_t


def _norm_proj_kv_kernel(xp_ref, xs_ref, gm_ref, w_ref, g_ref, cos_ref, sin_ref, bd_ref,
                         xn_ref, kv_ref, wbf_ref):
    _cast_weight_once(w_ref, wbf_ref, 0)
    is_prompt = pl.program_id(0) < N_P // ROW_TILE_SMALL
    x = jnp.where(is_prompt, xp_ref[...], xs_ref[...])
    xn = _rmsnorm_rows(x, gm_ref[...]).astype(BF16)
    xn_ref[...] = xn
    z = jnp.dot(xn, wbf_ref[...], preferred_element_type=F32)
    g128 = g_ref[...]
    cos_t = cos_ref[...]
    sin_t = sin_ref[...]
    bd = bd_ref[...]
    for c in range(D_KV // LANES):
        zc = z[:, c * LANES:(c + 1) * LANES]
        kv_ref[:, c * LANES:(c + 1) * LANES] = _headnorm_rope(zc, g128, cos_t, sin_t, bd)
    kv_ref[:, D_KV:] = z[:, D_KV:]


def _norm_proj_kv(x_p, x_s, g_mix, w_in, k_g128, cos_t, sin_t, bd):
    tm = ROW_TILE_SMALL
    tn = 2 * D_KV
    last_p = N_P // tm - 1
    const = lambda shape: pl.BlockSpec(shape, lambda i: (0, 0))
    return pl.pallas_call(
        _norm_proj_kv_kernel,
        grid=(N_ALL // tm,),
        in_specs=[pl.BlockSpec((tm, D_MODEL), lambda i: (jnp.minimum(i, last_p), 0)),
                  const((tm, D_MODEL)),
                  const((1, D_MODEL)),
                  pl.BlockSpec((D_MODEL, tn), lambda i: (0, D_ATTN // tn)),
                  const((1, LANES)),
                  pl.BlockSpec((tm, LANES), lambda i: (i, 0)),
                  pl.BlockSpec((tm, LANES), lambda i: (i, 0)),
                  const((LANES, LANES))],
        out_specs=[pl.BlockSpec((tm, D_MODEL), lambda i: (i, 0)),
                   pl.BlockSpec((tm, tn), lambda i: (i, 0))],
        out_shape=[jax.ShapeDtypeStruct((N_ALL, D_MODEL), BF16),
                   jax.ShapeDtypeStruct((N_ALL, tn), F32)],
        scratch_shapes=[pltpu.VMEM((D_MODEL, tn), BF16)],
        compiler_params=_params(1),
        name="norm_proj_kv",
    )(x_p, x_s, g_mix, w_in, k_g128, cos_t, sin_t, bd)


def _proj_rest_kernel(x_ref, w_ref, scale_ref, o_ref, wbf_ref):
    _cast_weight_once(w_ref, wbf_ref, 1)
    z = jnp.dot(x_ref[...], wbf_ref[...], preferred_element_type=F32)
    o_ref[...] = (z * scale_ref[...]).astype(o_ref.dtype)


def _proj_q_kernel(x_ref, w_ref, g_ref, cos_ref, sin_ref, bd_ref, o_ref, wbf_ref):
    _cast_weight_once(w_ref, wbf_ref, 1)
    z = jnp.dot(x_ref[...], wbf_ref[...], preferred_element_type=F32)
    g128 = g_ref[...]
    cos_t = cos_ref[...]
    sin_t = sin_ref[...]
    bd = bd_ref[...]
    for c in range(z.shape[1] // LANES):
        zc = z[:, c * LANES:(c + 1) * LANES]
        out = _headnorm_rope(zc, g128, cos_t, sin_t, bd)
        o_ref[:, c * LANES:(c + 1) * LANES] = out.astype(o_ref.dtype)


def _proj_q_rest(xn, w_in, q_g128, cos_q, sin_q, bd, rest_scale):
    tm = ROW_TILE_BIG
    n_row = N_ALL // tm
    k = D_MODEL
    x_spec = pl.BlockSpec((tm, k), lambda j, i: (i, 0))
    rope_specs = [pl.BlockSpec((1, LANES), lambda j, i: (0, 0)),
                  pl.BlockSpec((tm, LANES), lambda j, i: (i, 0)),
                  pl.BlockSpec((tm, LANES), lambda j, i: (i, 0)),
                  pl.BlockSpec((LANES, LANES), lambda j, i: (0, 0))]

    tn_q = 1024
    q = pl.pallas_call(
        _proj_q_kernel,
        grid=(D_ATTN // tn_q, n_row),
        in_specs=[x_spec, pl.BlockSpec((k, tn_q), lambda j, i: (0, j))] + rope_specs,
        out_specs=pl.BlockSpec((tm, tn_q), lambda j, i: (i, j)),
        out_shape=jax.ShapeDtypeStruct((N_ALL, D_ATTN), BF16),
        scratch_shapes=[pltpu.VMEM((k, tn_q), BF16)],
        compiler_params=_params(2),
        name="proj_q",
    )(xn, w_in, q_g128, cos_q, sin_q, bd)

    tn_r = 1280
    rest_start = D_ATTN + 2 * D_KV
    rest = pl.pallas_call(
        _proj_rest_kernel,
        grid=(D_REST // tn_r, n_row),
        in_specs=[x_spec,
                  pl.BlockSpec((k, tn_r), lambda j, i: (0, j + rest_start // tn_r)),
                  pl.BlockSpec((1, tn_r), lambda j, i: (0, j))],
        out_specs=pl.BlockSpec((tm, tn_r), lambda j, i: (i, j)),
        out_shape=jax.ShapeDtypeStruct((N_ALL, D_REST), BF16),
        scratch_shapes=[pltpu.VMEM((k, tn_r), BF16)],
        compiler_params=_params(2),
        name="proj_rest",
    )(xn, w_in, rest_scale)
    return q, rest


def _gated_merge(attn_half, u, u1, u2, b_ref, ga_half_ref, gc_half_ref, cw_ref):
    return _gated_merge_values(attn_half, u, u1, u2, b_ref[...].astype(F32),
                               ga_half_ref[...].astype(F32), gc_half_ref[...].astype(F32),
                               0.5 * cw_ref[...])


def _gated_merge_values(attn_half, u, u1, u2, b, ga_half, gc_half, cw_half):
    conv_half = cw_half[0:1, :] * u2 + cw_half[1:2, :] * u1 + cw_half[2:3, :] * u
    bc_half = b * conv_half
    ta = jnp.tanh(ga_half)
    tc = jnp.tanh(gc_half)
    return (ta * attn_half + attn_half) + (tc * bc_half + bc_half)


def _prompt_attention_block(j, sinks_ref, q_ref, kvp_ref, kvc_ref, attn_ref):
    n_keys = 2 * WINDOW
    kv = jnp.concatenate([kvp_ref[...], kvc_ref[...]], axis=0)
    key_row = lax.broadcasted_iota(jnp.int32, (n_keys, LANES), 0)
    low = lax.broadcasted_iota(jnp.int32, (n_keys, LANES), 1) < HEAD_DIM
    low_v = low & (key_row > 0)
    high_v = jnp.logical_not(low) & (key_row > 0)
    ones_low = jnp.where(low, 2.0, 0.0)
    ones_high = jnp.where(low, 0.0, 2.0)

    r = lax.broadcasted_iota(jnp.int32, (WINDOW, n_keys), 0)
    c = lax.broadcasted_iota(jnp.int32, (WINDOW, n_keys), 1)
    first_key = jnp.where(j > 0, 0, WINDOW)
    mask = jnp.where((c > r) & (c <= r + WINDOW) & (c >= first_key), 0.0, NEG_INF).astype(BF16)
    mask2 = jnp.concatenate([mask, mask], axis=1)
    eye = (lax.broadcasted_iota(jnp.int32, (WINDOW, WINDOW), 0)
           == lax.broadcasted_iota(jnp.int32, (WINDOW, WINDOW), 1))
    eye = jnp.where(eye, 1.0, 0.0).astype(BF16)
    sink_lane = lax.broadcasted_iota(jnp.int32, (WINDOW, LANES), 1) == 0
    no_keys = jnp.zeros((HEAD_DIM, n_keys), F32)

    for g in range(N_KV_HEADS):
        chunk = (g // 2) * LANES
        kc_t = kv[:, chunk:chunk + LANES].T
        kg_t = kc_t[(g % 2) * HEAD_DIM:(g % 2 + 1) * HEAD_DIM]
        vc = kv[:, D_KV + chunk:D_KV + chunk + LANES]
        if g % 2 == 0:
            v_low = jnp.where(low_v, vc, 0.0)
            v_high = pltpu.roll(v_low, HEAD_DIM, 1)
        else:
            v_high = jnp.where(high_v, vc, 0.0)
            v_low = pltpu.roll(v_high, HEAD_DIM, 1)
        k_bd_t = jnp.concatenate(
            [jnp.concatenate([kg_t, no_keys], axis=1),
             jnp.concatenate([no_keys, kg_t], axis=1)], axis=0).astype(BF16)
        k_aug = jnp.concatenate([k_bd_t, mask2], axis=0)
        v_bd = jnp.concatenate(
            [jnp.concatenate([v_low, ones_low], axis=1),
             jnp.concatenate([v_high, ones_high], axis=1)], axis=0).astype(BF16)

        for k in range(PAIRS_PER_GROUP):
            pair = g * PAIRS_PER_GROUP + k
            cols = slice(pair * LANES, (pair + 1) * LANES)
            q_aug = jnp.concatenate([q_ref[:, cols], eye], axis=1)
            s = jnp.dot(q_aug, k_aug, preferred_element_type=F32)
            halves = []
            for half in range(2):
                sh = s[:, half * n_keys:(half + 1) * n_keys]
                sink = sinks_ref[0, 2 * pair + half] * LOG2E
                sh = jnp.concatenate([jnp.where(sink_lane, sink, sh[:, :LANES]), sh[:, LANES:]], axis=1)
                halves.append(jnp.exp2(sh - jnp.max(sh, axis=1, keepdims=True)))
            p = jnp.concatenate(halves, axis=1).astype(BF16)
            o = jnp.dot(p, v_bd, preferred_element_type=F32)
            attn_ref[:, cols] = (o[:, :LANES] / o[:, LANES:]).astype(attn_ref.dtype)


def _prompt_mix_kernel(sinks_ref, q_ref, kvp_ref, kvc_ref, h_ref, b_ref, c_ref, ga_ref, gc_ref,
                       cw_ref, mix_ref, tail_ref, u_scr, attn_scr):
    j = pl.program_id(0)

    @pl.when(j == 0)
    def _():
        u_scr[0:8, :] = jnp.zeros((8, D_CONV), F32)

    _prompt_attention_block(j, sinks_ref, q_ref, kvp_ref, kvc_ref, attn_scr)

    rb, cb = MERGE_CHUNK
    cw_half = 0.5 * cw_ref[...]
    for r0 in range(0, WINDOW, rb):
        rows = slice(r0, r0 + rb)
        u_scr[8 + r0:8 + r0 + rb, :] = c_ref[rows, :].astype(F32) * h_ref[rows, :].astype(F32)
        for c0 in range(0, D_MODEL, cb):
            cols = slice(c0, c0 + cb)
            mix = _gated_merge_values(
                attn_scr[rows, cols],
                u_scr[8 + r0:8 + r0 + rb, cols], u_scr[7 + r0:7 + r0 + rb, cols],
                u_scr[6 + r0:6 + r0 + rb, cols], b_ref[rows, cols].astype(F32),
                ga_ref[rows, cols].astype(F32), gc_ref[rows, cols].astype(F32), cw_half[:, cols])
            mix_ref[rows, cols] = mix.astype(mix_ref.dtype)
    tail = u_scr[WINDOW:WINDOW + 8, :]
    u_scr[0:8, :] = tail
    tail_ref[...] = tail


def _prompt_mix(sinks, q, kv, rest, conv_w):
    nb = N_P // WINDOW
    blk = lambda c: pl.BlockSpec((WINDOW, D_MODEL), lambda j, c=c: (j, c))
    return pl.pallas_call(
        _prompt_mix_kernel,
        grid=(nb,),
        in_specs=[pl.BlockSpec(memory_space=pltpu.SMEM),
                  pl.BlockSpec((WINDOW, D_ATTN), lambda j: (j, 0)),
                  pl.BlockSpec((WINDOW, 2 * D_KV), lambda j: (jnp.maximum(j - 1, 0), 0)),
                  pl.BlockSpec((WINDOW, 2 * D_KV), lambda j: (j, 0)),
                  blk(0), blk(1), blk(2), blk(3), blk(4),
                  pl.BlockSpec((CONV_WIDTH, D_CONV), lambda j: (0, 0))],
        out_specs=[pl.BlockSpec((WINDOW, D_MODEL), lambda j: (j, 0)),
                   pl.BlockSpec((8, D_CONV), lambda j: (0, 0))],
        out_shape=[jax.ShapeDtypeStruct((N_P, D_MODEL), BF16),
                   jax.ShapeDtypeStruct((8, D_CONV), F32)],
        scratch_shapes=[pltpu.VMEM((8 + WINDOW, D_CONV), F32), pltpu.VMEM((WINDOW, D_ATTN), F32)],
        compiler_params=_params(1),
        name="prompt_mix",
    )(sinks, q, kv, kv, rest, rest, rest, rest, rest, conv_w)


def _sample_attn_kernel(q_ref, kvn_ref, ckt_ref, cvt_ref, sink_ref, rep_ref, wo_ref, wd_ref,
                        o_ref, kwt_ref, vwt_ref, wo_bf_ref, wd_bf_ref, bias_c, bias_n):
    n_new = SEQ_BLOCK * DEC_SEQ
    rows = DEC_SEQ * N_HEADS
    keep = WINDOW - DEC_SEQ

    wo_bf_ref[...] = wo_ref[...].astype(BF16)
    wd_bf_ref[...] = wd_ref[...].astype(BF16)

    @pl.when(pl.program_id(0) == 0)
    def _():
        row = lax.broadcasted_iota(jnp.int32, (rows, WINDOW), 0)
        col = lax.broadcasted_iota(jnp.int32, (rows, WINDOW), 1)
        t = row >> 5
        bias_c[...] = jnp.where(col > t, 0.0, NEG_INF)
        for s in range(SEQ_BLOCK):
            valid_new = (col < n_new) & ((col >> 2) == s) & ((col & 3) <= t)
            bias_n[s] = jnp.where(valid_new, 0.0, NEG_INF)

    kvn = jnp.concatenate([kvn_ref[...], jnp.zeros((WINDOW - n_new, 2 * D_KV), F32)], axis=0)
    kn_t = kvn[:, :D_KV].T
    vn_t = kvn[:, D_KV:].T
    kn_t_bf = kn_t.astype(BF16)
    vn_bf = kvn[:, D_KV:].astype(BF16)

    all_rows = SEQ_BLOCK * rows
    row2 = lax.broadcasted_iota(jnp.int32, (all_rows, D_KV), 0)
    col2 = lax.broadcasted_iota(jnp.int32, (all_rows, D_KV), 1)
    same_group = ((row2 & (N_HEADS - 1)) >> 3) == (col2 >> 6)
    win_lane = lax.broadcasted_iota(jnp.int32, (D_KV, WINDOW), 1)
    sink_b = jnp.concatenate([sink_ref[...]] * SEQ_BLOCK, axis=0)
    ones = jnp.ones((2 * WINDOW, LANES), BF16)

    q_all = q_ref[...].reshape(all_rows, HEAD_DIM)
    q4 = jnp.dot(q_all, rep_ref[...], preferred_element_type=F32)
    q4 = jnp.where(same_group, q4, 0.0).astype(BF16)
    s_n = jnp.dot(q4, kn_t_bf, preferred_element_type=F32) + bias_n[...].reshape(all_rows, WINDOW)
    s_c = jnp.concatenate(
        [jnp.dot(q4[s * rows:(s + 1) * rows], ckt_ref[s].astype(BF16), preferred_element_type=F32)
         + bias_c[...] for s in range(SEQ_BLOCK)], axis=0)
    m_b = jnp.maximum(jnp.max(jnp.maximum(s_c, s_n), axis=1, keepdims=True), sink_b)
    p_c = jnp.exp2(s_c - m_b).astype(BF16)
    p_n = jnp.exp2(s_n - m_b).astype(BF16)
    l_b = (jnp.dot(jnp.concatenate([p_c, p_n], axis=1), ones, preferred_element_type=F32)
           + jnp.exp2(sink_b - m_b))
    o4 = jnp.concatenate(
        [lax.dot_general(p_c[s * rows:(s + 1) * rows], cvt_ref[s].astype(BF16),
                         (((1,), (1,)), ((), ())), preferred_element_type=F32)
         for s in range(SEQ_BLOCK)], axis=0) + jnp.dot(p_n, vn_bf, preferred_element_type=F32)
    o4 = jnp.where(same_group, o4, 0.0)
    a = o4[:, 0:LANES] + o4[:, LANES:]
    o = (a + pltpu.roll(a, HEAD_DIM, 1)) / (2.0 * l_b)
    o_ref[...] = o[:, 0:HEAD_DIM].reshape(SEQ_BLOCK, rows, HEAD_DIM)

    for s in range(SEQ_BLOCK):
        k_t = ckt_ref[s]
        v_t = cvt_ref[s]
        new_shift = keep - s * DEC_SEQ
        kwt_ref[s] = jnp.where(win_lane >= keep, pltpu.roll(kn_t, new_shift, 1),
                               pltpu.roll(k_t, keep, 1))
        vwt_ref[s] = jnp.where(win_lane >= keep, pltpu.roll(vn_t, new_shift, 1),
                               pltpu.roll(v_t, keep, 1))


def _sample_attn(q_s, kv, cache_kt, cache_vt, sink_col, rep, w_out, w_down):
    rows = DEC_SEQ * N_HEADS
    n_new = SEQ_BLOCK * DEC_SEQ
    n_steps = DEC_BATCH // SEQ_BLOCK
    wo_rows = D_ATTN // n_steps
    wd_rows = D_FF // n_steps
    cache_spec = pl.BlockSpec((SEQ_BLOCK, D_KV, WINDOW), lambda i: (i, 0, 0))
    wo_spec = pl.BlockSpec((wo_rows, D_MODEL), lambda i: (i, 0))
    wd_spec = pl.BlockSpec((wd_rows, D_MODEL), lambda i: (i, 0))
    return pl.pallas_call(
        _sample_attn_kernel,
        grid=(n_steps,),
        in_specs=[pl.BlockSpec((SEQ_BLOCK, rows, HEAD_DIM), lambda i: (i, 0, 0)),
                  pl.BlockSpec((n_new, 2 * D_KV), lambda i: (i + N_P // n_new, 0)),
                  cache_spec, cache_spec,
                  pl.BlockSpec((rows, LANES), lambda i: (0, 0)),
                  pl.BlockSpec((HEAD_DIM, D_KV), lambda i: (0, 0)),
                  wo_spec, wd_spec],
        out_specs=[pl.BlockSpec((SEQ_BLOCK, rows, HEAD_DIM), lambda i: (i, 0, 0)),
                   cache_spec, cache_spec, wo_spec, wd_spec],
        out_shape=[jax.ShapeDtypeStruct((DEC_BATCH, rows, HEAD_DIM), F32),
                   jax.ShapeDtypeStruct((DEC_BATCH, D_KV, WINDOW), F32),
                   jax.ShapeDtypeStruct((DEC_BATCH, D_KV, WINDOW), F32),
                   jax.ShapeDtypeStruct((D_ATTN, D_MODEL), BF16),
                   jax.ShapeDtypeStruct((D_FF, D_MODEL), BF16)],
        scratch_shapes=[pltpu.VMEM((rows, WINDOW), F32), pltpu.VMEM((SEQ_BLOCK, rows, WINDOW), F32)],
        compiler_params=_params(1),
        name="sample_attn",
    )(q_s, kv, cache_kt, cache_vt, sink_col, rep, w_out, w_down)


def _sample_merge_kernel(attn_ref, h_ref, b_ref, c_ref, ga_ref, gc_ref, st_ref, cw_ref,
                         mix_ref, u_ref):
    u = c_ref[...].astype(F32) * h_ref[...].astype(F32)
    u_ref[...] = u
    rows = u.shape[0]
    t = lax.broadcasted_iota(jnp.int32, u.shape, 0) & (DEC_SEQ - 1)
    st = st_ref[...]
    u1 = jnp.where(t == 0, pltpu.roll(st, rows - 1, 0), pltpu.roll(u, 1, 0))
    u2 = jnp.where(t < 2, st, pltpu.roll(u, 2, 0))
    mix = _gated_merge(attn_ref[...], u, u1, u2, b_ref, ga_ref, gc_ref, cw_ref)
    mix_ref[...] = mix.astype(mix_ref.dtype)


def _sample_merge(attn_s, rest, st_rows, conv_w):
    tm = WINDOW
    off = N_P // tm
    blk = lambda c: pl.BlockSpec((tm, D_MODEL), lambda i, c=c: (i + off, c))
    return pl.pallas_call(
        _sample_merge_kernel,
        grid=(N_S // tm,),
        in_specs=[pl.BlockSpec((tm, D_ATTN), lambda i: (i, 0)),
                  blk(0), blk(1), blk(2), blk(3), blk(4),
                  pl.BlockSpec((tm, D_CONV), lambda i: (i, 0)),
                  pl.BlockSpec((CONV_WIDTH, D_CONV), lambda i: (0, 0))],
        out_specs=[pl.BlockSpec((tm, D_MODEL), lambda i: (i, 0)),
                   pl.BlockSpec((tm, D_CONV), lambda i: (i, 0))],
        out_shape=[jax.ShapeDtypeStruct((N_S, D_MODEL), BF16),
                   jax.ShapeDtypeStruct((N_S, D_CONV), F32)],
        compiler_params=_params(1),
        name="sample_merge",
    )(attn_s, rest, rest, rest, rest, rest, st_rows, conv_w)


def _out_proj_kernel(mp_ref, ms_ref, w_ref, xp_ref, xs_ref, gf_ref, y_ref, yn_ref):
    is_prompt = pl.program_id(0) < N_P // ROW_TILE_SMALL
    res = jnp.where(is_prompt, xp_ref[...], xs_ref[...])
    mix = jnp.where(is_prompt, mp_ref[...], ms_ref[...])
    y = res + jnp.dot(mix, w_ref[...], preferred_element_type=F32)
    y_ref[...] = y
    yn_ref[...] = _rmsnorm_rows(y, gf_ref[...]).astype(yn_ref.dtype)


def _out_proj(mix_p, mix_s, w_out_bf, x_p, x_s, g_ffn):
    tm = ROW_TILE_SMALL
    last_p = N_P // tm - 1
    once = pl.Buffered(1)
    return pl.pallas_call(
        _out_proj_kernel,
        grid=(N_ALL // tm,),
        in_specs=[pl.BlockSpec((tm, D_ATTN), lambda i: (jnp.minimum(i, last_p), 0)),
                  pl.BlockSpec((tm, D_ATTN), lambda i: (0, 0), pipeline_mode=once),
                  pl.BlockSpec((D_ATTN, D_MODEL), lambda i: (0, 0), pipeline_mode=once),
                  pl.BlockSpec((tm, D_MODEL), lambda i: (jnp.minimum(i, last_p), 0)),
                  pl.BlockSpec((tm, D_MODEL), lambda i: (0, 0), pipeline_mode=once),
                  pl.BlockSpec((1, D_MODEL), lambda i: (0, 0))],
        out_specs=[pl.BlockSpec((tm, D_MODEL), lambda i: (i, 0)),
                   pl.BlockSpec((tm, D_MODEL), lambda i: (i, 0))],
        out_shape=[jax.ShapeDtypeStruct((N_ALL, D_MODEL), F32),
                   jax.ShapeDtypeStruct((N_ALL, D_MODEL), BF16)],
        compiler_params=_params(1),
        name="out_proj",
    )(mix_p, mix_s, w_out_bf, x_p, x_s, g_ffn)


def _ffn_up_kernel(x_ref, wg_ref, wu_ref, o_ref, wg_bf, wu_bf):
    @pl.when(pl.program_id(1) == 0)
    def _():
        wg_bf[...] = wg_ref[...].astype(BF16)
        wu_bf[...] = wu_ref[...].astype(BF16)

    x = x_ref[...]
    g = jnp.dot(x, wg_bf[...], preferred_element_type=F32)
    u = jnp.dot(x, wu_bf[...], preferred_element_type=F32)
    o_ref[...] = ((g * jax.nn.sigmoid(g)) * u).astype(o_ref.dtype)


def _ffn_up(yn, w_gate_up):
    tm = ROW_TILE_BIG
    tn = 512
    nt = D_FF // tn
    return pl.pallas_call(
        _ffn_up_kernel,
        grid=(nt, N_ALL // tm),
        in_specs=[pl.BlockSpec((tm, D_MODEL), lambda j, i: (i, 0)),
                  pl.BlockSpec((D_MODEL, tn), lambda j, i: (0, j)),
                  pl.BlockSpec((D_MODEL, tn), lambda j, i: (0, j + nt))],
        out_specs=pl.BlockSpec((tm, tn), lambda j, i: (i, j)),
        out_shape=jax.ShapeDtypeStruct((N_ALL, D_FF), BF16),
        scratch_shapes=[pltpu.VMEM((D_MODEL, tn), BF16), pltpu.VMEM((D_MODEL, tn), BF16)],
        compiler_params=_params(2),
        name="ffn_up",
    )(yn, w_gate_up, w_gate_up)


def _ffn_down_kernel(a_ref, w_ref, r_ref, yp_ref, ys_ref):
    i = pl.program_id(0)
    n_prompt_tiles = N_P // FFN_DOWN_ROWS
    y = r_ref[...] + jnp.dot(a_ref[...], w_ref[...], preferred_element_type=F32)

    @pl.when(i < n_prompt_tiles)
    def _():
        yp_ref[...] = y

    @pl.when(i >= n_prompt_tiles)
    def _():
        ys_ref[...] = y


def _ffn_down(act, w_down_bf, y1):
    tm = FFN_DOWN_ROWS
    n_p = N_P // tm
    return pl.pallas_call(
        _ffn_down_kernel,
        grid=(N_ALL // tm,),
        in_specs=[pl.BlockSpec((tm, D_FF), lambda i: (i, 0)),
                  pl.BlockSpec((D_FF, D_MODEL), lambda i: (0, 0), pipeline_mode=pl.Buffered(1)),
                  pl.BlockSpec((tm, D_MODEL), lambda i: (i, 0))],
        out_specs=[pl.BlockSpec((tm, D_MODEL), lambda i: (jnp.minimum(i, n_p - 1), 0)),
                   pl.BlockSpec((tm, D_MODEL), lambda i: (jnp.maximum(i - n_p, 0), 0))],
        out_shape=[jax.ShapeDtypeStruct((N_P, D_MODEL), F32),
                   jax.ShapeDtypeStruct((N_S, D_MODEL), F32)],
        compiler_params=_params(1),
        name="ffn_down",
    )(act, w_down_bf, y1)


def _rope_tables():
    inv = ROPE_THETA ** (-jnp.arange(0, HEAD_DIM, 2, dtype=F32) / HEAD_DIM)
    inv_t = jnp.tile(inv, LANES // (HEAD_DIM // 2))[None, :]
    sign = jnp.asarray(np.tile(np.repeat([-1.0, 1.0], HEAD_DIM // 2), LANES // HEAD_DIM), F32)[None, :]
    ang_a = (jnp.arange(N_P // LANES, dtype=jnp.int32) * LANES).astype(F32)[:, None] * inv_t
    ang_b = jnp.arange(LANES, dtype=jnp.int32).astype(F32)[:, None] * inv_t
    ca, sa = jnp.cos(ang_a)[:, None, :], jnp.sin(ang_a)[:, None, :]
    cb, sb = jnp.cos(ang_b)[None, :, :], jnp.sin(ang_b)[None, :, :]
    cos_p = (ca * cb - sa * sb).reshape(N_P, LANES)
    sin_p = (sa * cb + ca * sb).reshape(N_P, LANES)
    ang_s = (PAST_LEN + jnp.arange(DEC_SEQ, dtype=jnp.int32)).astype(F32)[:, None] * inv_t
    cos_s = jnp.tile(jnp.cos(ang_s), (DEC_BATCH, 1))
    sin_s = jnp.tile(jnp.sin(ang_s), (DEC_BATCH, 1))
    return (jnp.concatenate([cos_p, cos_s], axis=0),
            jnp.concatenate([sin_p, sin_s], axis=0) * sign)


def kernel(x_prompt, x_sample, cache_k_win, cache_v_win, state_conv, norm_mix, w_in, q_norm, k_norm,
           sinks, conv_w, w_out, norm_ffn, w_gate_up, w_down):
    assert x_prompt.shape == (1, SEQ, D_MODEL) and x_sample.shape == (DEC_BATCH, DEC_SEQ, D_MODEL)
    assert w_in.shape == (1, D_MODEL, D_IN)
    assert cache_k_win.shape == (1, DEC_BATCH, WINDOW, N_KV_HEADS, HEAD_DIM)

    x_p = x_prompt.reshape(N_P, D_MODEL)
    x_s = x_sample.reshape(N_S, D_MODEL)
    w_in2 = w_in.reshape(D_MODEL, D_IN)
    w_out2 = w_out.reshape(D_ATTN, D_MODEL)
    w_gu2 = w_gate_up.reshape(D_MODEL, 2 * D_FF)
    w_dn2 = w_down.reshape(D_FF, D_MODEL)
    conv_w2 = conv_w.reshape(CONV_WIDTH, D_CONV)
    cache_kt = jnp.transpose(cache_k_win.reshape(DEC_BATCH, WINDOW, D_KV), (0, 2, 1))
    cache_vt = jnp.transpose(cache_v_win.reshape(DEC_BATCH, WINDOW, D_KV), (0, 2, 1))

    cos_t, sin_t = _rope_tables()
    q_g128 = jnp.tile(q_norm.reshape(1, HEAD_DIM) * Q_SCALE, (1, LANES // HEAD_DIM))
    k_g128 = jnp.tile(k_norm.reshape(1, HEAD_DIM), (1, LANES // HEAD_DIM))
    rest_scale = jnp.asarray(np.repeat([1.0, 1.0, 1.0, 0.5, 0.5], D_MODEL)[None, :], F32)
    head_of_lane = np.arange(LANES) // HEAD_DIM
    bd = jnp.asarray(np.where(head_of_lane[:, None] == head_of_lane[None, :], 1.0 / HEAD_DIM, 0.0),
                     dtype=BF16)
    sinks2 = sinks.reshape(1, N_HEADS).astype(F32)
    sink_col = jnp.broadcast_to(jnp.tile(sinks2.reshape(N_HEADS) * LOG2E, DEC_SEQ)[:, None],
                                (DEC_SEQ * N_HEADS, LANES))
    rep = jnp.asarray(np.tile(np.eye(HEAD_DIM), (1, N_KV_HEADS)), dtype=BF16)
    st_rows = jnp.pad(state_conv.reshape(DEC_BATCH, CONV_WIDTH - 1, D_CONV),
                      ((0, 0), (0, DEC_SEQ - (CONV_WIDTH - 1)), (0, 0))).reshape(N_S, D_CONV)

    xn, kv = _norm_proj_kv(x_p, x_s, norm_mix.reshape(1, D_MODEL), w_in2, k_g128, cos_t, sin_t, bd)
    q, rest = _proj_q_rest(xn, w_in2, q_g128, cos_t, sin_t, bd, rest_scale)

    mix_p, tail = _prompt_mix(sinks2, q, kv, rest, conv_w2)
    q_s = q[N_P:].reshape(DEC_BATCH, DEC_SEQ * N_HEADS, HEAD_DIM)
    attn_s, kwt, vwt, w_out_bf, w_down_bf = _sample_attn(q_s, kv, cache_kt, cache_vt, sink_col, rep,
                                                         w_out2, w_dn2)
    mix_s, u_s = _sample_merge(attn_s.reshape(N_S, D_ATTN), rest, st_rows, conv_w2)

    y1, yn = _out_proj(mix_p, mix_s, w_out_bf, x_p, x_s, norm_ffn.reshape(1, D_MODEL))
    act = _ffn_up(yn, w_gu2)
    y_p, y_s = _ffn_down(act, w_down_bf, y1)

    kv_tail = kv[N_P - WINDOW:N_P]
    win_shape = (1, DEC_BATCH, WINDOW, N_KV_HEADS, HEAD_DIM)
    return (y_p.reshape(1, SEQ, D_MODEL),
            y_s.reshape(DEC_BATCH, DEC_SEQ, D_MODEL),
            kv_tail[:, :D_KV].reshape(1, 1, WINDOW, N_KV_HEADS, HEAD_DIM),
            kv_tail[:, D_KV:].reshape(1, 1, WINDOW, N_KV_HEADS, HEAD_DIM),
            tail[8 - (CONV_WIDTH - 1):].reshape(1, 1, CONV_WIDTH - 1, D_CONV),
            jnp.transpose(kwt, (0, 2, 1)).reshape(win_shape),
            jnp.transpose(vwt, (0, 2, 1)).reshape(win_shape),
            u_s.reshape(DEC_BATCH, DEC_SEQ, D_CONV)[:, DEC_SEQ - (CONV_WIDTH - 1):][None])
```

```python
import math

import numpy as np
import jax
import jax.numpy as jnp
from jax import lax
from jax.experimental import pallas as pl
from jax.experimental.pallas import tpu as pltpu

F32 = jnp.float32
BF16 = jnp.bfloat16

D_MODEL = 2048
SEQ = 8192
DEC_BATCH = 128
DEC_SEQ = 4
PAST_LEN = 8192
N_HEADS = 32
HEAD_DIM = 64
N_KV_HEADS = 4
GROUP = N_HEADS // N_KV_HEADS
D_ATTN = N_HEADS * HEAD_DIM
D_KV = N_KV_HEADS * HEAD_DIM
WINDOW = 128
D_CONV = D_MODEL
CONV_WIDTH = 3
D_FF = 5632
ROPE_THETA = 10000.0
EPS = 1e-6
NEG_INF = -1e30
D_IN = 2 * D_ATTN + 2 * D_KV + 3 * D_CONV + D_CONV
D_REST = 5 * D_MODEL
LOG2E = math.log2(math.e)
Q_SCALE = HEAD_DIM ** -0.5 * LOG2E

N_P = SEQ
N_S = DEC_BATCH * DEC_SEQ
N_ALL = N_P + N_S

LANES = 128
VMEM_LIMIT = 56 * 1024 * 1024

ROW_TILE_SMALL = 512
ROW_TILE_BIG = 1088
FFN_DOWN_ROWS = 256
MERGE_CHUNK = (32, 256)
SEQ_BLOCK = 8
PAIRS_PER_GROUP = GROUP // 2


def _params(n_axes):
    return pltpu.CompilerParams(
        dimension_semantics=("arbitrary",) * n_axes, vmem_limit_bytes=VMEM_LIMIT)


def _rmsnorm_rows(x, g):
    ms = jnp.mean(x * x, axis=-1, keepdims=True)
    return x * lax.rsqrt(ms + EPS) * g


def _cast_weight_once(w_ref, wbf_ref, axis):
    @pl.when(pl.program_id(axis) == 0)
    def _():
        wbf_ref[...] = w_ref[...].astype(BF16)


def _headnorm_rope(zc, g128, cos_t, sin_t, bd):
    ms = jnp.dot((zc * zc).astype(BF16), bd, preferred_element_type=F32)
    y = zc * lax.rsqrt(ms + EPS) * g128
    lane = lax.broadcasted_iota(jnp.int32, y.shape, 1)
    first_half = (lane & (HEAD_DIM // 2)) == 0
    partner = jnp.where(first_half,
                        pltpu.roll(y, LANES - HEAD_DIM // 2, 1),
                        pltpu.roll(y, HEAD_DIM // 2, 1))
    return y * cos_t + partner * sin_t


def _norm_proj_kv_kernel(xp_ref, xs_ref, gm_ref, w_ref, g_ref, cos_ref, sin_ref, bd_ref,
                         xn_ref, kv_ref, wbf_ref):
    _cast_weight_once(w_ref, wbf_ref, 0)
    is_prompt = pl.program_id(0) < N_P // ROW_TILE_SMALL
    x = jnp.where(is_prompt, xp_ref[...], xs_ref[...])
    xn = _rmsnorm_rows(x, gm_ref[...]).astype(BF16)
    xn_ref[...] = xn
    z = jnp.dot(xn, wbf_ref[...], preferred_element_type=F32)
    g128 = g_ref[...]
    cos_t = cos_ref[...]
    sin_t = sin_ref[...]
    bd = bd_ref[...]
    for c in range(D_KV // LANES):
        zc = z[:, c * LANES:(c + 1) * LANES]
        kv_ref[:, c * LANES:(c + 1) * LANES] = _headnorm_rope(zc, g128, cos_t, sin_t, bd)
    kv_ref[:, D_KV:] = z[:, D_KV:]


def _norm_proj_kv(x_p, x_s, g_mix, w_in, k_g128, cos_t, sin_t, bd):
    tm = ROW_TILE_SMALL
    tn = 2 * D_KV
    last_p = N_P // tm - 1
    const = lambda shape: pl.BlockSpec(shape, lambda i: (0, 0))
    return pl.pallas_call(
        _norm_proj_kv_kernel,
        grid=(N_ALL // tm,),
        in_specs=[pl.BlockSpec((tm, D_MODEL), lambda i: (jnp.minimum(i, last_p), 0)),
                  const((tm, D_MODEL)),
                  const((1, D_MODEL)),
                  pl.BlockSpec((D_MODEL, tn), lambda i: (0, D_ATTN // tn)),
                  const((1, LANES)),
                  pl.BlockSpec((tm, LANES), lambda i: (i, 0)),
                  pl.BlockSpec((tm, LANES), lambda i: (i, 0)),
                  const((LANES, LANES))],
        out_specs=[pl.BlockSpec((tm, D_MODEL), lambda i: (i, 0)),
                   pl.BlockSpec((tm, tn), lambda i: (i, 0))],
        out_shape=[jax.ShapeDtypeStruct((N_ALL, D_MODEL), BF16),
                   jax.ShapeDtypeStruct((N_ALL, tn), F32)],
        scratch_shapes=[pltpu.VMEM((D_MODEL, tn), BF16)],
        compiler_params=_params(1),
        name="norm_proj_kv",
    )(x_p, x_s, g_mix, w_in, k_g128, cos_t, sin_t, bd)


def _proj_rest_kernel(x_ref, w_ref, scale_ref, wo_ref, o_ref, wo_bf_ref, wbf_ref):
    _cast_weight_once(w_ref, wbf_ref, 1)
    wo_bf_ref[...] = wo_ref[...].astype(BF16)
    z = jnp.dot(x_ref[...], wbf_ref[...], preferred_element_type=F32)
    o_ref[...] = (z * scale_ref[...]).astype(o_ref.dtype)


def _proj_q_kernel(x_ref, w_ref, g_ref, cos_ref, sin_ref, bd_ref, o_ref, wbf_ref):
    _cast_weight_once(w_ref, wbf_ref, 1)
    z = jnp.dot(x_ref[...], wbf_ref[...], preferred_element_type=F32)
    g128 = g_ref[...]
    cos_t = cos_ref[...]
    sin_t = sin_ref[...]
    bd = bd_ref[...]
    for c in range(z.shape[1] // LANES):
        zc = z[:, c * LANES:(c + 1) * LANES]
        out = _headnorm_rope(zc, g128, cos_t, sin_t, bd)
        o_ref[:, c * LANES:(c + 1) * LANES] = out.astype(o_ref.dtype)


def _proj_q_rest(xn, w_in, q_g128, cos_q, sin_q, bd, rest_scale, w_out):
    tm = ROW_TILE_BIG
    n_row = N_ALL // tm
    k = D_MODEL
    x_spec = pl.BlockSpec((tm, k), lambda j, i: (i, 0))
    rope_specs = [pl.BlockSpec((1, LANES), lambda j, i: (0, 0)),
                  pl.BlockSpec((tm, LANES), lambda j, i: (i, 0)),
                  pl.BlockSpec((tm, LANES), lambda j, i: (i, 0)),
                  pl.BlockSpec((LANES, LANES), lambda j, i: (0, 0))]

    tn_q = 1024
    q = pl.pallas_call(
        _proj_q_kernel,
        grid=(D_ATTN // tn_q, n_row),
        in_specs=[x_spec, pl.BlockSpec((k, tn_q), lambda j, i: (0, j))] + rope_specs,
        out_specs=pl.BlockSpec((tm, tn_q), lambda j, i: (i, j)),
        out_shape=jax.ShapeDtypeStruct((N_ALL, D_ATTN), BF16),
        scratch_shapes=[pltpu.VMEM((k, tn_q), BF16)],
        compiler_params=_params(2),
        name="proj_q",
    )(xn, w_in, q_g128, cos_q, sin_q, bd)

    tn_r = 1280
    rest_start = D_ATTN + 2 * D_KV
    n_col = D_REST // tn_r
    wo_rows = D_ATTN // (n_col * n_row)
    wo_spec = pl.BlockSpec((wo_rows, D_MODEL), lambda j, i: (j * n_row + i, 0))
    rest, w_out_bf = pl.pallas_call(
        _proj_rest_kernel,
        grid=(n_col, n_row),
        in_specs=[x_spec,
                  pl.BlockSpec((k, tn_r), lambda j, i: (0, j + rest_start // tn_r)),
                  pl.BlockSpec((1, tn_r), lambda j, i: (0, j)),
                  wo_spec],
        out_specs=[pl.BlockSpec((tm, tn_r), lambda j, i: (i, j)), wo_spec],
        out_shape=[jax.ShapeDtypeStruct((N_ALL, D_REST), BF16),
                   jax.ShapeDtypeStruct((D_ATTN, D_MODEL), BF16)],
        scratch_shapes=[pltpu.VMEM((k, tn_r), BF16)],
        compiler_params=_params(2),
        name="proj_rest",
    )(xn, w_in, rest_scale, w_out)
    return q, rest, w_out_bf


def _gated_merge(attn_half, u, u1, u2, b_ref, ga_half_ref, gc_half_ref, cw_ref):
    return _gated_merge_values(attn_half, u, u1, u2, b_ref[...].astype(F32),
                               ga_half_ref[...].astype(F32), gc_half_ref[...].astype(F32),
                               0.5 * cw_ref[...])


def _gated_merge_values(attn_half, u, u1, u2, b, ga_half, gc_half, cw_half):
    conv_half = cw_half[0:1, :] * u2 + cw_half[1:2, :] * u1 + cw_half[2:3, :] * u
    bc_half = b * conv_half
    ta = jnp.tanh(ga_half)
    tc = jnp.tanh(gc_half)
    return (ta * attn_half + attn_half) + (tc * bc_half + bc_half)


def _prompt_attention_block(j, sinks_ref, q_ref, kvp_ref, kvc_ref, attn_ref):
    n_keys = 2 * WINDOW
    kv = jnp.concatenate([kvp_ref[...], kvc_ref[...]], axis=0)
    key_row = lax.broadcasted_iota(jnp.int32, (n_keys, LANES), 0)
    low = lax.broadcasted_iota(jnp.int32, (n_keys, LANES), 1) < HEAD_DIM
    low_v = low & (key_row > 0)
    high_v = jnp.logical_not(low) & (key_row > 0)
    ones_low = jnp.where(low, 2.0, 0.0)
    ones_high = jnp.where(low, 0.0, 2.0)

    r = lax.broadcasted_iota(jnp.int32, (WINDOW, n_keys), 0)
    c = lax.broadcasted_iota(jnp.int32, (WINDOW, n_keys), 1)
    first_key = jnp.where(j > 0, 0, WINDOW)
    mask = jnp.where((c > r) & (c <= r + WINDOW) & (c >= first_key), 0.0, NEG_INF).astype(BF16)
    mask2 = jnp.concatenate([mask, mask], axis=1)
    eye = (lax.broadcasted_iota(jnp.int32, (WINDOW, WINDOW), 0)
           == lax.broadcasted_iota(jnp.int32, (WINDOW, WINDOW), 1))
    eye = jnp.where(eye, 1.0, 0.0).astype(BF16)
    sink_lane = lax.broadcasted_iota(jnp.int32, (WINDOW, LANES), 1) == 0
    no_keys = jnp.zeros((HEAD_DIM, n_keys), F32)

    for g in range(N_KV_HEADS):
        chunk = (g // 2) * LANES
        kc_t = kv[:, chunk:chunk + LANES].T
        kg_t = kc_t[(g % 2) * HEAD_DIM:(g % 2 + 1) * HEAD_DIM]
        vc = kv[:, D_KV + chunk:D_KV + chunk + LANES]
        if g % 2 == 0:
            v_low = jnp.where(low_v, vc, 0.0)
            v_high = pltpu.roll(v_low, HEAD_DIM, 1)
        else:
            v_high = jnp.where(high_v, vc, 0.0)
            v_low = pltpu.roll(v_high, HEAD_DIM, 1)
        k_bd_t = jnp.concatenate(
            [jnp.concatenate([kg_t, no_keys], axis=1),
             jnp.concatenate([no_keys, kg_t], axis=1)], axis=0).astype(BF16)
        k_aug = jnp.concatenate([k_bd_t, mask2], axis=0)
        v_bd = jnp.concatenate(
            [jnp.concatenate([v_low, ones_low], axis=1),
             jnp.concatenate([v_high, ones_high], axis=1)], axis=0).astype(BF16)

        for k in range(PAIRS_PER_GROUP):
            pair = g * PAIRS_PER_GROUP + k
            cols = slice(pair * LANES, (pair + 1) * LANES)
            q_aug = jnp.concatenate([q_ref[:, cols], eye], axis=1)
            s = jnp.dot(q_aug, k_aug, preferred_element_type=F32)
            halves = []
            for half in range(2):
                sh = s[:, half * n_keys:(half + 1) * n_keys]
                sink = sinks_ref[0, 2 * pair + half] * LOG2E
                sh = jnp.concatenate([jnp.where(sink_lane, sink, sh[:, :LANES]), sh[:, LANES:]], axis=1)
                halves.append(jnp.exp2(sh - jnp.max(sh, axis=1, keepdims=True)))
            p = jnp.concatenate(halves, axis=1).astype(BF16)
            o = jnp.dot(p, v_bd, preferred_element_type=F32)
            attn_ref[:, cols] = (o[:, :LANES] / o[:, LANES:]).astype(attn_ref.dtype)


def _prompt_mix_kernel(sinks_ref, q_ref, kvp_ref, kvc_ref, h_ref, b_ref, c_ref, ga_ref, gc_ref,
                       cw_ref, mix_ref, tail_ref, u_scr, attn_scr):
    j = pl.program_id(0)

    @pl.when(j == 0)
    def _():
        u_scr[0:8, :] = jnp.zeros((8, D_CONV), F32)

    _prompt_attention_block(j, sinks_ref, q_ref, kvp_ref, kvc_ref, attn_scr)

    rb, cb = MERGE_CHUNK
    cw_half = 0.5 * cw_ref[...]
    for r0 in range(0, WINDOW, rb):
        rows = slice(r0, r0 + rb)
        u_scr[8 + r0:8 + r0 + rb, :] = c_ref[rows, :].astype(F32) * h_ref[rows, :].astype(F32)
        for c0 in range(0, D_MODEL, cb):
            cols = slice(c0, c0 + cb)
            mix = _gated_merge_values(
                attn_scr[rows, cols],
                u_scr[8 + r0:8 + r0 + rb, cols], u_scr[7 + r0:7 + r0 + rb, cols],
                u_scr[6 + r0:6 + r0 + rb, cols], b_ref[rows, cols].astype(F32),
                ga_ref[rows, cols].astype(F32), gc_ref[rows, cols].astype(F32), cw_half[:, cols])
            mix_ref[rows, cols] = mix.astype(mix_ref.dtype)
    tail = u_scr[WINDOW:WINDOW + 8, :]
    u_scr[0:8, :] = tail
    tail_ref[...] = tail


def _prompt_mix(sinks, q, kv, rest, conv_w):
    nb = N_P // WINDOW
    blk = lambda c: pl.BlockSpec((WINDOW, D_MODEL), lambda j, c=c: (j, c))
    return pl.pallas_call(
        _prompt_mix_kernel,
        grid=(nb,),
        in_specs=[pl.BlockSpec(memory_space=pltpu.SMEM),
                  pl.BlockSpec((WINDOW, D_ATTN), lambda j: (j, 0)),
                  pl.BlockSpec((WINDOW, 2 * D_KV), lambda j: (jnp.maximum(j - 1, 0), 0)),
                  pl.BlockSpec((WINDOW, 2 * D_KV), lambda j: (j, 0)),
                  blk(0), blk(1), blk(2), blk(3), blk(4),
                  pl.BlockSpec((CONV_WIDTH, D_CONV), lambda j: (0, 0))],
        out_specs=[pl.BlockSpec((WINDOW, D_MODEL), lambda j: (j, 0)),
                   pl.BlockSpec((8, D_CONV), lambda j: (0, 0))],
        out_shape=[jax.ShapeDtypeStruct((N_P, D_MODEL), BF16),
                   jax.ShapeDtypeStruct((8, D_CONV), F32)],
        scratch_shapes=[pltpu.VMEM((8 + WINDOW, D_CONV), F32), pltpu.VMEM((WINDOW, D_ATTN), F32)],
        compiler_params=_params(1),
        name="prompt_mix",
    )(sinks, q, kv, kv, rest, rest, rest, rest, rest, conv_w)


def _sample_attn_kernel(q_ref, kvn_ref, ckt_ref, cvt_ref, sink_ref, rep_ref,
                        o_ref, kwt_ref, vwt_ref, bias_c, bias_n):
    n_new = SEQ_BLOCK * DEC_SEQ
    rows = DEC_SEQ * N_HEADS
    keep = WINDOW - DEC_SEQ

    @pl.when(pl.program_id(0) == 0)
    def _():
        row = lax.broadcasted_iota(jnp.int32, (rows, WINDOW), 0)
        col = lax.broadcasted_iota(jnp.int32, (rows, WINDOW), 1)
        t = row >> 5
        bias_c[...] = jnp.where(col > t, 0.0, NEG_INF)
        for s in range(SEQ_BLOCK):
            valid_new = (col < n_new) & ((col >> 2) == s) & ((col & 3) <= t)
            bias_n[s] = jnp.where(valid_new, 0.0, NEG_INF)

    kvn = jnp.concatenate([kvn_ref[...], jnp.zeros((WINDOW - n_new, 2 * D_KV), F32)], axis=0)
    kn_t = kvn[:, :D_KV].T
    vn_t = kvn[:, D_KV:].T
    kn_t_bf = kn_t.astype(BF16)
    vn_bf = kvn[:, D_KV:].astype(BF16)

    all_rows = SEQ_BLOCK * rows
    row2 = lax.broadcasted_iota(jnp.int32, (all_rows, D_KV), 0)
    col2 = lax.broadcasted_iota(jnp.int32, (all_rows, D_KV), 1)
    same_group = ((row2 & (N_HEADS - 1)) >> 3) == (col2 >> 6)
    win_lane = lax.broadcasted_iota(jnp.int32, (D_KV, WINDOW), 1)
    sink_b = jnp.concatenate([sink_ref[...]] * SEQ_BLOCK, axis=0)
    ones = jnp.ones((2 * WINDOW, LANES), BF16)

    q_all = q_ref[...].reshape(all_rows, HEAD_DIM)
    q4 = jnp.dot(q_all, rep_ref[...], preferred_element_type=F32)
    q4 = jnp.where(same_group, q4, 0.0).astype(BF16)
    s_n = jnp.dot(q4, kn_t_bf, preferred_element_type=F32) + bias_n[...].reshape(all_rows, WINDOW)
    s_c = jnp.concatenate(
        [jnp.dot(q4[s * rows:(s + 1) * rows], ckt_ref[s].astype(BF16), preferred_element_type=F32)
         + bias_c[...] for s in range(SEQ_BLOCK)], axis=0)
    m_b = jnp.maximum(jnp.max(jnp.maximum(s_c, s_n), axis=1, keepdims=True), sink_b)
    p_c = jnp.exp2(s_c - m_b).astype(BF16)
    p_n = jnp.exp2(s_n - m_b).astype(BF16)
    l_b = (jnp.dot(jnp.concatenate([p_c, p_n], axis=1), ones, preferred_element_type=F32)
           + jnp.exp2(sink_b - m_b))
    o4 = jnp.concatenate(
        [lax.dot_general(p_c[s * rows:(s + 1) * rows], cvt_ref[s].astype(BF16),
                         (((1,), (1,)), ((), ())), preferred_element_type=F32)
         for s in range(SEQ_BLOCK)], axis=0) + jnp.dot(p_n, vn_bf, preferred_element_type=F32)
    o4 = jnp.where(same_group, o4, 0.0)
    a = o4[:, 0:LANES] + o4[:, LANES:]
    o = (a + pltpu.roll(a, HEAD_DIM, 1)) / (2.0 * l_b)
    o_ref[...] = o[:, 0:HEAD_DIM].reshape(SEQ_BLOCK, rows, HEAD_DIM)

    for s in range(SEQ_BLOCK):
        k_t = ckt_ref[s]
        v_t = cvt_ref[s]
        new_shift = keep - s * DEC_SEQ
        kwt_ref[s] = jnp.where(win_lane >= keep, pltpu.roll(kn_t, new_shift, 1),
                               pltpu.roll(k_t, keep, 1))
        vwt_ref[s] = jnp.where(win_lane >= keep, pltpu.roll(vn_t, new_shift, 1),
                               pltpu.roll(v_t, keep, 1))


def _sample_attn(q_s, kv, cache_kt, cache_vt, sink_col, rep):
    rows = DEC_SEQ * N_HEADS
    n_new = SEQ_BLOCK * DEC_SEQ
    cache_spec = pl.BlockSpec((SEQ_BLOCK, D_KV, WINDOW), lambda i: (i, 0, 0))
    return pl.pallas_call(
        _sample_attn_kernel,
        grid=(DEC_BATCH // SEQ_BLOCK,),
        in_specs=[pl.BlockSpec((SEQ_BLOCK, rows, HEAD_DIM), lambda i: (i, 0, 0)),
                  pl.BlockSpec((n_new, 2 * D_KV), lambda i: (i + N_P // n_new, 0)),
                  cache_spec, cache_spec,
                  pl.BlockSpec((rows, LANES), lambda i: (0, 0)),
                  pl.BlockSpec((HEAD_DIM, D_KV), lambda i: (0, 0))],
        out_specs=[pl.BlockSpec((SEQ_BLOCK, rows, HEAD_DIM), lambda i: (i, 0, 0)),
                   cache_spec, cache_spec],
        out_shape=[jax.ShapeDtypeStruct((DEC_BATCH, rows, HEAD_DIM), F32),
                   jax.ShapeDtypeStruct((DEC_BATCH, D_KV, WINDOW), F32),
                   jax.ShapeDtypeStruct((DEC_BATCH, D_KV, WINDOW), F32)],
        scratch_shapes=[pltpu.VMEM((rows, WINDOW), F32), pltpu.VMEM((SEQ_BLOCK, rows, WINDOW), F32)],
        compiler_params=_params(1),
        name="sample_attn",
    )(q_s, kv, cache_kt, cache_vt, sink_col, rep)


def _sample_merge_kernel(attn_ref, h_ref, b_ref, c_ref, ga_ref, gc_ref, st_ref, cw_ref,
                         mix_ref, u_ref):
    u = c_ref[...].astype(F32) * h_ref[...].astype(F32)
    u_ref[...] = u
    rows = u.shape[0]
    t = lax.broadcasted_iota(jnp.int32, u.shape, 0) & (DEC_SEQ - 1)
    st = st_ref[...]
    u1 = jnp.where(t == 0, pltpu.roll(st, rows - 1, 0), pltpu.roll(u, 1, 0))
    u2 = jnp.where(t < 2, st, pltpu.roll(u, 2, 0))
    mix = _gated_merge(attn_ref[...], u, u1, u2, b_ref, ga_ref, gc_ref, cw_ref)
    mix_ref[...] = mix.astype(mix_ref.dtype)


def _sample_merge(attn_s, rest, st_rows, conv_w):
    tm = WINDOW
    off = N_P // tm
    blk = lambda c: pl.BlockSpec((tm, D_MODEL), lambda i, c=c: (i + off, c))
    return pl.pallas_call(
        _sample_merge_kernel,
        grid=(N_S // tm,),
        in_specs=[pl.BlockSpec((tm, D_ATTN), lambda i: (i, 0)),
                  blk(0), blk(1), blk(2), blk(3), blk(4),
                  pl.BlockSpec((tm, D_CONV), lambda i: (i, 0)),
                  pl.BlockSpec((CONV_WIDTH, D_CONV), lambda i: (0, 0))],
        out_specs=[pl.BlockSpec((tm, D_MODEL), lambda i: (i, 0)),
                   pl.BlockSpec((tm, D_CONV), lambda i: (i, 0))],
        out_shape=[jax.ShapeDtypeStruct((N_S, D_MODEL), BF16),
                   jax.ShapeDtypeStruct((N_S, D_CONV), F32)],
        compiler_params=_params(1),
        name="sample_merge",
    )(attn_s, rest, rest, rest, rest, rest, st_rows, conv_w)


def _out_proj_kernel(mp_ref, ms_ref, w_ref, xp_ref, xs_ref, gf_ref, y_ref, yn_ref):
    is_prompt = pl.program_id(0) < N_P // ROW_TILE_SMALL
    res = jnp.where(is_prompt, xp_ref[...], xs_ref[...])
    mix = jnp.where(is_prompt, mp_ref[...], ms_ref[...])
    y = res + jnp.dot(mix, w_ref[...], preferred_element_type=F32)
    y_ref[...] = y
    yn_ref[...] = _rmsnorm_rows(y, gf_ref[...]).astype(yn_ref.dtype)


def _out_proj(mix_p, mix_s, w_out_bf, x_p, x_s, g_ffn):
    tm = ROW_TILE_SMALL
    last_p = N_P // tm - 1
    once = pl.Buffered(1)
    return pl.pallas_call(
        _out_proj_kernel,
        grid=(N_ALL // tm,),
        in_specs=[pl.BlockSpec((tm, D_ATTN), lambda i: (jnp.minimum(i, last_p), 0)),
                  pl.BlockSpec((tm, D_ATTN), lambda i: (0, 0), pipeline_mode=once),
                  pl.BlockSpec((D_ATTN, D_MODEL), lambda i: (0, 0), pipeline_mode=once),
                  pl.BlockSpec((tm, D_MODEL), lambda i: (jnp.minimum(i, last_p), 0)),
                  pl.BlockSpec((tm, D_MODEL), lambda i: (0, 0), pipeline_mode=once),
                  pl.BlockSpec((1, D_MODEL), lambda i: (0, 0))],
        out_specs=[pl.BlockSpec((tm, D_MODEL), lambda i: (i, 0)),
                   pl.BlockSpec((tm, D_MODEL), lambda i: (i, 0))],
        out_shape=[jax.ShapeDtypeStruct((N_ALL, D_MODEL), F32),
                   jax.ShapeDtypeStruct((N_ALL, D_MODEL), BF16)],
        compiler_params=_params(1),
        name="out_proj",
    )(mix_p, mix_s, w_out_bf, x_p, x_s, g_ffn)


def _ffn_up_kernel(x_ref, wg_ref, wu_ref, wd_ref, o_ref, wd_bf_ref, wg_bf, wu_bf):
    @pl.when(pl.program_id(1) == 0)
    def _():
        wg_bf[...] = wg_ref[...].astype(BF16)
        wu_bf[...] = wu_ref[...].astype(BF16)

    wd_bf_ref[...] = wd_ref[...].astype(BF16)

    x = x_ref[...]
    g = jnp.dot(x, wg_bf[...], preferred_element_type=F32)
    u = jnp.dot(x, wu_bf[...], preferred_element_type=F32)
    o_ref[...] = ((g * jax.nn.sigmoid(g)) * u).astype(o_ref.dtype)


def _ffn_up(yn, w_gate_up, w_down):
    tm = ROW_TILE_BIG
    tn = 512
    nt = D_FF // tn
    n_row = N_ALL // tm
    wd_rows = D_FF // (nt * n_row)
    wd_spec = pl.BlockSpec((wd_rows, D_MODEL), lambda j, i: (j * n_row + i, 0))
    return pl.pallas_call(
        _ffn_up_kernel,
        grid=(nt, n_row),
        in_specs=[pl.BlockSpec((tm, D_MODEL), lambda j, i: (i, 0)),
                  pl.BlockSpec((D_MODEL, tn), lambda j, i: (0, j)),
                  pl.BlockSpec((D_MODEL, tn), lambda j, i: (0, j + nt)),
                  wd_spec],
        out_specs=[pl.BlockSpec((tm, tn), lambda j, i: (i, j)), wd_spec],
        out_shape=[jax.ShapeDtypeStruct((N_ALL, D_FF), BF16),
                   jax.ShapeDtypeStruct((D_FF, D_MODEL), BF16)],
        scratch_shapes=[pltpu.VMEM((D_MODEL, tn), BF16), pltpu.VMEM((D_MODEL, tn), BF16)],
        compiler_params=_params(2),
        name="ffn_up",
    )(yn, w_gate_up, w_gate_up, w_down)


def _ffn_down_kernel(a_ref, w_ref, r_ref, yp_ref, ys_ref):
    i = pl.program_id(0)
    n_prompt_tiles = N_P // FFN_DOWN_ROWS
    y = r_ref[...] + jnp.dot(a_ref[...], w_ref[...], preferred_element_type=F32)

    @pl.when(i < n_prompt_tiles)
    def _():
        yp_ref[...] = y

    @pl.when(i >= n_prompt_tiles)
    def _():
        ys_ref[...] = y


def _ffn_down(act, w_down_bf, y1):
    tm = FFN_DOWN_ROWS
    n_p = N_P // tm
    return pl.pallas_call(
        _ffn_down_kernel,
        grid=(N_ALL // tm,),
        in_specs=[pl.BlockSpec((tm, D_FF), lambda i: (i, 0)),
                  pl.BlockSpec((D_FF, D_MODEL), lambda i: (0, 0), pipeline_mode=pl.Buffered(1)),
                  pl.BlockSpec((tm, D_MODEL), lambda i: (i, 0))],
        out_specs=[pl.BlockSpec((tm, D_MODEL), lambda i: (jnp.minimum(i, n_p - 1), 0)),
                   pl.BlockSpec((tm, D_MODEL), lambda i: (jnp.maximum(i - n_p, 0), 0))],
        out_shape=[jax.ShapeDtypeStruct((N_P, D_MODEL), F32),
                   jax.ShapeDtypeStruct((N_S, D_MODEL), F32)],
        compiler_params=_params(1),
        name="ffn_down",
    )(act, w_down_bf, y1)


def _rope_tables():
    inv = ROPE_THETA ** (-jnp.arange(0, HEAD_DIM, 2, dtype=F32) / HEAD_DIM)
    inv_t = jnp.tile(inv, LANES // (HEAD_DIM // 2))[None, :]
    sign = jnp.asarray(np.tile(np.repeat([-1.0, 1.0], HEAD_DIM // 2), LANES // HEAD_DIM), F32)[None, :]
    ang_a = (jnp.arange(N_P // LANES, dtype=jnp.int32) * LANES).astype(F32)[:, None] * inv_t
    ang_b = jnp.arange(LANES, dtype=jnp.int32).astype(F32)[:, None] * inv_t
    ca, sa = jnp.cos(ang_a)[:, None, :], jnp.sin(ang_a)[:, None, :]
    cb, sb = jnp.cos(ang_b)[None, :, :], jnp.sin(ang_b)[None, :, :]
    cos_p = (ca * cb - sa * sb).reshape(N_P, LANES)
    sin_p = (sa * cb + ca * sb).reshape(N_P, LANES)
    ang_s = (PAST_LEN + jnp.arange(DEC_SEQ, dtype=jnp.int32)).astype(F32)[:, None] * inv_t
    cos_s = jnp.tile(jnp.cos(ang_s), (DEC_BATCH, 1))
    sin_s = jnp.tile(jnp.sin(ang_s), (DEC_BATCH, 1))
    return (jnp.concatenate([cos_p, cos_s], axis=0),
            jnp.concatenate([sin_p, sin_s], axis=0) * sign)


def kernel(x_prompt, x_sample, cache_k_win, cache_v_win, state_conv, norm_mix, w_in, q_norm, k_norm,
           sinks, conv_w, w_out, norm_ffn, w_gate_up, w_down):
    assert x_prompt.shape == (1, SEQ, D_MODEL) and x_sample.shape == (DEC_BATCH, DEC_SEQ, D_MODEL)
    assert w_in.shape == (1, D_MODEL, D_IN)
    assert cache_k_win.shape == (1, DEC_BATCH, WINDOW, N_KV_HEADS, HEAD_DIM)

    x_p = x_prompt.reshape(N_P, D_MODEL)
    x_s = x_sample.reshape(N_S, D_MODEL)
    w_in2 = w_in.reshape(D_MODEL, D_IN)
    w_out2 = w_out.reshape(D_ATTN, D_MODEL)
    w_gu2 = w_gate_up.reshape(D_MODEL, 2 * D_FF)
    w_dn2 = w_down.reshape(D_FF, D_MODEL)
    conv_w2 = conv_w.reshape(CONV_WIDTH, D_CONV)
    cache_kt = jnp.transpose(cache_k_win.reshape(DEC_BATCH, WINDOW, D_KV), (0, 2, 1))
    cache_vt = jnp.transpose(cache_v_win.reshape(DEC_BATCH, WINDOW, D_KV), (0, 2, 1))

    cos_t, sin_t = _rope_tables()
    q_g128 = jnp.tile(q_norm.reshape(1, HEAD_DIM) * Q_SCALE, (1, LANES // HEAD_DIM))
    k_g128 = jnp.tile(k_norm.reshape(1, HEAD_DIM), (1, LANES // HEAD_DIM))
    rest_scale = jnp.asarray(np.repeat([1.0, 1.0, 1.0, 0.5, 0.5], D_MODEL)[None, :], F32)
    head_of_lane = np.arange(LANES) // HEAD_DIM
    bd = jnp.asarray(np.where(head_of_lane[:, None] == head_of_lane[None, :], 1.0 / HEAD_DIM, 0.0),
                     dtype=BF16)
    sinks2 = sinks.reshape(1, N_HEADS).astype(F32)
    sink_col = jnp.broadcast_to(jnp.tile(sinks2.reshape(N_HEADS) * LOG2E, DEC_SEQ)[:, None],
                                (DEC_SEQ * N_HEADS, LANES))
    rep = jnp.asarray(np.tile(np.eye(HEAD_DIM), (1, N_KV_HEADS)), dtype=BF16)
    st_rows = jnp.pad(state_conv.reshape(DEC_BATCH, CONV_WIDTH - 1, D_CONV),
                      ((0, 0), (0, DEC_SEQ - (CONV_WIDTH - 1)), (0, 0))).reshape(N_S, D_CONV)

    xn, kv = _norm_proj_kv(x_p, x_s, norm_mix.reshape(1, D_MODEL), w_in2, k_g128, cos_t, sin_t, bd)
    q, rest, w_out_bf = _proj_q_rest(xn, w_in2, q_g128, cos_t, sin_t, bd, rest_scale, w_out2)

    mix_p, tail = _prompt_mix(sinks2, q, kv, rest, conv_w2)
    q_s = q[N_P:].reshape(DEC_BATCH, DEC_SEQ * N_HEADS, HEAD_DIM)
    attn_s, kwt, vwt = _sample_attn(q_s, kv, cache_kt, cache_vt, sink_col, rep)
    mix_s, u_s = _sample_merge(attn_s.reshape(N_S, D_ATTN), rest, st_rows, conv_w2)

    y1, yn = _out_proj(mix_p, mix_s, w_out_bf, x_p, x_s, norm_ffn.reshape(1, D_MODEL))
    act, w_down_bf = _ffn_up(yn, w_gu2, w_dn2)
    y_p, y_s = _ffn_down(act, w_down_bf, y1)

    kv_tail = kv[N_P - WINDOW:N_P]
    win_shape = (1, DEC_BATCH, WINDOW, N_KV_HEADS, HEAD_DIM)
    return (y_p.reshape(1, SEQ, D_MODEL),
            y_s.reshape(DEC_BATCH, DEC_SEQ, D_MODEL),
            kv_tail[:, :D_KV].reshape(1, 1, WINDOW, N_KV_HEADS, HEAD_DIM),
            kv_tail[:, D_KV:].reshape(1, 1, WINDOW, N_KV_HEADS, HEAD_DIM),
            tail[8 - (CONV_WIDTH - 1):].reshape(1, 1, CONV_WIDTH - 1, D_CONV),
            jnp.transpose(kwt, (0, 2, 1)).reshape(win_shape),
            jnp.transpose(vwt, (0, 2, 1)).reshape(win_shape),
            u_s.reshape(DEC_BATCH, DEC_SEQ, D_CONV)[:, DEC_SEQ - (CONV_WIDTH - 1):][None])
```

```python
import math

import numpy as np
import jax
import jax.numpy as jnp
from jax import lax
from jax.experimental import pallas as pl
from jax.experimental.pallas import tpu as pltpu

F32 = jnp.float32
BF16 = jnp.bfloat16

D_MODEL = 2048
SEQ = 8192
DEC_BATCH = 128
DEC_SEQ = 4
PAST_LEN = 8192
N_HEADS = 32
HEAD_DIM = 64
N_KV_HEADS = 4
GROUP = N_HEADS // N_KV_HEADS
D_ATTN = N_HEADS * HEAD_DIM
D_KV = N_KV_HEADS * HEAD_DIM
WINDOW = 128
D_CONV = D_MODEL
CONV_WIDTH = 3
D_FF = 5632
ROPE_THETA = 10000.0
EPS = 1e-6
NEG_INF = -1e30
D_IN = 2 * D_ATTN + 2 * D_KV + 3 * D_CONV + D_CONV
D_REST = 5 * D_MODEL
LOG2E = math.log2(math.e)
Q_SCALE = HEAD_DIM ** -0.5 * LOG2E

N_P = SEQ
N_S = DEC_BATCH * DEC_SEQ
N_ALL = N_P + N_S

LANES = 128
VMEM_LIMIT = 56 * 1024 * 1024

ROW_TILE_SMALL = 512
ROW_TILE_BIG = 1088
ROW_TILE_HUGE = 2176
MXU_WIDTH = 256
FFN_DOWN_ROWS = 256
MERGE_CHUNK = (32, 256)
SEQ_BLOCK = 8
PAIRS_PER_GROUP = GROUP // 2


def _params(n_axes):
    return pltpu.CompilerParams(
        dimension_semantics=("arbitrary",) * n_axes, vmem_limit_bytes=VMEM_LIMIT)


def _rmsnorm_rows(x, g):
    ms = jnp.mean(x * x, axis=-1, keepdims=True)
    return x * lax.rsqrt(ms + EPS) * g


def _cast_weight_once(w_ref, wbf_ref, axis):
    @pl.when(pl.program_id(axis) == 0)
    def _():
        wbf_ref[...] = w_ref[...].astype(BF16)


def _headnorm_rope(zc, g128, cos_t, sin_t, bd):
    ms = jnp.dot((zc * zc).astype(BF16), bd, preferred_element_type=F32)
    y = zc * lax.rsqrt(ms + EPS) * g128
    lane = lax.broadcasted_iota(jnp.int32, y.shape, 1)
    first_half = (lane & (HEAD_DIM // 2)) == 0
    partner = jnp.where(first_half,
                        pltpu.roll(y, LANES - HEAD_DIM // 2, 1),
                        pltpu.roll(y, HEAD_DIM // 2, 1))
    return y * cos_t + partner * sin_t


def _norm_proj_kv_kernel(xp_ref, xs_ref, gm_ref, w_ref, g_ref, cos_ref, sin_ref, bd_ref,
                         xn_ref, kv_ref, wbf_ref):
    _cast_weight_once(w_ref, wbf_ref, 0)
    is_prompt = pl.program_id(0) < N_P // ROW_TILE_SMALL
    x = jnp.where(is_prompt, xp_ref[...], xs_ref[...])
    xn = _rmsnorm_rows(x, gm_ref[...]).astype(BF16)
    xn_ref[...] = xn
    z = jnp.dot(xn, wbf_ref[...], preferred_element_type=F32)
    g128 = g_ref[...]
    cos_t = cos_ref[...]
    sin_t = sin_ref[...]
    bd = bd_ref[...]
    for c in range(D_KV // LANES):
        zc = z[:, c * LANES:(c + 1) * LANES]
        kv_ref[:, c * LANES:(c + 1) * LANES] = _headnorm_rope(zc, g128, cos_t, sin_t, bd)
    kv_ref[:, D_KV:] = z[:, D_KV:]


def _norm_proj_kv(x_p, x_s, g_mix, w_in, k_g128, cos_t, sin_t, bd):
    tm = ROW_TILE_SMALL
    tn = 2 * D_KV
    last_p = N_P // tm - 1
    const = lambda shape: pl.BlockSpec(shape, lambda i: (0, 0))
    return pl.pallas_call(
        _norm_proj_kv_kernel,
        grid=(N_ALL // tm,),
        in_specs=[pl.BlockSpec((tm, D_MODEL), lambda i: (jnp.minimum(i, last_p), 0)),
                  const((tm, D_MODEL)),
                  const((1, D_MODEL)),
                  pl.BlockSpec((D_MODEL, tn), lambda i: (0, D_ATTN // tn)),
                  const((1, LANES)),
                  pl.BlockSpec((tm, LANES), lambda i: (i, 0)),
                  pl.BlockSpec((tm, LANES), lambda i: (i, 0)),
                  const((LANES, LANES))],
        out_specs=[pl.BlockSpec((tm, D_MODEL), lambda i: (i, 0)),
                   pl.BlockSpec((tm, tn), lambda i: (i, 0))],
        out_shape=[jax.ShapeDtypeStruct((N_ALL, D_MODEL), BF16),
                   jax.ShapeDtypeStruct((N_ALL, tn), F32)],
        scratch_shapes=[pltpu.VMEM((D_MODEL, tn), BF16)],
        compiler_params=_params(1),
        name="norm_proj_kv",
    )(x_p, x_s, g_mix, w_in, k_g128, cos_t, sin_t, bd)


def _proj_rest_kernel(x_ref, w_ref, scale_ref, o_ref, wbf_ref):
    _cast_weight_once(w_ref, wbf_ref, 1)
    for c0 in range(0, o_ref.shape[1], MXU_WIDTH):
        cols = slice(c0, c0 + MXU_WIDTH)
        z = jnp.dot(x_ref[...], wbf_ref[:, cols], preferred_element_type=F32)
        o_ref[:, cols] = (z * scale_ref[:, cols]).astype(o_ref.dtype)


def _proj_q_kernel(x_ref, w_ref, g_ref, cos_ref, sin_ref, bd_ref, wo_ref, o_ref, wo_bf_ref, wbf_ref):
    _cast_weight_once(w_ref, wbf_ref, 1)
    wo_bf_ref[...] = wo_ref[...].astype(BF16)
    z = jnp.dot(x_ref[...], wbf_ref[...], preferred_element_type=F32)
    g128 = g_ref[...]
    cos_t = cos_ref[...]
    sin_t = sin_ref[...]
    bd = bd_ref[...]
    for c in range(z.shape[1] // LANES):
        zc = z[:, c * LANES:(c + 1) * LANES]
        out = _headnorm_rope(zc, g128, cos_t, sin_t, bd)
        o_ref[:, c * LANES:(c + 1) * LANES] = out.astype(o_ref.dtype)


def _proj_q_rest(xn, w_in, q_g128, cos_q, sin_q, bd, rest_scale, w_out):
    tm = ROW_TILE_BIG
    n_row = N_ALL // tm
    k = D_MODEL
    x_spec = pl.BlockSpec((tm, k), lambda j, i: (i, 0))
    rope_specs = [pl.BlockSpec((1, LANES), lambda j, i: (0, 0)),
                  pl.BlockSpec((tm, LANES), lambda j, i: (i, 0)),
                  pl.BlockSpec((tm, LANES), lambda j, i: (i, 0)),
                  pl.BlockSpec((LANES, LANES), lambda j, i: (0, 0))]

    tn_q = 1024
    n_col_q = D_ATTN // tn_q
    wo_rows = D_ATTN // (n_col_q * n_row)
    wo_spec = pl.BlockSpec((wo_rows, D_MODEL), lambda j, i: (j * n_row + i, 0))
    q, w_out_bf = pl.pallas_call(
        _proj_q_kernel,
        grid=(n_col_q, n_row),
        in_specs=[x_spec, pl.BlockSpec((k, tn_q), lambda j, i: (0, j))] + rope_specs + [wo_spec],
        out_specs=[pl.BlockSpec((tm, tn_q), lambda j, i: (i, j)), wo_spec],
        out_shape=[jax.ShapeDtypeStruct((N_ALL, D_ATTN), BF16),
                   jax.ShapeDtypeStruct((D_ATTN, D_MODEL), BF16)],
        scratch_shapes=[pltpu.VMEM((k, tn_q), BF16)],
        compiler_params=_params(2),
        name="proj_q",
    )(xn, w_in, q_g128, cos_q, sin_q, bd, w_out)

    tm_r = ROW_TILE_HUGE
    tn_r = 1024
    rest_start = D_ATTN + 2 * D_KV
    rest = pl.pallas_call(
        _proj_rest_kernel,
        grid=(D_REST // tn_r, N_ALL // tm_r),
        in_specs=[pl.BlockSpec((tm_r, k), lambda j, i: (i, 0)),
                  pl.BlockSpec((pl.Element(k), pl.Element(tn_r)),
                               lambda j, i: (0, pl.multiple_of(rest_start + j * tn_r, LANES))),
                  pl.BlockSpec((1, tn_r), lambda j, i: (0, j))],
        out_specs=pl.BlockSpec((tm_r, tn_r), lambda j, i: (i, j)),
        out_shape=jax.ShapeDtypeStruct((N_ALL, D_REST), BF16),
        scratch_shapes=[pltpu.VMEM((k, tn_r), BF16)],
        compiler_params=_params(2),
        name="proj_rest",
    )(xn, w_in, rest_scale)
    return q, rest, w_out_bf


def _gated_merge(attn_half, u, u1, u2, b_ref, ga_half_ref, gc_half_ref, cw_ref):
    return _gated_merge_values(attn_half, u, u1, u2, b_ref[...].astype(F32),
                               ga_half_ref[...].astype(F32), gc_half_ref[...].astype(F32),
                               0.5 * cw_ref[...])


def _gated_merge_values(attn_half, u, u1, u2, b, ga_half, gc_half, cw_half):
    conv_half = cw_half[0:1, :] * u2 + cw_half[1:2, :] * u1 + cw_half[2:3, :] * u
    bc_half = b * conv_half
    ta = jnp.tanh(ga_half)
    tc = jnp.tanh(gc_half)
    return (ta * attn_half + attn_half) + (tc * bc_half + bc_half)


def _prompt_attention_block(j, sinks_ref, q_ref, kvp_ref, kvc_ref, attn_ref):
    n_keys = 2 * WINDOW
    kv = jnp.concatenate([kvp_ref[...], kvc_ref[...]], axis=0)
    key_row = lax.broadcasted_iota(jnp.int32, (n_keys, LANES), 0)
    low = lax.broadcasted_iota(jnp.int32, (n_keys, LANES), 1) < HEAD_DIM
    low_v = low & (key_row > 0)
    high_v = jnp.logical_not(low) & (key_row > 0)
    ones_low = jnp.where(low, 2.0, 0.0)
    ones_high = jnp.where(low, 0.0, 2.0)

    r = lax.broadcasted_iota(jnp.int32, (WINDOW, n_keys), 0)
    c = lax.broadcasted_iota(jnp.int32, (WINDOW, n_keys), 1)
    first_key = jnp.where(j > 0, 0, WINDOW)
    mask = jnp.where((c > r) & (c <= r + WINDOW) & (c >= first_key), 0.0, NEG_INF).astype(BF16)
    mask2 = jnp.concatenate([mask, mask], axis=1)
    eye = (lax.broadcasted_iota(jnp.int32, (WINDOW, WINDOW), 0)
           == lax.broadcasted_iota(jnp.int32, (WINDOW, WINDOW), 1))
    eye = jnp.where(eye, 1.0, 0.0).astype(BF16)
    sink_lane = lax.broadcasted_iota(jnp.int32, (WINDOW, LANES), 1) == 0
    no_keys = jnp.zeros((HEAD_DIM, n_keys), F32)

    for g in range(N_KV_HEADS):
        chunk = (g // 2) * LANES
        kc_t = kv[:, chunk:chunk + LANES].T
        kg_t = kc_t[(g % 2) * HEAD_DIM:(g % 2 + 1) * HEAD_DIM]
        vc = kv[:, D_KV + chunk:D_KV + chunk + LANES]
        if g % 2 == 0:
            v_low = jnp.where(low_v, vc, 0.0)
            v_high = pltpu.roll(v_low, HEAD_DIM, 1)
        else:
            v_high = jnp.where(high_v, vc, 0.0)
            v_low = pltpu.roll(v_high, HEAD_DIM, 1)
        k_bd_t = jnp.concatenate(
            [jnp.concatenate([kg_t, no_keys], axis=1),
             jnp.concatenate([no_keys, kg_t], axis=1)], axis=0).astype(BF16)
        k_aug = jnp.concatenate([k_bd_t, mask2], axis=0)
        v_bd = jnp.concatenate(
            [jnp.concatenate([v_low, ones_low], axis=1),
             jnp.concatenate([v_high, ones_high], axis=1)], axis=0).astype(BF16)

        for k in range(PAIRS_PER_GROUP):
            pair = g * PAIRS_PER_GROUP + k
            cols = slice(pair * LANES, (pair + 1) * LANES)
            q_aug = jnp.concatenate([q_ref[:, cols], eye], axis=1)
            s = jnp.dot(q_aug, k_aug, preferred_element_type=F32)
            halves = []
            for half in range(2):
                sh = s[:, half * n_keys:(half + 1) * n_keys]
                sink = sinks_ref[0, 2 * pair + half] * LOG2E
                sh = jnp.concatenate([jnp.where(sink_lane, sink, sh[:, :LANES]), sh[:, LANES:]], axis=1)
                halves.append(jnp.exp2(sh - jnp.max(sh, axis=1, keepdims=True)))
            p = jnp.concatenate(halves, axis=1).astype(BF16)
            o = jnp.dot(p, v_bd, preferred_element_type=F32)
            attn_ref[:, cols] = (o[:, :LANES] / o[:, LANES:]).astype(attn_ref.dtype)


def _prompt_mix_kernel(sinks_ref, q_ref, kvp_ref, kvc_ref, h_ref, b_ref, c_ref, ga_ref, gc_ref,
                       cw_ref, mix_ref, tail_ref, u_scr, attn_scr):
    j = pl.program_id(0)

    @pl.when(j == 0)
    def _():
        u_scr[0:8, :] = jnp.zeros((8, D_CONV), F32)

    _prompt_attention_block(j, sinks_ref, q_ref, kvp_ref, kvc_ref, attn_scr)

    rb, cb = MERGE_CHUNK
    cw_half = 0.5 * cw_ref[...]
    for r0 in range(0, WINDOW, rb):
        rows = slice(r0, r0 + rb)
        u_scr[8 + r0:8 + r0 + rb, :] = c_ref[rows, :].astype(F32) * h_ref[rows, :].astype(F32)
        for c0 in range(0, D_MODEL, cb):
            cols = slice(c0, c0 + cb)
            mix = _gated_merge_values(
                attn_scr[rows, cols],
                u_scr[8 + r0:8 + r0 + rb, cols], u_scr[7 + r0:7 + r0 + rb, cols],
                u_scr[6 + r0:6 + r0 + rb, cols], b_ref[rows, cols].astype(F32),
                ga_ref[rows, cols].astype(F32), gc_ref[rows, cols].astype(F32), cw_half[:, cols])
            mix_ref[rows, cols] = mix.astype(mix_ref.dtype)
    tail = u_scr[WINDOW:WINDOW + 8, :]
    u_scr[0:8, :] = tail
    tail_ref[...] = tail


def _prompt_mix(sinks, q, kv, rest, conv_w):
    nb = N_P // WINDOW
    blk = lambda c: pl.BlockSpec((WINDOW, D_MODEL), lambda j, c=c: (j, c))
    return pl.pallas_call(
        _prompt_mix_kernel,
        grid=(nb,),
        in_specs=[pl.BlockSpec(memory_space=pltpu.SMEM),
                  pl.BlockSpec((WINDOW, D_ATTN), lambda j: (j, 0)),
                  pl.BlockSpec((WINDOW, 2 * D_KV), lambda j: (jnp.maximum(j - 1, 0), 0)),
                  pl.BlockSpec((WINDOW, 2 * D_KV), lambda j: (j, 0)),
                  blk(0), blk(1), blk(2), blk(3), blk(4),
                  pl.BlockSpec((CONV_WIDTH, D_CONV), lambda j: (0, 0))],
        out_specs=[pl.BlockSpec((WINDOW, D_MODEL), lambda j: (j, 0)),
                   pl.BlockSpec((8, D_CONV), lambda j: (0, 0))],
        out_shape=[jax.ShapeDtypeStruct((N_P, D_MODEL), BF16),
                   jax.ShapeDtypeStruct((8, D_CONV), F32)],
        scratch_shapes=[pltpu.VMEM((8 + WINDOW, D_CONV), F32), pltpu.VMEM((WINDOW, D_ATTN), F32)],
        compiler_params=_params(1),
        name="prompt_mix",
    )(sinks, q, kv, kv, rest, rest, rest, rest, rest, conv_w)


def _sample_attn_kernel(q_ref, kvn_ref, ckt_ref, cvt_ref, sink_ref, rep_ref,
                        o_ref, kwt_ref, vwt_ref, bias_c, bias_n):
    n_new = SEQ_BLOCK * DEC_SEQ
    rows = DEC_SEQ * N_HEADS
    keep = WINDOW - DEC_SEQ

    @pl.when(pl.program_id(0) == 0)
    def _():
        row = lax.broadcasted_iota(jnp.int32, (rows, WINDOW), 0)
        col = lax.broadcasted_iota(jnp.int32, (rows, WINDOW), 1)
        t = row >> 5
        bias_c[...] = jnp.where(col > t, 0.0, NEG_INF)
        for s in range(SEQ_BLOCK):
            valid_new = (col < n_new) & ((col >> 2) == s) & ((col & 3) <= t)
            bias_n[s] = jnp.where(valid_new, 0.0, NEG_INF)

    kvn = jnp.concatenate([kvn_ref[...], jnp.zeros((WINDOW - n_new, 2 * D_KV), F32)], axis=0)
    kn_t = kvn[:, :D_KV].T
    vn_t = kvn[:, D_KV:].T
    kn_t_bf = kn_t.astype(BF16)
    vn_bf = kvn[:, D_KV:].astype(BF16)

    all_rows = SEQ_BLOCK * rows
    row2 = lax.broadcasted_iota(jnp.int32, (all_rows, D_KV), 0)
    col2 = lax.broadcasted_iota(jnp.int32, (all_rows, D_KV), 1)
    same_group = ((row2 & (N_HEADS - 1)) >> 3) == (col2 >> 6)
    win_lane = lax.broadcasted_iota(jnp.int32, (D_KV, WINDOW), 1)
    sink_b = jnp.concatenate([sink_ref[...]] * SEQ_BLOCK, axis=0)
    ones = jnp.ones((2 * WINDOW, LANES), BF16)

    q_all = q_ref[...].reshape(all_rows, HEAD_DIM)
    q4 = jnp.dot(q_all, rep_ref[...], preferred_element_type=F32)
    q4 = jnp.where(same_group, q4, 0.0).astype(BF16)
    s_n = jnp.dot(q4, kn_t_bf, preferred_element_type=F32) + bias_n[...].reshape(all_rows, WINDOW)
    s_c = jnp.concatenate(
        [jnp.dot(q4[s * rows:(s + 1) * rows], ckt_ref[s].astype(BF16), preferred_element_type=F32)
         + bias_c[...] for s in range(SEQ_BLOCK)], axis=0)
    m_b = jnp.maximum(jnp.max(jnp.maximum(s_c, s_n), axis=1, keepdims=True), sink_b)
    p_c = jnp.exp2(s_c - m_b).astype(BF16)
    p_n = jnp.exp2(s_n - m_b).astype(BF16)
    l_b = (jnp.dot(jnp.concatenate([p_c, p_n], axis=1), ones, preferred_element_type=F32)
           + jnp.exp2(sink_b - m_b))
    o4 = jnp.concatenate(
        [lax.dot_general(p_c[s * rows:(s + 1) * rows], cvt_ref[s].astype(BF16),
                         (((1,), (1,)), ((), ())), preferred_element_type=F32)
         for s in range(SEQ_BLOCK)], axis=0) + jnp.dot(p_n, vn_bf, preferred_element_type=F32)
    o4 = jnp.where(same_group, o4, 0.0)
    a = o4[:, 0:LANES] + o4[:, LANES:]
    o = (a + pltpu.roll(a, HEAD_DIM, 1)) / (2.0 * l_b)
    o_ref[...] = o[:, 0:HEAD_DIM].reshape(SEQ_BLOCK, rows, HEAD_DIM)

    for s in range(SEQ_BLOCK):
        k_t = ckt_ref[s]
        v_t = cvt_ref[s]
        new_shift = keep - s * DEC_SEQ
        kwt_ref[s] = jnp.where(win_lane >= keep, pltpu.roll(kn_t, new_shift, 1),
                               pltpu.roll(k_t, keep, 1))
        vwt_ref[s] = jnp.where(win_lane >= keep, pltpu.roll(vn_t, new_shift, 1),
                               pltpu.roll(v_t, keep, 1))


def _sample_attn(q_s, kv, cache_kt, cache_vt, sink_col, rep):
    rows = DEC_SEQ * N_HEADS
    n_new = SEQ_BLOCK * DEC_SEQ
    cache_spec = pl.BlockSpec((SEQ_BLOCK, D_KV, WINDOW), lambda i: (i, 0, 0))
    return pl.pallas_call(
        _sample_attn_kernel,
        grid=(DEC_BATCH // SEQ_BLOCK,),
        in_specs=[pl.BlockSpec((SEQ_BLOCK, rows, HEAD_DIM), lambda i: (i, 0, 0)),
                  pl.BlockSpec((n_new, 2 * D_KV), lambda i: (i + N_P // n_new, 0)),
                  cache_spec, cache_spec,
                  pl.BlockSpec((rows, LANES), lambda i: (0, 0)),
                  pl.BlockSpec((HEAD_DIM, D_KV), lambda i: (0, 0))],
        out_specs=[pl.BlockSpec((SEQ_BLOCK, rows, HEAD_DIM), lambda i: (i, 0, 0)),
                   cache_spec, cache_spec],
        out_shape=[jax.ShapeDtypeStruct((DEC_BATCH, rows, HEAD_DIM), F32),
                   jax.ShapeDtypeStruct((DEC_BATCH, D_KV, WINDOW), F32),
                   jax.ShapeDtypeStruct((DEC_BATCH, D_KV, WINDOW), F32)],
        scratch_shapes=[pltpu.VMEM((rows, WINDOW), F32), pltpu.VMEM((SEQ_BLOCK, rows, WINDOW), F32)],
        compiler_params=_params(1),
        name="sample_attn",
    )(q_s, kv, cache_kt, cache_vt, sink_col, rep)


def _sample_merge_kernel(attn_ref, h_ref, b_ref, c_ref, ga_ref, gc_ref, st_ref, cw_ref,
                         mix_ref, u_ref):
    u = c_ref[...].astype(F32) * h_ref[...].astype(F32)
    u_ref[...] = u
    rows = u.shape[0]
    t = lax.broadcasted_iota(jnp.int32, u.shape, 0) & (DEC_SEQ - 1)
    st = st_ref[...]
    u1 = jnp.where(t == 0, pltpu.roll(st, rows - 1, 0), pltpu.roll(u, 1, 0))
    u2 = jnp.where(t < 2, st, pltpu.roll(u, 2, 0))
    mix = _gated_merge(attn_ref[...], u, u1, u2, b_ref, ga_ref, gc_ref, cw_ref)
    mix_ref[...] = mix.astype(mix_ref.dtype)


def _sample_merge(attn_s, rest, st_rows, conv_w):
    tm = WINDOW
    off = N_P // tm
    blk = lambda c: pl.BlockSpec((tm, D_MODEL), lambda i, c=c: (i + off, c))
    return pl.pallas_call(
        _sample_merge_kernel,
        grid=(N_S // tm,),
        in_specs=[pl.BlockSpec((tm, D_ATTN), lambda i: (i, 0)),
                  blk(0), blk(1), blk(2), blk(3), blk(4),
                  pl.BlockSpec((tm, D_CONV), lambda i: (i, 0)),
                  pl.BlockSpec((CONV_WIDTH, D_CONV), lambda i: (0, 0))],
        out_specs=[pl.BlockSpec((tm, D_MODEL), lambda i: (i, 0)),
                   pl.BlockSpec((tm, D_CONV), lambda i: (i, 0))],
        out_shape=[jax.ShapeDtypeStruct((N_S, D_MODEL), BF16),
                   jax.ShapeDtypeStruct((N_S, D_CONV), F32)],
        compiler_params=_params(1),
        name="sample_merge",
    )(attn_s, rest, rest, rest, rest, rest, st_rows, conv_w)


def _out_proj_kernel(mp_ref, ms_ref, w_ref, xp_ref, xs_ref, gf_ref, y_ref, yn_ref):
    is_prompt = pl.program_id(0) < N_P // ROW_TILE_SMALL
    res = jnp.where(is_prompt, xp_ref[...], xs_ref[...])
    mix = jnp.where(is_prompt, mp_ref[...], ms_ref[...])
    y = res + jnp.dot(mix, w_ref[...], preferred_element_type=F32)
    y_ref[...] = y
    yn_ref[...] = _rmsnorm_rows(y, gf_ref[...]).astype(yn_ref.dtype)


def _out_proj(mix_p, mix_s, w_out_bf, x_p, x_s, g_ffn):
    tm = ROW_TILE_SMALL
    last_p = N_P // tm - 1
    once = pl.Buffered(1)
    return pl.pallas_call(
        _out_proj_kernel,
        grid=(N_ALL // tm,),
        in_specs=[pl.BlockSpec((tm, D_ATTN), lambda i: (jnp.minimum(i, last_p), 0)),
                  pl.BlockSpec((tm, D_ATTN), lambda i: (0, 0), pipeline_mode=once),
                  pl.BlockSpec((D_ATTN, D_MODEL), lambda i: (0, 0), pipeline_mode=once),
                  pl.BlockSpec((tm, D_MODEL), lambda i: (jnp.minimum(i, last_p), 0)),
                  pl.BlockSpec((tm, D_MODEL), lambda i: (0, 0), pipeline_mode=once),
                  pl.BlockSpec((1, D_MODEL), lambda i: (0, 0))],
        out_specs=[pl.BlockSpec((tm, D_MODEL), lambda i: (i, 0)),
                   pl.BlockSpec((tm, D_MODEL), lambda i: (i, 0))],
        out_shape=[jax.ShapeDtypeStruct((N_ALL, D_MODEL), F32),
                   jax.ShapeDtypeStruct((N_ALL, D_MODEL), BF16)],
        compiler_params=_params(1),
        name="out_proj",
    )(mix_p, mix_s, w_out_bf, x_p, x_s, g_ffn)


def _ffn_up_kernel(x_ref, wg_ref, wu_ref, wd_ref, o_ref, wd_bf_ref, wg_bf, wu_bf):
    @pl.when(pl.program_id(1) == 0)
    def _():
        wg_bf[...] = wg_ref[...].astype(BF16)
        wu_bf[...] = wu_ref[...].astype(BF16)

    wd_bf_ref[...] = wd_ref[...].astype(BF16)

    for c0 in range(0, o_ref.shape[1], MXU_WIDTH):
        cols = slice(c0, c0 + MXU_WIDTH)
        g = jnp.dot(x_ref[...], wg_bf[:, cols], preferred_element_type=F32)
        u = jnp.dot(x_ref[...], wu_bf[:, cols], preferred_element_type=F32)
        o_ref[:, cols] = ((g * jax.nn.sigmoid(g)) * u).astype(o_ref.dtype)


def _ffn_up(yn, w_gate_up, w_down):
    tm = ROW_TILE_HUGE
    tn = 512
    nt = D_FF // tn
    n_row = N_ALL // tm
    wd_rows = D_FF // (nt * n_row)
    wd_spec = pl.BlockSpec((wd_rows, D_MODEL), lambda j, i: (j * n_row + i, 0))
    return pl.pallas_call(
        _ffn_up_kernel,
        grid=(nt, n_row),
        in_specs=[pl.BlockSpec((tm, D_MODEL), lambda j, i: (i, 0)),
                  pl.BlockSpec((D_MODEL, tn), lambda j, i: (0, j)),
                  pl.BlockSpec((D_MODEL, tn), lambda j, i: (0, j + nt)),
                  wd_spec],
        out_specs=[pl.BlockSpec((tm, tn), lambda j, i: (i, j)), wd_spec],
        out_shape=[jax.ShapeDtypeStruct((N_ALL, D_FF), BF16),
                   jax.ShapeDtypeStruct((D_FF, D_MODEL), BF16)],
        scratch_shapes=[pltpu.VMEM((D_MODEL, tn), BF16), pltpu.VMEM((D_MODEL, tn), BF16)],
        compiler_params=_params(2),
        name="ffn_up",
    )(yn, w_gate_up, w_gate_up, w_down)


def _ffn_down_kernel(a_ref, w_ref, r_ref, yp_ref, ys_ref):
    i = pl.program_id(0)
    n_prompt_tiles = N_P // FFN_DOWN_ROWS
    y = r_ref[...] + jnp.dot(a_ref[...], w_ref[...], preferred_element_type=F32)

    @pl.when(i < n_prompt_tiles)
    def _():
        yp_ref[...] = y

    @pl.when(i >= n_prompt_tiles)
    def _():
        ys_ref[...] = y


def _ffn_down(act, w_down_bf, y1):
    tm = FFN_DOWN_ROWS
    n_p = N_P // tm
    return pl.pallas_call(
        _ffn_down_kernel,
        grid=(N_ALL // tm,),
        in_specs=[pl.BlockSpec((tm, D_FF), lambda i: (i, 0)),
                  pl.BlockSpec((D_FF, D_MODEL), lambda i: (0, 0), pipeline_mode=pl.Buffered(1)),
                  pl.BlockSpec((tm, D_MODEL), lambda i: (i, 0))],
        out_specs=[pl.BlockSpec((tm, D_MODEL), lambda i: (jnp.minimum(i, n_p - 1), 0)),
                   pl.BlockSpec((tm, D_MODEL), lambda i: (jnp.maximum(i - n_p, 0), 0))],
        out_shape=[jax.ShapeDtypeStruct((N_P, D_MODEL), F32),
                   jax.ShapeDtypeStruct((N_S, D_MODEL), F32)],
        compiler_params=_params(1),
        name="ffn_down",
    )(act, w_down_bf, y1)


def _rope_tables():
    inv = ROPE_THETA ** (-jnp.arange(0, HEAD_DIM, 2, dtype=F32) / HEAD_DIM)
    inv_t = jnp.tile(inv, LANES // (HEAD_DIM // 2))[None, :]
    sign = jnp.asarray(np.tile(np.repeat([-1.0, 1.0], HEAD_DIM // 2), LANES // HEAD_DIM), F32)[None, :]
    ang_a = (jnp.arange(N_P // LANES, dtype=jnp.int32) * LANES).astype(F32)[:, None] * inv_t
    ang_b = jnp.arange(LANES, dtype=jnp.int32).astype(F32)[:, None] * inv_t
    ca, sa = jnp.cos(ang_a)[:, None, :], jnp.sin(ang_a)[:, None, :]
    cb, sb = jnp.cos(ang_b)[None, :, :], jnp.sin(ang_b)[None, :, :]
    cos_p = (ca * cb - sa * sb).reshape(N_P, LANES)
    sin_p = (sa * cb + ca * sb).reshape(N_P, LANES)
    ang_s = (PAST_LEN + jnp.arange(DEC_SEQ, dtype=jnp.int32)).astype(F32)[:, None] * inv_t
    cos_s = jnp.tile(jnp.cos(ang_s), (DEC_BATCH, 1))
    sin_s = jnp.tile(jnp.sin(ang_s), (DEC_BATCH, 1))
    return (jnp.concatenate([cos_p, cos_s], axis=0),
            jnp.concatenate([sin_p, sin_s], axis=0) * sign)


def kernel(x_prompt, x_sample, cache_k_win, cache_v_win, state_conv, norm_mix, w_in, q_norm, k_norm,
           sinks, conv_w, w_out, norm_ffn, w_gate_up, w_down):
    assert x_prompt.shape == (1, SEQ, D_MODEL) and x_sample.shape == (DEC_BATCH, DEC_SEQ, D_MODEL)
    assert w_in.shape == (1, D_MODEL, D_IN)
    assert cache_k_win.shape == (1, DEC_BATCH, WINDOW, N_KV_HEADS, HEAD_DIM)

    x_p = x_prompt.reshape(N_P, D_MODEL)
    x_s = x_sample.reshape(N_S, D_MODEL)
    w_in2 = w_in.reshape(D_MODEL, D_IN)
    w_out2 = w_out.reshape(D_ATTN, D_MODEL)
    w_gu2 = w_gate_up.reshape(D_MODEL, 2 * D_FF)
    w_dn2 = w_down.reshape(D_FF, D_MODEL)
    conv_w2 = conv_w.reshape(CONV_WIDTH, D_CONV)
    cache_kt = jnp.transpose(cache_k_win.reshape(DEC_BATCH, WINDOW, D_KV), (0, 2, 1))
    cache_vt = jnp.transpose(cache_v_win.reshape(DEC_BATCH, WINDOW, D_KV), (0, 2, 1))

    cos_t, sin_t = _rope_tables()
    q_g128 = jnp.tile(q_norm.reshape(1, HEAD_DIM) * Q_SCALE, (1, LANES // HEAD_DIM))
    k_g128 = jnp.tile(k_norm.reshape(1, HEAD_DIM), (1, LANES // HEAD_DIM))
    rest_scale = jnp.asarray(np.repeat([1.0, 1.0, 1.0, 0.5, 0.5], D_MODEL)[None, :], F32)
    head_of_lane = np.arange(LANES) // HEAD_DIM
    bd = jnp.asarray(np.where(head_of_lane[:, None] == head_of_lane[None, :], 1.0 / HEAD_DIM, 0.0),
                     dtype=BF16)
    sinks2 = sinks.reshape(1, N_HEADS).astype(F32)
    sink_col = jnp.broadcast_to(jnp.tile(sinks2.reshape(N_HEADS) * LOG2E, DEC_SEQ)[:, None],
                                (DEC_SEQ * N_HEADS, LANES))
    rep = jnp.asarray(np.tile(np.eye(HEAD_DIM), (1, N_KV_HEADS)), dtype=BF16)
    st_rows = jnp.pad(state_conv.reshape(DEC_BATCH, CONV_WIDTH - 1, D_CONV),
                      ((0, 0), (0, DEC_SEQ - (CONV_WIDTH - 1)), (0, 0))).reshape(N_S, D_CONV)

    xn, kv = _norm_proj_kv(x_p, x_s, norm_mix.reshape(1, D_MODEL), w_in2, k_g128, cos_t, sin_t, bd)
    q, rest, w_out_bf = _proj_q_rest(xn, w_in2, q_g128, cos_t, sin_t, bd, rest_scale, w_out2)

    mix_p, tail = _prompt_mix(sinks2, q, kv, rest, conv_w2)
    q_s = q[N_P:].reshape(DEC_BATCH, DEC_SEQ * N_HEADS, HEAD_DIM)
    attn_s, kwt, vwt = _sample_attn(q_s, kv, cache_kt, cache_vt, sink_col, rep)
    mix_s, u_s = _sample_merge(attn_s.reshape(N_S, D_ATTN), rest, st_rows, conv_w2)

    y1, yn = _out_proj(mix_p, mix_s, w_out_bf, x_p, x_s, norm_ffn.reshape(1, D_MODEL))
    act, w_down_bf = _ffn_up(yn, w_gu2, w_dn2)
    y_p, y_s = _ffn_down(act, w_down_bf, y1)

    kv_tail = kv[N_P - WINDOW:N_P]
    win_shape = (1, DEC_BATCH, WINDOW, N_KV_HEADS, HEAD_DIM)
    return (y_p.reshape(1, SEQ, D_MODEL),
            y_s.reshape(DEC_BATCH, DEC_SEQ, D_MODEL),
            kv_tail[:, :D_KV].reshape(1, 1, WINDOW, N_KV_HEADS, HEAD_DIM),
            kv_tail[:, D_KV:].reshape(1, 1, WINDOW, N_KV_HEADS, HEAD_DIM),
            tail[8 - (CONV_WIDTH - 1):].reshape(1, 1, CONV_WIDTH - 1, D_CONV),
            jnp.transpose(kwt, (0, 2, 1)).reshape(win_shape),
            jnp.transpose(vwt, (0, 2, 1)).reshape(win_shape),
            u_s.reshape(DEC_BATCH, DEC_SEQ, D_CONV)[:, DEC_SEQ - (CONV_WIDTH - 1):][None])
```

```python
import math

import numpy as np
import jax
import jax.numpy as jnp
from jax import lax
from jax.experimental import pallas as pl
from jax.experimental.pallas import tpu as pltpu

F32 = jnp.float32
BF16 = jnp.bfloat16

D_MODEL = 2048
SEQ = 8192
DEC_BATCH = 128
DEC_SEQ = 4
PAST_LEN = 8192
N_HEADS = 32
HEAD_DIM = 64
N_KV_HEADS = 4
GROUP = N_HEADS // N_KV_HEADS
D_ATTN = N_HEADS * HEAD_DIM
D_KV = N_KV_HEADS * HEAD_DIM
WINDOW = 128
D_CONV = D_MODEL
CONV_WIDTH = 3
D_FF = 5632
ROPE_THETA = 10000.0
EPS = 1e-6
NEG_INF = -1e30
D_IN = 2 * D_ATTN + 2 * D_KV + 3 * D_CONV + D_CONV
D_REST = 5 * D_MODEL
LOG2E = math.log2(math.e)
Q_SCALE = HEAD_DIM ** -0.5 * LOG2E

N_P = SEQ
N_S = DEC_BATCH * DEC_SEQ
N_ALL = N_P + N_S

LANES = 128
VMEM_LIMIT = 56 * 1024 * 1024

ROW_TILE_SMALL = 512
ROW_TILE_BIG = 1088
ROW_TILE_HUGE = 2176
MXU_WIDTH = 256
FFN_DOWN_ROWS = 256
MIX_BLOCKS = 4
MERGE_CHUNK = (32, 256)
SEQ_BLOCK = 8
PAIRS_PER_GROUP = GROUP // 2


def _params(n_axes):
    return pltpu.CompilerParams(
        dimension_semantics=("arbitrary",) * n_axes, vmem_limit_bytes=VMEM_LIMIT)


def _rmsnorm_rows(x, g):
    ms = jnp.mean(x * x, axis=-1, keepdims=True)
    return x * lax.rsqrt(ms + EPS) * g


def _cast_weight_once(w_ref, wbf_ref, axis):
    @pl.when(pl.program_id(axis) == 0)
    def _():
        wbf_ref[...] = w_ref[...].astype(BF16)


def _headnorm_rope(zc, g128, cos_t, sin_t, bd):
    ms = jnp.dot((zc * zc).astype(BF16), bd, preferred_element_type=F32)
    y = zc * lax.rsqrt(ms + EPS) * g128
    lane = lax.broadcasted_iota(jnp.int32, y.shape, 1)
    first_half = (lane & (HEAD_DIM // 2)) == 0
    partner = jnp.where(first_half,
                        pltpu.roll(y, LANES - HEAD_DIM // 2, 1),
                        pltpu.roll(y, HEAD_DIM // 2, 1))
    return y * cos_t + partner * sin_t


def _norm_proj_kv_kernel(xp_ref, xs_ref, gm_ref, w_ref, g_ref, cos_ref, sin_ref, bd_ref,
                         xn_ref, kv_ref, wbf_ref):
    _cast_weight_once(w_ref, wbf_ref, 0)
    is_prompt = pl.program_id(0) < N_P // ROW_TILE_SMALL
    x = jnp.where(is_prompt, xp_ref[...], xs_ref[...])
    xn = _rmsnorm_rows(x, gm_ref[...]).astype(BF16)
    xn_ref[...] = xn
    z = jnp.dot(xn, wbf_ref[...], preferred_element_type=F32)
    g128 = g_ref[...]
    cos_t = cos_ref[...]
    sin_t = sin_ref[...]
    bd = bd_ref[...]
    for c in range(D_KV // LANES):
        zc = z[:, c * LANES:(c + 1) * LANES]
        kv_ref[:, c * LANES:(c + 1) * LANES] = _headnorm_rope(zc, g128, cos_t, sin_t, bd)
    kv_ref[:, D_KV:] = z[:, D_KV:]


def _norm_proj_kv(x_p, x_s, g_mix, w_in, k_g128, cos_t, sin_t, bd):
    tm = ROW_TILE_SMALL
    tn = 2 * D_KV
    last_p = N_P // tm - 1
    const = lambda shape: pl.BlockSpec(shape, lambda i: (0, 0))
    return pl.pallas_call(
        _norm_proj_kv_kernel,
        grid=(N_ALL // tm,),
        in_specs=[pl.BlockSpec((tm, D_MODEL), lambda i: (jnp.minimum(i, last_p), 0)),
                  const((tm, D_MODEL)),
                  const((1, D_MODEL)),
                  pl.BlockSpec((D_MODEL, tn), lambda i: (0, D_ATTN // tn)),
                  const((1, LANES)),
                  pl.BlockSpec((tm, LANES), lambda i: (i, 0)),
                  pl.BlockSpec((tm, LANES), lambda i: (i, 0)),
                  const((LANES, LANES))],
        out_specs=[pl.BlockSpec((tm, D_MODEL), lambda i: (i, 0)),
                   pl.BlockSpec((tm, tn), lambda i: (i, 0))],
        out_shape=[jax.ShapeDtypeStruct((N_ALL, D_MODEL), BF16),
                   jax.ShapeDtypeStruct((N_ALL, tn), F32)],
        scratch_shapes=[pltpu.VMEM((D_MODEL, tn), BF16)],
        compiler_params=_params(1),
        name="norm_proj_kv",
    )(x_p, x_s, g_mix, w_in, k_g128, cos_t, sin_t, bd)


def _proj_rest_kernel(x_ref, w_ref, scale_ref, o_ref, wbf_ref):
    _cast_weight_once(w_ref, wbf_ref, 1)
    for c0 in range(0, o_ref.shape[1], MXU_WIDTH):
        cols = slice(c0, c0 + MXU_WIDTH)
        z = jnp.dot(x_ref[...], wbf_ref[:, cols], preferred_element_type=F32)
        o_ref[:, cols] = (z * scale_ref[:, cols]).astype(o_ref.dtype)


def _proj_q_kernel(x_ref, w_ref, g_ref, cos_ref, sin_ref, bd_ref, wo_ref, o_ref, wo_bf_ref, wbf_ref):
    _cast_weight_once(w_ref, wbf_ref, 1)
    wo_bf_ref[...] = wo_ref[...].astype(BF16)
    z = jnp.dot(x_ref[...], wbf_ref[...], preferred_element_type=F32)
    g128 = g_ref[...]
    cos_t = cos_ref[...]
    sin_t = sin_ref[...]
    bd = bd_ref[...]
    for c in range(z.shape[1] // LANES):
        zc = z[:, c * LANES:(c + 1) * LANES]
        out = _headnorm_rope(zc, g128, cos_t, sin_t, bd)
        o_ref[:, c * LANES:(c + 1) * LANES] = out.astype(o_ref.dtype)


def _proj_q_rest(xn, w_in, q_g128, cos_q, sin_q, bd, rest_scale, w_out):
    tm = ROW_TILE_BIG
    n_row = N_ALL // tm
    k = D_MODEL
    x_spec = pl.BlockSpec((tm, k), lambda j, i: (i, 0))
    rope_specs = [pl.BlockSpec((1, LANES), lambda j, i: (0, 0)),
                  pl.BlockSpec((tm, LANES), lambda j, i: (i, 0)),
                  pl.BlockSpec((tm, LANES), lambda j, i: (i, 0)),
                  pl.BlockSpec((LANES, LANES), lambda j, i: (0, 0))]

    tn_q = 1024
    n_col_q = D_ATTN // tn_q
    wo_rows = D_ATTN // (n_col_q * n_row)
    wo_spec = pl.BlockSpec((wo_rows, D_MODEL), lambda j, i: (j * n_row + i, 0))
    q, w_out_bf = pl.pallas_call(
        _proj_q_kernel,
        grid=(n_col_q, n_row),
        in_specs=[x_spec, pl.BlockSpec((k, tn_q), lambda j, i: (0, j))] + rope_specs + [wo_spec],
        out_specs=[pl.BlockSpec((tm, tn_q), lambda j, i: (i, j)), wo_spec],
        out_shape=[jax.ShapeDtypeStruct((N_ALL, D_ATTN), BF16),
                   jax.ShapeDtypeStruct((D_ATTN, D_MODEL), BF16)],
        scratch_shapes=[pltpu.VMEM((k, tn_q), BF16)],
        compiler_params=_params(2),
        name="proj_q",
    )(xn, w_in, q_g128, cos_q, sin_q, bd, w_out)

    tm_r = ROW_TILE_HUGE
    tn_r = 1024
    rest_start = D_ATTN + 2 * D_KV
    rest = pl.pallas_call(
        _proj_rest_kernel,
        grid=(D_REST // tn_r, N_ALL // tm_r),
        in_specs=[pl.BlockSpec((tm_r, k), lambda j, i: (i, 0)),
                  pl.BlockSpec((pl.Element(k), pl.Element(tn_r)),
                               lambda j, i: (0, pl.multiple_of(rest_start + j * tn_r, LANES))),
                  pl.BlockSpec((1, tn_r), lambda j, i: (0, j))],
        out_specs=pl.BlockSpec((tm_r, tn_r), lambda j, i: (i, j)),
        out_shape=jax.ShapeDtypeStruct((N_ALL, D_REST), BF16),
        scratch_shapes=[pltpu.VMEM((k, tn_r), BF16)],
        compiler_params=_params(2),
        name="proj_rest",
    )(xn, w_in, rest_scale)
    return q, rest, w_out_bf


def _gated_merge(attn_half, u, u1, u2, b_ref, ga_half_ref, gc_half_ref, cw_ref):
    return _gated_merge_values(attn_half, u, u1, u2, b_ref[...].astype(F32),
                               ga_half_ref[...], gc_half_ref[...], 0.5 * cw_ref[...])


def _gated_merge_values(attn_half, u, u1, u2, b, ga_half, gc_half, cw_half):
    conv_half = cw_half[0:1, :] * u2 + cw_half[1:2, :] * u1 + cw_half[2:3, :] * u
    bc = (b * conv_half).astype(BF16)
    a = attn_half.astype(BF16)
    ta = jnp.tanh(ga_half)
    tc = jnp.tanh(gc_half)
    return (ta * a + a) + (tc * bc + bc)


def _prompt_attention_block(j, sinks_ref, q_ref, kv_prev, kv_cur, attn_ref, row0):
    n_keys = 2 * WINDOW
    q_rows = slice(row0, row0 + WINDOW)
    kv = jnp.concatenate([kv_prev, kv_cur], axis=0)
    key_row = lax.broadcasted_iota(jnp.int32, (n_keys, LANES), 0)
    low = lax.broadcasted_iota(jnp.int32, (n_keys, LANES), 1) < HEAD_DIM
    low_v = low & (key_row > 0)
    high_v = jnp.logical_not(low) & (key_row > 0)
    ones_low = jnp.where(low, 2.0, 0.0)
    ones_high = jnp.where(low, 0.0, 2.0)

    r = lax.broadcasted_iota(jnp.int32, (WINDOW, n_keys), 0)
    c = lax.broadcasted_iota(jnp.int32, (WINDOW, n_keys), 1)
    first_key = jnp.where(j > 0, 0, WINDOW)
    mask = jnp.where((c > r) & (c <= r + WINDOW) & (c >= first_key), 0.0, NEG_INF).astype(BF16)
    mask2 = jnp.concatenate([mask, mask], axis=1)
    eye = (lax.broadcasted_iota(jnp.int32, (WINDOW, WINDOW), 0)
           == lax.broadcasted_iota(jnp.int32, (WINDOW, WINDOW), 1))
    eye = jnp.where(eye, 1.0, 0.0).astype(BF16)
    sink_lane = lax.broadcasted_iota(jnp.int32, (WINDOW, LANES), 1) == 0
    no_keys = jnp.zeros((HEAD_DIM, n_keys), F32)

    for g in range(N_KV_HEADS):
        chunk = (g // 2) * LANES
        kc_t = kv[:, chunk:chunk + LANES].T
        kg_t = kc_t[(g % 2) * HEAD_DIM:(g % 2 + 1) * HEAD_DIM]
        vc = kv[:, D_KV + chunk:D_KV + chunk + LANES]
        if g % 2 == 0:
            v_low = jnp.where(low_v, vc, 0.0)
            v_high = pltpu.roll(v_low, HEAD_DIM, 1)
        else:
            v_high = jnp.where(high_v, vc, 0.0)
            v_low = pltpu.roll(v_high, HEAD_DIM, 1)
        k_bd_t = jnp.concatenate(
            [jnp.concatenate([kg_t, no_keys], axis=1),
             jnp.concatenate([no_keys, kg_t], axis=1)], axis=0).astype(BF16)
        k_aug = jnp.concatenate([k_bd_t, mask2], axis=0)
        v_bd = jnp.concatenate(
            [jnp.concatenate([v_low, ones_low], axis=1),
             jnp.concatenate([v_high, ones_high], axis=1)], axis=0).astype(BF16)

        for k in range(PAIRS_PER_GROUP):
            pair = g * PAIRS_PER_GROUP + k
            cols = slice(pair * LANES, (pair + 1) * LANES)
            q_aug = jnp.concatenate([q_ref[q_rows, cols], eye], axis=1)
            s = jnp.dot(q_aug, k_aug, preferred_element_type=F32)
            halves = []
            for half in range(2):
                sh = s[:, half * n_keys:(half + 1) * n_keys]
                sink = sinks_ref[0, 2 * pair + half] * LOG2E
                sh = jnp.concatenate([jnp.where(sink_lane, sink, sh[:, :LANES]), sh[:, LANES:]], axis=1)
                halves.append(jnp.exp2(sh - jnp.max(sh, axis=1, keepdims=True)))
            p = jnp.concatenate(halves, axis=1).astype(BF16)
            o = jnp.dot(p, v_bd, preferred_element_type=F32)
            attn_ref[q_rows, cols] = (o[:, :LANES] / o[:, LANES:]).astype(attn_ref.dtype)


def _prompt_mix_kernel(sinks_ref, q_ref, kvp_ref, kvc_ref, h_ref, b_ref, c_ref, ga_ref, gc_ref,
                       cw_ref, mix_ref, tail_ref, u_scr, attn_scr):
    j = pl.program_id(0)

    @pl.when(j == 0)
    def _():
        u_scr[0:8, :] = jnp.zeros((8, D_CONV), F32)

    tm = MIX_BLOCKS * WINDOW
    for sb in range(MIX_BLOCKS):
        kv_prev = kvp_ref[...] if sb == 0 else kvc_ref[(sb - 1) * WINDOW:sb * WINDOW, :]
        _prompt_attention_block(j * MIX_BLOCKS + sb, sinks_ref, q_ref, kv_prev,
                                kvc_ref[sb * WINDOW:(sb + 1) * WINDOW, :], attn_scr, sb * WINDOW)

    rb, cb = MERGE_CHUNK
    cw_half = 0.5 * cw_ref[...]
    for r0 in range(0, tm, rb):
        rows = slice(r0, r0 + rb)
        u_scr[8 + r0:8 + r0 + rb, :] = c_ref[rows, :].astype(F32) * h_ref[rows, :].astype(F32)
        for c0 in range(0, D_MODEL, cb):
            cols = slice(c0, c0 + cb)
            mix = _gated_merge_values(
                attn_scr[rows, cols],
                u_scr[8 + r0:8 + r0 + rb, cols], u_scr[7 + r0:7 + r0 + rb, cols],
                u_scr[6 + r0:6 + r0 + rb, cols], b_ref[rows, cols].astype(F32),
                ga_ref[rows, cols], gc_ref[rows, cols], cw_half[:, cols])
            mix_ref[rows, cols] = mix.astype(mix_ref.dtype)
    tail = u_scr[tm:tm + 8, :]
    u_scr[0:8, :] = tail
    tail_ref[...] = tail


def _prompt_mix(sinks, q, kv, rest, conv_w):
    tm = MIX_BLOCKS * WINDOW
    blk = lambda c: pl.BlockSpec((tm, D_MODEL), lambda j, c=c: (j, c))
    return pl.pallas_call(
        _prompt_mix_kernel,
        grid=(N_P // tm,),
        in_specs=[pl.BlockSpec(memory_space=pltpu.SMEM),
                  pl.BlockSpec((tm, D_ATTN), lambda j: (j, 0)),
                  pl.BlockSpec((WINDOW, 2 * D_KV), lambda j: (jnp.maximum(j * MIX_BLOCKS - 1, 0), 0)),
                  pl.BlockSpec((tm, 2 * D_KV), lambda j: (j, 0)),
                  blk(0), blk(1), blk(2), blk(3), blk(4),
                  pl.BlockSpec((CONV_WIDTH, D_CONV), lambda j: (0, 0))],
        out_specs=[pl.BlockSpec((tm, D_MODEL), lambda j: (j, 0)),
                   pl.BlockSpec((8, D_CONV), lambda j: (0, 0))],
        out_shape=[jax.ShapeDtypeStruct((N_P, D_MODEL), BF16),
                   jax.ShapeDtypeStruct((8, D_CONV), F32)],
        scratch_shapes=[pltpu.VMEM((8 + tm, D_CONV), F32), pltpu.VMEM((tm, D_ATTN), F32)],
        compiler_params=_params(1),
        name="prompt_mix",
    )(sinks, q, kv, kv, rest, rest, rest, rest, rest, conv_w)


def _sample_attn_kernel(q_ref, kvn_ref, ckt_ref, cvt_ref, sink_ref, rep_ref,
                        o_ref, kwt_ref, vwt_ref, bias_c, bias_n):
    n_new = SEQ_BLOCK * DEC_SEQ
    rows = DEC_SEQ * N_HEADS
    keep = WINDOW - DEC_SEQ

    @pl.when(pl.program_id(0) == 0)
    def _():
        row = lax.broadcasted_iota(jnp.int32, (rows, WINDOW), 0)
        col = lax.broadcasted_iota(jnp.int32, (rows, WINDOW), 1)
        t = row >> 5
        bias_c[...] = jnp.where(col > t, 0.0, NEG_INF)
        for s in range(SEQ_BLOCK):
            valid_new = (col < n_new) & ((col >> 2) == s) & ((col & 3) <= t)
            bias_n[s] = jnp.where(valid_new, 0.0, NEG_INF)

    kvn = jnp.concatenate([kvn_ref[...], jnp.zeros((WINDOW - n_new, 2 * D_KV), F32)], axis=0)
    kn_t = kvn[:, :D_KV].T
    vn_t = kvn[:, D_KV:].T
    kn_t_bf = kn_t.astype(BF16)
    vn_bf = kvn[:, D_KV:].astype(BF16)

    all_rows = SEQ_BLOCK * rows
    row2 = lax.broadcasted_iota(jnp.int32, (all_rows, D_KV), 0)
    col2 = lax.broadcasted_iota(jnp.int32, (all_rows, D_KV), 1)
    same_group = ((row2 & (N_HEADS - 1)) >> 3) == (col2 >> 6)
    win_lane = lax.broadcasted_iota(jnp.int32, (D_KV, WINDOW), 1)
    sink_b = jnp.concatenate([sink_ref[...]] * SEQ_BLOCK, axis=0)
    ones = jnp.ones((2 * WINDOW, LANES), BF16)

    q_all = q_ref[...].reshape(all_rows, HEAD_DIM)
    q4 = jnp.dot(q_all, rep_ref[...], preferred_element_type=F32)
    q4 = jnp.where(same_group, q4, 0.0).astype(BF16)
    s_n = jnp.dot(q4, kn_t_bf, preferred_element_type=F32) + bias_n[...].reshape(all_rows, WINDOW)
    s_c = jnp.concatenate(
        [jnp.dot(q4[s * rows:(s + 1) * rows], ckt_ref[s].astype(BF16), preferred_element_type=F32)
         + bias_c[...] for s in range(SEQ_BLOCK)], axis=0)
    m_b = jnp.maximum(jnp.max(jnp.maximum(s_c, s_n), axis=1, keepdims=True), sink_b)
    p_c = jnp.exp2(s_c - m_b).astype(BF16)
    p_n = jnp.exp2(s_n - m_b).astype(BF16)
    l_b = (jnp.dot(jnp.concatenate([p_c, p_n], axis=1), ones, preferred_element_type=F32)
           + jnp.exp2(sink_b - m_b))
    o4 = jnp.concatenate(
        [lax.dot_general(p_c[s * rows:(s + 1) * rows], cvt_ref[s].astype(BF16),
                         (((1,), (1,)), ((), ())), preferred_element_type=F32)
         for s in range(SEQ_BLOCK)], axis=0) + jnp.dot(p_n, vn_bf, preferred_element_type=F32)
    o4 = jnp.where(same_group, o4, 0.0)
    a = o4[:, 0:LANES] + o4[:, LANES:]
    o = (a + pltpu.roll(a, HEAD_DIM, 1)) / (2.0 * l_b)
    o_ref[...] = o[:, 0:HEAD_DIM].reshape(SEQ_BLOCK, rows, HEAD_DIM)

    for s in range(SEQ_BLOCK):
        k_t = ckt_ref[s]
        v_t = cvt_ref[s]
        new_shift = keep - s * DEC_SEQ
        kwt_ref[s] = jnp.where(win_lane >= keep, pltpu.roll(kn_t, new_shift, 1),
                               pltpu.roll(k_t, keep, 1))
        vwt_ref[s] = jnp.where(win_lane >= keep, pltpu.roll(vn_t, new_shift, 1),
                               pltpu.roll(v_t, keep, 1))


def _sample_attn(q_s, kv, cache_kt, cache_vt, sink_col, rep):
    rows = DEC_SEQ * N_HEADS
    n_new = SEQ_BLOCK * DEC_SEQ
    cache_spec = pl.BlockSpec((SEQ_BLOCK, D_KV, WINDOW), lambda i: (i, 0, 0))
    return pl.pallas_call(
        _sample_attn_kernel,
        grid=(DEC_BATCH // SEQ_BLOCK,),
        in_specs=[pl.BlockSpec((SEQ_BLOCK, rows, HEAD_DIM), lambda i: (i, 0, 0)),
                  pl.BlockSpec((n_new, 2 * D_KV), lambda i: (i + N_P // n_new, 0)),
                  cache_spec, cache_spec,
                  pl.BlockSpec((rows, LANES), lambda i: (0, 0)),
                  pl.BlockSpec((HEAD_DIM, D_KV), lambda i: (0, 0))],
        out_specs=[pl.BlockSpec((SEQ_BLOCK, rows, HEAD_DIM), lambda i: (i, 0, 0)),
                   cache_spec, cache_spec],
        out_shape=[jax.ShapeDtypeStruct((DEC_BATCH, rows, HEAD_DIM), F32),
                   jax.ShapeDtypeStruct((DEC_BATCH, D_KV, WINDOW), F32),
                   jax.ShapeDtypeStruct((DEC_BATCH, D_KV, WINDOW), F32)],
        scratch_shapes=[pltpu.VMEM((rows, WINDOW), F32), pltpu.VMEM((SEQ_BLOCK, rows, WINDOW), F32)],
        compiler_params=_params(1),
        name="sample_attn",
    )(q_s, kv, cache_kt, cache_vt, sink_col, rep)


def _sample_merge_kernel(attn_ref, h_ref, b_ref, c_ref, ga_ref, gc_ref, st_ref, cw_ref,
                         mix_ref, u_ref):
    u = c_ref[...].astype(F32) * h_ref[...].astype(F32)
    u_ref[...] = u
    rows = u.shape[0]
    t = lax.broadcasted_iota(jnp.int32, u.shape, 0) & (DEC_SEQ - 1)
    st = st_ref[...]
    u1 = jnp.where(t == 0, pltpu.roll(st, rows - 1, 0), pltpu.roll(u, 1, 0))
    u2 = jnp.where(t < 2, st, pltpu.roll(u, 2, 0))
    mix = _gated_merge(attn_ref[...], u, u1, u2, b_ref, ga_ref, gc_ref, cw_ref)
    mix_ref[...] = mix.astype(mix_ref.dtype)


def _sample_merge(attn_s, rest, st_rows, conv_w):
    tm = WINDOW
    off = N_P // tm
    blk = lambda c: pl.BlockSpec((tm, D_MODEL), lambda i, c=c: (i + off, c))
    return pl.pallas_call(
        _sample_merge_kernel,
        grid=(N_S // tm,),
        in_specs=[pl.BlockSpec((tm, D_ATTN), lambda i: (i, 0)),
                  blk(0), blk(1), blk(2), blk(3), blk(4),
                  pl.BlockSpec((tm, D_CONV), lambda i: (i, 0)),
                  pl.BlockSpec((CONV_WIDTH, D_CONV), lambda i: (0, 0))],
        out_specs=[pl.BlockSpec((tm, D_MODEL), lambda i: (i, 0)),
                   pl.BlockSpec((tm, D_CONV), lambda i: (i, 0))],
        out_shape=[jax.ShapeDtypeStruct((N_S, D_MODEL), BF16),
                   jax.ShapeDtypeStruct((N_S, D_CONV), F32)],
        compiler_params=_params(1),
        name="sample_merge",
    )(attn_s, rest, rest, rest, rest, rest, st_rows, conv_w)


def _out_proj_kernel(mp_ref, ms_ref, w_ref, xp_ref, xs_ref, gf_ref, y_ref, yn_ref):
    is_prompt = pl.program_id(0) < N_P // ROW_TILE_SMALL
    res = jnp.where(is_prompt, xp_ref[...], xs_ref[...])
    mix = jnp.where(is_prompt, mp_ref[...], ms_ref[...])
    y = res + jnp.dot(mix, w_ref[...], preferred_element_type=F32)
    y_ref[...] = y
    yn_ref[...] = _rmsnorm_rows(y, gf_ref[...]).astype(yn_ref.dtype)


def _out_proj(mix_p, mix_s, w_out_bf, x_p, x_s, g_ffn):
    tm = ROW_TILE_SMALL
    last_p = N_P // tm - 1
    once = pl.Buffered(1)
    return pl.pallas_call(
        _out_proj_kernel,
        grid=(N_ALL // tm,),
        in_specs=[pl.BlockSpec((tm, D_ATTN), lambda i: (jnp.minimum(i, last_p), 0)),
                  pl.BlockSpec((tm, D_ATTN), lambda i: (0, 0), pipeline_mode=once),
                  pl.BlockSpec((D_ATTN, D_MODEL), lambda i: (0, 0), pipeline_mode=once),
                  pl.BlockSpec((tm, D_MODEL), lambda i: (jnp.minimum(i, last_p), 0)),
                  pl.BlockSpec((tm, D_MODEL), lambda i: (0, 0), pipeline_mode=once),
                  pl.BlockSpec((1, D_MODEL), lambda i: (0, 0))],
        out_specs=[pl.BlockSpec((tm, D_MODEL), lambda i: (i, 0)),
                   pl.BlockSpec((tm, D_MODEL), lambda i: (i, 0))],
        out_shape=[jax.ShapeDtypeStruct((N_ALL, D_MODEL), F32),
                   jax.ShapeDtypeStruct((N_ALL, D_MODEL), BF16)],
        compiler_params=_params(1),
        name="out_proj",
    )(mix_p, mix_s, w_out_bf, x_p, x_s, g_ffn)


def _ffn_up_kernel(x_ref, wg_ref, wu_ref, wd_ref, o_ref, wd_bf_ref, wg_bf, wu_bf):
    @pl.when(pl.program_id(1) == 0)
    def _():
        wg_bf[...] = wg_ref[...].astype(BF16)
        wu_bf[...] = wu_ref[...].astype(BF16)

    wd_bf_ref[...] = wd_ref[...].astype(BF16)

    for c0 in range(0, o_ref.shape[1], MXU_WIDTH):
        cols = slice(c0, c0 + MXU_WIDTH)
        g = jnp.dot(x_ref[...], wg_bf[:, cols], preferred_element_type=F32)
        u = jnp.dot(x_ref[...], wu_bf[:, cols], preferred_element_type=F32)
        o_ref[:, cols] = ((g * jax.nn.sigmoid(g)) * u).astype(o_ref.dtype)


def _ffn_up(yn, w_gate_up, w_down):
    tm = ROW_TILE_HUGE
    tn = 512
    nt = D_FF // tn
    n_row = N_ALL // tm
    wd_rows = D_FF // (nt * n_row)
    wd_spec = pl.BlockSpec((wd_rows, D_MODEL), lambda j, i: (j * n_row + i, 0))
    return pl.pallas_call(
        _ffn_up_kernel,
        grid=(nt, n_row),
        in_specs=[pl.BlockSpec((tm, D_MODEL), lambda j, i: (i, 0)),
                  pl.BlockSpec((D_MODEL, tn), lambda j, i: (0, j)),
                  pl.BlockSpec((D_MODEL, tn), lambda j, i: (0, j + nt)),
                  wd_spec],
        out_specs=[pl.BlockSpec((tm, tn), lambda j, i: (i, j)), wd_spec],
        out_shape=[jax.ShapeDtypeStruct((N_ALL, D_FF), BF16),
                   jax.ShapeDtypeStruct((D_FF, D_MODEL), BF16)],
        scratch_shapes=[pltpu.VMEM((D_MODEL, tn), BF16), pltpu.VMEM((D_MODEL, tn), BF16)],
        compiler_params=_params(2),
        name="ffn_up",
    )(yn, w_gate_up, w_gate_up, w_down)


def _ffn_down_kernel(a_ref, w_ref, r_ref, yp_ref, ys_ref):
    i = pl.program_id(0)
    n_prompt_tiles = N_P // FFN_DOWN_ROWS
    y = r_ref[...] + jnp.dot(a_ref[...], w_ref[...], preferred_element_type=F32)

    @pl.when(i < n_prompt_tiles)
    def _():
        yp_ref[...] = y

    @pl.when(i >= n_prompt_tiles)
    def _():
        ys_ref[...] = y


def _ffn_down(act, w_down_bf, y1):
    tm = FFN_DOWN_ROWS
    n_p = N_P // tm
    return pl.pallas_call(
        _ffn_down_kernel,
        grid=(N_ALL // tm,),
        in_specs=[pl.BlockSpec((tm, D_FF), lambda i: (i, 0)),
                  pl.BlockSpec((D_FF, D_MODEL), lambda i: (0, 0), pipeline_mode=pl.Buffered(1)),
                  pl.BlockSpec((tm, D_MODEL), lambda i: (i, 0))],
        out_specs=[pl.BlockSpec((tm, D_MODEL), lambda i: (jnp.minimum(i, n_p - 1), 0)),
                   pl.BlockSpec((tm, D_MODEL), lambda i: (jnp.maximum(i - n_p, 0), 0))],
        out_shape=[jax.ShapeDtypeStruct((N_P, D_MODEL), F32),
                   jax.ShapeDtypeStruct((N_S, D_MODEL), F32)],
        compiler_params=_params(1),
        name="ffn_down",
    )(act, w_down_bf, y1)


def _rope_tables():
    inv = ROPE_THETA ** (-jnp.arange(0, HEAD_DIM, 2, dtype=F32) / HEAD_DIM)
    inv_t = jnp.tile(inv, LANES // (HEAD_DIM // 2))[None, :]
    sign = jnp.asarray(np.tile(np.repeat([-1.0, 1.0], HEAD_DIM // 2), LANES // HEAD_DIM), F32)[None, :]
    ang_a = (jnp.arange(N_P // LANES, dtype=jnp.int32) * LANES).astype(F32)[:, None] * inv_t
    ang_b = jnp.arange(LANES, dtype=jnp.int32).astype(F32)[:, None] * inv_t
    ca, sa = jnp.cos(ang_a)[:, None, :], jnp.sin(ang_a)[:, None, :]
    cb, sb = jnp.cos(ang_b)[None, :, :], jnp.sin(ang_b)[None, :, :]
    cos_p = (ca * cb - sa * sb).reshape(N_P, LANES)
    sin_p = (sa * cb + ca * sb).reshape(N_P, LANES)
    ang_s = (PAST_LEN + jnp.arange(DEC_SEQ, dtype=jnp.int32)).astype(F32)[:, None] * inv_t
    cos_s = jnp.tile(jnp.cos(ang_s), (DEC_BATCH, 1))
    sin_s = jnp.tile(jnp.sin(ang_s), (DEC_BATCH, 1))
    return (jnp.concatenate([cos_p, cos_s], axis=0),
            jnp.concatenate([sin_p, sin_s], axis=0) * sign)


def kernel(x_prompt, x_sample, cache_k_win, cache_v_win, state_conv, norm_mix, w_in, q_norm, k_norm,
           sinks, conv_w, w_out, norm_ffn, w_gate_up, w_down):
    assert x_prompt.shape == (1, SEQ, D_MODEL) and x_sample.shape == (DEC_BATCH, DEC_SEQ, D_MODEL)
    assert w_in.shape == (1, D_MODEL, D_IN)
    assert cache_k_win.shape == (1, DEC_BATCH, WINDOW, N_KV_HEADS, HEAD_DIM)

    x_p = x_prompt.reshape(N_P, D_MODEL)
    x_s = x_sample.reshape(N_S, D_MODEL)
    w_in2 = w_in.reshape(D_MODEL, D_IN)
    w_out2 = w_out.reshape(D_ATTN, D_MODEL)
    w_gu2 = w_gate_up.reshape(D_MODEL, 2 * D_FF)
    w_dn2 = w_down.reshape(D_FF, D_MODEL)
    conv_w2 = conv_w.reshape(CONV_WIDTH, D_CONV)
    cache_kt = jnp.transpose(cache_k_win.reshape(DEC_BATCH, WINDOW, D_KV), (0, 2, 1))
    cache_vt = jnp.transpose(cache_v_win.reshape(DEC_BATCH, WINDOW, D_KV), (0, 2, 1))

    cos_t, sin_t = _rope_tables()
    q_g128 = jnp.tile(q_norm.reshape(1, HEAD_DIM) * Q_SCALE, (1, LANES // HEAD_DIM))
    k_g128 = jnp.tile(k_norm.reshape(1, HEAD_DIM), (1, LANES // HEAD_DIM))
    rest_scale = jnp.asarray(np.repeat([1.0, 1.0, 1.0, 0.5, 0.5], D_MODEL)[None, :], F32)
    head_of_lane = np.arange(LANES) // HEAD_DIM
    bd = jnp.asarray(np.where(head_of_lane[:, None] == head_of_lane[None, :], 1.0 / HEAD_DIM, 0.0),
                     dtype=BF16)
    sinks2 = sinks.reshape(1, N_HEADS).astype(F32)
    sink_col = jnp.broadcast_to(jnp.tile(sinks2.reshape(N_HEADS) * LOG2E, DEC_SEQ)[:, None],
                                (DEC_SEQ * N_HEADS, LANES))
    rep = jnp.asarray(np.tile(np.eye(HEAD_DIM), (1, N_KV_HEADS)), dtype=BF16)
    st_rows = jnp.pad(state_conv.reshape(DEC_BATCH, CONV_WIDTH - 1, D_CONV),
                      ((0, 0), (0, DEC_SEQ - (CONV_WIDTH - 1)), (0, 0))).reshape(N_S, D_CONV)

    xn, kv = _norm_proj_kv(x_p, x_s, norm_mix.reshape(1, D_MODEL), w_in2, k_g128, cos_t, sin_t, bd)
    q, rest, w_out_bf = _proj_q_rest(xn, w_in2, q_g128, cos_t, sin_t, bd, rest_scale, w_out2)

    mix_p, tail = _prompt_mix(sinks2, q, kv, rest, conv_w2)
    q_s = q[N_P:].reshape(DEC_BATCH, DEC_SEQ * N_HEADS, HEAD_DIM)
    attn_s, kwt, vwt = _sample_attn(q_s, kv, cache_kt, cache_vt, sink_col, rep)
    mix_s, u_s = _sample_merge(attn_s.reshape(N_S, D_ATTN), rest, st_rows, conv_w2)

    y1, yn = _out_proj(mix_p, mix_s, w_out_bf, x_p, x_s, norm_ffn.reshape(1, D_MODEL))
    act, w_down_bf = _ffn_up(yn, w_gu2, w_dn2)
    y_p, y_s = _ffn_down(act, w_down_bf, y1)

    kv_tail = kv[N_P - WINDOW:N_P]
    win_shape = (1, DEC_BATCH, WINDOW, N_KV_HEADS, HEAD_DIM)
    return (y_p.reshape(1, SEQ, D_MODEL),
            y_s.reshape(DEC_BATCH, DEC_SEQ, D_MODEL),
            kv_tail[:, :D_KV].reshape(1, 1, WINDOW, N_KV_HEADS, HEAD_DIM),
            kv_tail[:, D_KV:].reshape(1, 1, WINDOW, N_KV_HEADS, HEAD_DIM),
            tail[8 - (CONV_WIDTH - 1):].reshape(1, 1, CONV_WIDTH - 1, D_CONV),
            jnp.transpose(kwt, (0, 2, 1)).reshape(win_shape),
            jnp.transpose(vwt, (0, 2, 1)).reshape(win_shape),
            u_s.reshape(DEC_BATCH, DEC_SEQ, D_CONV)[:, DEC_SEQ - (CONV_WIDTH - 1):][None])
```

```python
import math

import numpy as np
import jax
import jax.numpy as jnp
from jax import lax
from jax.experimental import pallas as pl
from jax.experimental.pallas import tpu as pltpu

F32 = jnp.float32
BF16 = jnp.bfloat16

D_MODEL = 2048
SEQ = 8192
DEC_BATCH = 128
DEC_SEQ = 4
PAST_LEN = 8192
N_HEADS = 32
HEAD_DIM = 64
N_KV_HEADS = 4
GROUP = N_HEADS // N_KV_HEADS
D_ATTN = N_HEADS * HEAD_DIM
D_KV = N_KV_HEADS * HEAD_DIM
WINDOW = 128
D_CONV = D_MODEL
CONV_WIDTH = 3
D_FF = 5632
ROPE_THETA = 10000.0
EPS = 1e-6
NEG_INF = -1e30
D_IN = 2 * D_ATTN + 2 * D_KV + 3 * D_CONV + D_CONV
D_REST = 5 * D_MODEL
LOG2E = math.log2(math.e)
Q_SCALE = HEAD_DIM ** -0.5 * LOG2E

N_P = SEQ
N_S = DEC_BATCH * DEC_SEQ
N_ALL = N_P + N_S

LANES = 128
VMEM_LIMIT = 56 * 1024 * 1024

ROW_TILE_SMALL = 512
ROW_TILE_BIG = 1088
ROW_TILE_HUGE = 2176
MXU_WIDTH = 256
PROJ_Q_ROW_PIECES = 4
FFN_UP_ROW_PIECES = 4
FFN_DOWN_ROWS = 256
FFN_DOWN_CHUNK = 512
MIX_BLOCKS = 4
MERGE_CHUNK = (32, 256)
SEQ_BLOCK = 8
PAIRS_PER_GROUP = GROUP // 2


def _params(n_axes):
    return pltpu.CompilerParams(
        dimension_semantics=("arbitrary",) * n_axes, vmem_limit_bytes=VMEM_LIMIT)


def _rmsnorm_rows(x, g):
    ms = jnp.mean(x * x, axis=-1, keepdims=True)
    return x * lax.rsqrt(ms + EPS) * g


def _cast_weight_once(w_ref, wbf_ref, axis):
    @pl.when(pl.program_id(axis) == 0)
    def _():
        wbf_ref[...] = w_ref[...].astype(BF16)


def _headnorm_rope(z, g128, cos_t, sin_t, bd):
    ms = jnp.dot((z * z).astype(BF16), bd, preferred_element_type=F32)
    scale = lax.rsqrt(ms + EPS)
    lane = lax.broadcasted_iota(jnp.int32, (z.shape[0], LANES), 1)
    first_half = (lane & (HEAD_DIM // 2)) == 0
    halves = []
    for c in range(0, MXU_WIDTH, LANES):
        y = z[:, c:c + LANES] * scale[:, c:c + LANES] * g128
        partner = jnp.where(first_half,
                            pltpu.roll(y, LANES - HEAD_DIM // 2, 1),
                            pltpu.roll(y, HEAD_DIM // 2, 1))
        halves.append(y * cos_t + partner * sin_t)
    return halves


def _norm_proj_kv_kernel(xp_ref, xs_ref, gm_ref, w_ref, g_ref, cos_ref, sin_ref, bd_ref,
                         xn_ref, kv_ref, wbf_ref):
    _cast_weight_once(w_ref, wbf_ref, 0)
    is_prompt = pl.program_id(0) < N_P // ROW_TILE_SMALL
    g128 = g_ref[...]
    bd = bd_ref[...]
    piece = xn_ref.shape[0] // PROJ_Q_ROW_PIECES
    for r0 in range(0, xn_ref.shape[0], piece):
        rows = slice(r0, r0 + piece)
        x = jnp.where(is_prompt, xp_ref[rows, :], xs_ref[rows, :])
        xn = _rmsnorm_rows(x, gm_ref[...]).astype(BF16)
        xn_ref[rows, :] = xn
        z = jnp.dot(xn, wbf_ref[...], preferred_element_type=F32)
        k_low, k_high = _headnorm_rope(z[:, :D_KV], g128, cos_ref[rows, :], sin_ref[rows, :], bd)
        kv_ref[rows, 0:LANES] = k_low
        kv_ref[rows, LANES:D_KV] = k_high
        kv_ref[rows, D_KV:] = z[:, D_KV:]


def _norm_proj_kv(x_p, x_s, g_mix, w_in, k_g128, cos_t, sin_t, bd):
    tm = ROW_TILE_SMALL
    tn = 2 * D_KV
    last_p = N_P // tm - 1
    const = lambda shape: pl.BlockSpec(shape, lambda i: (0, 0))
    return pl.pallas_call(
        _norm_proj_kv_kernel,
        grid=(N_ALL // tm,),
        in_specs=[pl.BlockSpec((tm, D_MODEL), lambda i: (jnp.minimum(i, last_p), 0)),
                  const((tm, D_MODEL)),
                  const((1, D_MODEL)),
                  pl.BlockSpec((D_MODEL, tn), lambda i: (0, D_ATTN // tn)),
                  const((1, LANES)),
                  pl.BlockSpec((tm, LANES), lambda i: (i, 0)),
                  pl.BlockSpec((tm, LANES), lambda i: (i, 0)),
                  const((MXU_WIDTH, MXU_WIDTH))],
        out_specs=[pl.BlockSpec((tm, D_MODEL), lambda i: (i, 0)),
                   pl.BlockSpec((tm, tn), lambda i: (i, 0))],
        out_shape=[jax.ShapeDtypeStruct((N_ALL, D_MODEL), BF16),
                   jax.ShapeDtypeStruct((N_ALL, tn), F32)],
        scratch_shapes=[pltpu.VMEM((D_MODEL, tn), BF16)],
        compiler_params=_params(1),
        name="norm_proj_kv",
    )(x_p, x_s, g_mix, w_in, k_g128, cos_t, sin_t, bd)


def _proj_rest_kernel(x_ref, w_ref, scale_ref, o_ref, wbf_ref):
    _cast_weight_once(w_ref, wbf_ref, 1)
    for c0 in range(0, o_ref.shape[1], MXU_WIDTH):
        cols = slice(c0, c0 + MXU_WIDTH)
        z = jnp.dot(x_ref[...], wbf_ref[:, cols], preferred_element_type=F32)
        o_ref[:, cols] = (z * scale_ref[:, cols]).astype(o_ref.dtype)


def _proj_q_kernel(x_ref, w_ref, g_ref, cos_ref, sin_ref, bd_ref, wo_ref, o_ref, wo_bf_ref, wbf_ref):
    _cast_weight_once(w_ref, wbf_ref, 1)
    wo_bf_ref[...] = wo_ref[...].astype(BF16)
    g128 = g_ref[...]
    bd = bd_ref[...]
    piece = o_ref.shape[0] // PROJ_Q_ROW_PIECES
    for r0 in range(0, o_ref.shape[0], piece):
        rows = slice(r0, r0 + piece)
        z_all = jnp.dot(x_ref[rows, :], wbf_ref[...], preferred_element_type=F32)
        cos_t = cos_ref[rows, :]
        sin_t = sin_ref[rows, :]
        for c0 in range(0, o_ref.shape[1], MXU_WIDTH):
            low, high = _headnorm_rope(z_all[:, c0:c0 + MXU_WIDTH], g128, cos_t, sin_t, bd)
            o_ref[rows, c0:c0 + LANES] = low.astype(o_ref.dtype)
            o_ref[rows, c0 + LANES:c0 + MXU_WIDTH] = high.astype(o_ref.dtype)


def _proj_q_rest(xn, w_in, q_g128, cos_q, sin_q, bd, rest_scale, w_out):
    tm = ROW_TILE_BIG
    n_row = N_ALL // tm
    k = D_MODEL
    x_spec = pl.BlockSpec((tm, k), lambda j, i: (i, 0))
    rope_specs = [pl.BlockSpec((1, LANES), lambda j, i: (0, 0)),
                  pl.BlockSpec((tm, LANES), lambda j, i: (i, 0)),
                  pl.BlockSpec((tm, LANES), lambda j, i: (i, 0)),
                  pl.BlockSpec((MXU_WIDTH, MXU_WIDTH), lambda j, i: (0, 0))]

    tn_q = 1024
    n_col_q = D_ATTN // tn_q
    wo_rows = D_ATTN // (n_col_q * n_row)
    wo_spec = pl.BlockSpec((wo_rows, D_MODEL), lambda j, i: (j * n_row + i, 0))
    q, w_out_bf = pl.pallas_call(
        _proj_q_kernel,
        grid=(n_col_q, n_row),
        in_specs=[x_spec, pl.BlockSpec((k, tn_q), lambda j, i: (0, j))] + rope_specs + [wo_spec],
        out_specs=[pl.BlockSpec((tm, tn_q), lambda j, i: (i, j)), wo_spec],
        out_shape=[jax.ShapeDtypeStruct((N_ALL, D_ATTN), BF16),
                   jax.ShapeDtypeStruct((D_ATTN, D_MODEL), BF16)],
        scratch_shapes=[pltpu.VMEM((k, tn_q), BF16)],
        compiler_params=_params(2),
        name="proj_q",
    )(xn, w_in, q_g128, cos_q, sin_q, bd, w_out)

    tm_r = ROW_TILE_HUGE
    tn_r = 1024
    rest_start = D_ATTN + 2 * D_KV
    rest = pl.pallas_call(
        _proj_rest_kernel,
        grid=(D_REST // tn_r, N_ALL // tm_r),
        in_specs=[pl.BlockSpec((tm_r, k), lambda j, i: (i, 0)),
                  pl.BlockSpec((pl.Element(k), pl.Element(tn_r)),
                               lambda j, i: (0, pl.multiple_of(rest_start + j * tn_r, LANES))),
                  pl.BlockSpec((1, tn_r), lambda j, i: (0, j))],
        out_specs=pl.BlockSpec((tm_r, tn_r), lambda j, i: (i, j)),
        out_shape=jax.ShapeDtypeStruct((N_ALL, D_REST), BF16),
        scratch_shapes=[pltpu.VMEM((k, tn_r), BF16)],
        compiler_params=_params(2),
        name="proj_rest",
    )(xn, w_in, rest_scale)
    return q, rest, w_out_bf


def _gated_merge(attn_half, u, u1, u2, b_ref, ga_half_ref, gc_half_ref, cw_ref):
    return _gated_merge_values(attn_half, u, u1, u2, b_ref[...].astype(F32),
                               ga_half_ref[...], gc_half_ref[...], 0.5 * cw_ref[...])


def _gated_merge_values(attn_half, u, u1, u2, b, ga_half, gc_half, cw_half):
    conv_half = cw_half[0:1, :] * u2 + cw_half[1:2, :] * u1 + cw_half[2:3, :] * u
    bc = (b * conv_half).astype(BF16)
    a = attn_half.astype(BF16)
    ta = jnp.tanh(ga_half)
    tc = jnp.tanh(gc_half)
    return (ta * a + a) + (tc * bc + bc)


def _prompt_attention_block(j, sinks_ref, q_ref, kv_prev, kv_cur, attn_ref, row0):
    n_keys = 2 * WINDOW
    q_rows = slice(row0, row0 + WINDOW)
    kv = jnp.concatenate([kv_prev, kv_cur], axis=0)
    key_row = lax.broadcasted_iota(jnp.int32, (n_keys, LANES), 0)
    low = lax.broadcasted_iota(jnp.int32, (n_keys, LANES), 1) < HEAD_DIM
    low_v = low & (key_row > 0)
    high_v = jnp.logical_not(low) & (key_row > 0)
    ones_low = jnp.where(low, 2.0, 0.0)
    ones_high = jnp.where(low, 0.0, 2.0)

    r = lax.broadcasted_iota(jnp.int32, (WINDOW, n_keys), 0)
    c = lax.broadcasted_iota(jnp.int32, (WINDOW, n_keys), 1)
    first_key = jnp.where(j > 0, 0, WINDOW)
    mask = jnp.where((c > r) & (c <= r + WINDOW) & (c >= first_key), 0.0, NEG_INF).astype(BF16)
    mask2 = jnp.concatenate([mask, mask], axis=1)
    eye = (lax.broadcasted_iota(jnp.int32, (WINDOW, WINDOW), 0)
           == lax.broadcasted_iota(jnp.int32, (WINDOW, WINDOW), 1))
    eye = jnp.where(eye, 1.0, 0.0).astype(BF16)
    sink_lane = lax.broadcasted_iota(jnp.int32, (WINDOW, LANES), 1) == 0
    no_keys = jnp.zeros((HEAD_DIM, n_keys), F32)

    for g in range(N_KV_HEADS):
        chunk = (g // 2) * LANES
        kc_t = kv[:, chunk:chunk + LANES].T
        kg_t = kc_t[(g % 2) * HEAD_DIM:(g % 2 + 1) * HEAD_DIM]
        vc = kv[:, D_KV + chunk:D_KV + chunk + LANES]
        if g % 2 == 0:
            v_low = jnp.where(low_v, vc, 0.0)
            v_high = pltpu.roll(v_low, HEAD_DIM, 1)
        else:
            v_high = jnp.where(high_v, vc, 0.0)
            v_low = pltpu.roll(v_high, HEAD_DIM, 1)
        k_bd_t = jnp.concatenate(
            [jnp.concatenate([kg_t, no_keys], axis=1),
             jnp.concatenate([no_keys, kg_t], axis=1)], axis=0).astype(BF16)
        k_aug = jnp.concatenate([k_bd_t, mask2], axis=0)
        v_bd = jnp.concatenate(
            [jnp.concatenate([v_low, ones_low], axis=1),
             jnp.concatenate([v_high, ones_high], axis=1)], axis=0).astype(BF16)

        for k in range(PAIRS_PER_GROUP):
            pair = g * PAIRS_PER_GROUP + k
            cols = slice(pair * LANES, (pair + 1) * LANES)
            q_aug = jnp.concatenate([q_ref[q_rows, cols], eye], axis=1)
            s = jnp.dot(q_aug, k_aug, preferred_element_type=F32)
            halves = []
            for half in range(2):
                sh = s[:, half * n_keys:(half + 1) * n_keys]
                sink = sinks_ref[0, 2 * pair + half] * LOG2E
                sh = jnp.concatenate([jnp.where(sink_lane, sink, sh[:, :LANES]), sh[:, LANES:]], axis=1)
                halves.append(jnp.exp2(sh - jnp.max(sh, axis=1, keepdims=True)))
            p = jnp.concatenate(halves, axis=1).astype(BF16)
            o = jnp.dot(p, v_bd, preferred_element_type=F32)
            attn_ref[q_rows, cols] = (o[:, :LANES] / o[:, LANES:]).astype(attn_ref.dtype)


def _prompt_mix_kernel(sinks_ref, q_ref, kvp_ref, kvc_ref, h_ref, b_ref, c_ref, ga_ref, gc_ref,
                       cw_ref, mix_ref, tail_ref, u_scr, attn_scr):
    j = pl.program_id(0)

    @pl.when(j == 0)
    def _():
        u_scr[0:8, :] = jnp.zeros((8, D_CONV), F32)

    tm = MIX_BLOCKS * WINDOW
    for sb in range(MIX_BLOCKS):
        kv_prev = kvp_ref[...] if sb == 0 else kvc_ref[(sb - 1) * WINDOW:sb * WINDOW, :]
        _prompt_attention_block(j * MIX_BLOCKS + sb, sinks_ref, q_ref, kv_prev,
                                kvc_ref[sb * WINDOW:(sb + 1) * WINDOW, :], attn_scr, sb * WINDOW)

    rb, cb = MERGE_CHUNK
    cw_half = 0.5 * cw_ref[...]
    for r0 in range(0, tm, rb):
        rows = slice(r0, r0 + rb)
        u_scr[8 + r0:8 + r0 + rb, :] = c_ref[rows, :].astype(F32) * h_ref[rows, :].astype(F32)
        for c0 in range(0, D_MODEL, cb):
            cols = slice(c0, c0 + cb)
            mix = _gated_merge_values(
                attn_scr[rows, cols],
                u_scr[8 + r0:8 + r0 + rb, cols], u_scr[7 + r0:7 + r0 + rb, cols],
                u_scr[6 + r0:6 + r0 + rb, cols], b_ref[rows, cols].astype(F32),
                ga_ref[rows, cols], gc_ref[rows, cols], cw_half[:, cols])
            mix_ref[rows, cols] = mix.astype(mix_ref.dtype)
    tail = u_scr[tm:tm + 8, :]
    u_scr[0:8, :] = tail
    tail_ref[...] = tail


def _prompt_mix(sinks, q, kv, rest, conv_w):
    tm = MIX_BLOCKS * WINDOW
    blk = lambda c: pl.BlockSpec((tm, D_MODEL), lambda j, c=c: (j, c))
    return pl.pallas_call(
        _prompt_mix_kernel,
        grid=(N_P // tm,),
        in_specs=[pl.BlockSpec(memory_space=pltpu.SMEM),
                  pl.BlockSpec((tm, D_ATTN), lambda j: (j, 0)),
                  pl.BlockSpec((WINDOW, 2 * D_KV), lambda j: (jnp.maximum(j * MIX_BLOCKS - 1, 0), 0)),
                  pl.BlockSpec((tm, 2 * D_KV), lambda j: (j, 0)),
                  blk(0), blk(1), blk(2), blk(3), blk(4),
                  pl.BlockSpec((CONV_WIDTH, D_CONV), lambda j: (0, 0))],
        out_specs=[pl.BlockSpec((tm, D_MODEL), lambda j: (j, 0)),
                   pl.BlockSpec((8, D_CONV), lambda j: (0, 0))],
        out_shape=[jax.ShapeDtypeStruct((N_P, D_MODEL), BF16),
                   jax.ShapeDtypeStruct((8, D_CONV), F32)],
        scratch_shapes=[pltpu.VMEM((8 + tm, D_CONV), F32), pltpu.VMEM((tm, D_ATTN), F32)],
        compiler_params=_params(1),
        name="prompt_mix",
    )(sinks, q, kv, kv, rest, rest, rest, rest, rest, conv_w)


def _sample_attn_kernel(q_ref, kvn_ref, ckt_ref, cvt_ref, sink_ref, rep_ref,
                        o_ref, kwt_ref, vwt_ref, bias_c, bias_n):
    n_new = SEQ_BLOCK * DEC_SEQ
    rows = DEC_SEQ * N_HEADS
    keep = WINDOW - DEC_SEQ

    @pl.when(pl.program_id(0) == 0)
    def _():
        row = lax.broadcasted_iota(jnp.int32, (rows, WINDOW), 0)
        col = lax.broadcasted_iota(jnp.int32, (rows, WINDOW), 1)
        t = row >> 5
        bias_c[...] = jnp.where(col > t, 0.0, NEG_INF)
        for s in range(SEQ_BLOCK):
            valid_new = (col < n_new) & ((col >> 2) == s) & ((col & 3) <= t)
            bias_n[s] = jnp.where(valid_new, 0.0, NEG_INF)

    kvn = jnp.concatenate([kvn_ref[...], jnp.zeros((WINDOW - n_new, 2 * D_KV), F32)], axis=0)
    kn_t = kvn[:, :D_KV].T
    vn_t = kvn[:, D_KV:].T
    kn_t_bf = kn_t.astype(BF16)
    vn_bf = kvn[:, D_KV:].astype(BF16)

    all_rows = SEQ_BLOCK * rows
    row2 = lax.broadcasted_iota(jnp.int32, (all_rows, D_KV), 0)
    col2 = lax.broadcasted_iota(jnp.int32, (all_rows, D_KV), 1)
    same_group = ((row2 & (N_HEADS - 1)) >> 3) == (col2 >> 6)
    win_lane = lax.broadcasted_iota(jnp.int32, (D_KV, WINDOW), 1)
    sink_b = jnp.concatenate([sink_ref[...]] * SEQ_BLOCK, axis=0)
    ones = jnp.ones((2 * WINDOW, LANES), BF16)

    q_all = q_ref[...].reshape(all_rows, HEAD_DIM)
    q4 = jnp.dot(q_all, rep_ref[...], preferred_element_type=F32)
    q4 = jnp.where(same_group, q4, 0.0).astype(BF16)
    s_n = jnp.dot(q4, kn_t_bf, preferred_element_type=F32) + bias_n[...].reshape(all_rows, WINDOW)
    s_c = jnp.concatenate(
        [jnp.dot(q4[s * rows:(s + 1) * rows], ckt_ref[s].astype(BF16), preferred_element_type=F32)
         + bias_c[...] for s in range(SEQ_BLOCK)], axis=0)
    m_b = jnp.maximum(jnp.max(jnp.maximum(s_c, s_n), axis=1, keepdims=True), sink_b)
    p_c = jnp.exp2(s_c - m_b).astype(BF16)
    p_n = jnp.exp2(s_n - m_b).astype(BF16)
    l_b = (jnp.dot(jnp.concatenate([p_c, p_n], axis=1), ones, preferred_element_type=F32)
           + jnp.exp2(sink_b - m_b))
    o4 = jnp.concatenate(
        [lax.dot_general(p_c[s * rows:(s + 1) * rows], cvt_ref[s].astype(BF16),
                         (((1,), (1,)), ((), ())), preferred_element_type=F32)
         for s in range(SEQ_BLOCK)], axis=0) + jnp.dot(p_n, vn_bf, preferred_element_type=F32)
    o4 = jnp.where(same_group, o4, 0.0)
    a = o4[:, 0:LANES] + o4[:, LANES:]
    o = (a + pltpu.roll(a, HEAD_DIM, 1)) / (2.0 * l_b)
    o_ref[...] = o[:, 0:HEAD_DIM].reshape(SEQ_BLOCK, rows, HEAD_DIM)

    for s in range(SEQ_BLOCK):
        k_t = ckt_ref[s]
        v_t = cvt_ref[s]
        new_shift = keep - s * DEC_SEQ
        kwt_ref[s] = jnp.where(win_lane >= keep, pltpu.roll(kn_t, new_shift, 1),
                               pltpu.roll(k_t, keep, 1))
        vwt_ref[s] = jnp.where(win_lane >= keep, pltpu.roll(vn_t, new_shift, 1),
                               pltpu.roll(v_t, keep, 1))


def _sample_attn(q_s, kv, cache_kt, cache_vt, sink_col, rep):
    rows = DEC_SEQ * N_HEADS
    n_new = SEQ_BLOCK * DEC_SEQ
    cache_spec = pl.BlockSpec((SEQ_BLOCK, D_KV, WINDOW), lambda i: (i, 0, 0))
    return pl.pallas_call(
        _sample_attn_kernel,
        grid=(DEC_BATCH // SEQ_BLOCK,),
        in_specs=[pl.BlockSpec((SEQ_BLOCK, rows, HEAD_DIM), lambda i: (i, 0, 0)),
                  pl.BlockSpec((n_new, 2 * D_KV), lambda i: (i + N_P // n_new, 0)),
                  cache_spec, cache_spec,
                  pl.BlockSpec((rows, LANES), lambda i: (0, 0)),
                  pl.BlockSpec((HEAD_DIM, D_KV), lambda i: (0, 0))],
        out_specs=[pl.BlockSpec((SEQ_BLOCK, rows, HEAD_DIM), lambda i: (i, 0, 0)),
                   cache_spec, cache_spec],
        out_shape=[jax.ShapeDtypeStruct((DEC_BATCH, rows, HEAD_DIM), F32),
                   jax.ShapeDtypeStruct((DEC_BATCH, D_KV, WINDOW), F32),
                   jax.ShapeDtypeStruct((DEC_BATCH, D_KV, WINDOW), F32)],
        scratch_shapes=[pltpu.VMEM((rows, WINDOW), F32), pltpu.VMEM((SEQ_BLOCK, rows, WINDOW), F32)],
        compiler_params=_params(1),
        name="sample_attn",
    )(q_s, kv, cache_kt, cache_vt, sink_col, rep)


def _sample_merge_kernel(attn_ref, h_ref, b_ref, c_ref, ga_ref, gc_ref, st_ref, cw_ref,
                         mix_ref, u_ref):
    u = c_ref[...].astype(F32) * h_ref[...].astype(F32)
    u_ref[...] = u
    rows = u.shape[0]
    t = lax.broadcasted_iota(jnp.int32, u.shape, 0) & (DEC_SEQ - 1)
    st = st_ref[...]
    u1 = jnp.where(t == 0, pltpu.roll(st, rows - 1, 0), pltpu.roll(u, 1, 0))
    u2 = jnp.where(t < 2, st, pltpu.roll(u, 2, 0))
    mix = _gated_merge(attn_ref[...], u, u1, u2, b_ref, ga_ref, gc_ref, cw_ref)
    mix_ref[...] = mix.astype(mix_ref.dtype)


def _sample_merge(attn_s, rest, st_rows, conv_w):
    tm = WINDOW
    off = N_P // tm
    blk = lambda c: pl.BlockSpec((tm, D_MODEL), lambda i, c=c: (i + off, c))
    return pl.pallas_call(
        _sample_merge_kernel,
        grid=(N_S // tm,),
        in_specs=[pl.BlockSpec((tm, D_ATTN), lambda i: (i, 0)),
                  blk(0), blk(1), blk(2), blk(3), blk(4),
                  pl.BlockSpec((tm, D_CONV), lambda i: (i, 0)),
                  pl.BlockSpec((CONV_WIDTH, D_CONV), lambda i: (0, 0))],
        out_specs=[pl.BlockSpec((tm, D_MODEL), lambda i: (i, 0)),
                   pl.BlockSpec((tm, D_CONV), lambda i: (i, 0))],
        out_shape=[jax.ShapeDtypeStruct((N_S, D_MODEL), BF16),
                   jax.ShapeDtypeStruct((N_S, D_CONV), F32)],
        compiler_params=_params(1),
        name="sample_merge",
    )(attn_s, rest, rest, rest, rest, rest, st_rows, conv_w)


def _out_proj_kernel(mp_ref, ms_ref, w_ref, xp_ref, xs_ref, gf_ref, y_ref, yn_ref):
    is_prompt = pl.program_id(0) < N_P // ROW_TILE_SMALL
    res = jnp.where(is_prompt, xp_ref[...], xs_ref[...])
    mix = jnp.where(is_prompt, mp_ref[...], ms_ref[...])
    y = res + jnp.dot(mix, w_ref[...], preferred_element_type=F32)
    y_ref[...] = y
    yn_ref[...] = _rmsnorm_rows(y, gf_ref[...]).astype(yn_ref.dtype)


def _out_proj(mix_p, mix_s, w_out_bf, x_p, x_s, g_ffn):
    tm = ROW_TILE_SMALL
    last_p = N_P // tm - 1
    once = pl.Buffered(1)
    return pl.pallas_call(
        _out_proj_kernel,
        grid=(N_ALL // tm,),
        in_specs=[pl.BlockSpec((tm, D_ATTN), lambda i: (jnp.minimum(i, last_p), 0)),
                  pl.BlockSpec((tm, D_ATTN), lambda i: (0, 0), pipeline_mode=once),
                  pl.BlockSpec((D_ATTN, D_MODEL), lambda i: (0, 0), pipeline_mode=once),
                  pl.BlockSpec((tm, D_MODEL), lambda i: (jnp.minimum(i, last_p), 0)),
                  pl.BlockSpec((tm, D_MODEL), lambda i: (0, 0), pipeline_mode=once),
                  pl.BlockSpec((1, D_MODEL), lambda i: (0, 0))],
        out_specs=[pl.BlockSpec((tm, D_MODEL), lambda i: (i, 0)),
                   pl.BlockSpec((tm, D_MODEL), lambda i: (i, 0))],
        out_shape=[jax.ShapeDtypeStruct((N_ALL, D_MODEL), F32),
                   jax.ShapeDtypeStruct((N_ALL, D_MODEL), BF16)],
        compiler_params=_params(1),
        name="out_proj",
    )(mix_p, mix_s, w_out_bf, x_p, x_s, g_ffn)


def _ffn_up_kernel(x_ref, wg_ref, wu_ref, wd_ref, o_ref, wd_bf_ref, wg_bf, wu_bf):
    @pl.when(pl.program_id(1) == 0)
    def _():
        wg_bf[...] = wg_ref[...].astype(BF16)
        wu_bf[...] = wu_ref[...].astype(BF16)

    wd_bf_ref[...] = wd_ref[...].astype(BF16)

    piece = o_ref.shape[0] // FFN_UP_ROW_PIECES
    for c0 in range(0, o_ref.shape[1], MXU_WIDTH):
        cols = slice(c0, c0 + MXU_WIDTH)
        for r0 in range(0, o_ref.shape[0], piece):
            rows = slice(r0, r0 + piece)
            g = jnp.dot(x_ref[rows, :], wg_bf[:, cols], preferred_element_type=F32)
            u = jnp.dot(x_ref[rows, :], wu_bf[:, cols], preferred_element_type=F32)
            o_ref[rows, cols] = ((g * jax.nn.sigmoid(g)) * u).astype(o_ref.dtype)


def _ffn_up(yn, w_gate_up, w_down):
    tm = ROW_TILE_HUGE
    tn = 512
    nt = D_FF // tn
    n_row = N_ALL // tm
    wd_rows = D_FF // (nt * n_row)
    wd_spec = pl.BlockSpec((wd_rows, D_MODEL), lambda j, i: (j * n_row + i, 0))
    return pl.pallas_call(
        _ffn_up_kernel,
        grid=(nt, n_row),
        in_specs=[pl.BlockSpec((tm, D_MODEL), lambda j, i: (i, 0)),
                  pl.BlockSpec((D_MODEL, tn), lambda j, i: (0, j)),
                  pl.BlockSpec((D_MODEL, tn), lambda j, i: (0, j + nt)),
                  wd_spec],
        out_specs=[pl.BlockSpec((tm, tn), lambda j, i: (i, j)), wd_spec],
        out_shape=[jax.ShapeDtypeStruct((N_ALL, D_FF), BF16),
                   jax.ShapeDtypeStruct((D_FF, D_MODEL), BF16)],
        scratch_shapes=[pltpu.VMEM((D_MODEL, tn), BF16), pltpu.VMEM((D_MODEL, tn), BF16)],
        compiler_params=_params(2),
        name="ffn_up",
    )(yn, w_gate_up, w_gate_up, w_down)


def _ffn_down_kernel(a_ref, w_ref, r_ref, yp_ref, ys_ref):
    i = pl.program_id(0)
    n_prompt_tiles = N_P // FFN_DOWN_ROWS
    def project_into(y_ref):
        for c0 in range(0, D_MODEL, FFN_DOWN_CHUNK):
            cols = slice(c0, c0 + FFN_DOWN_CHUNK)
            y_ref[:, cols] = r_ref[:, cols] + jnp.dot(a_ref[...], w_ref[:, cols],
                                                      preferred_element_type=F32)

    @pl.when(i < n_prompt_tiles)
    def _():
        project_into(yp_ref)

    @pl.when(i >= n_prompt_tiles)
    def _():
        project_into(ys_ref)


def _ffn_down(act, w_down_bf, y1):
    tm = FFN_DOWN_ROWS
    n_p = N_P // tm
    return pl.pallas_call(
        _ffn_down_kernel,
        grid=(N_ALL // tm,),
        in_specs=[pl.BlockSpec((tm, D_FF), lambda i: (i, 0)),
                  pl.BlockSpec((D_FF, D_MODEL), lambda i: (0, 0), pipeline_mode=pl.Buffered(1)),
                  pl.BlockSpec((tm, D_MODEL), lambda i: (i, 0))],
        out_specs=[pl.BlockSpec((tm, D_MODEL), lambda i: (jnp.minimum(i, n_p - 1), 0)),
                   pl.BlockSpec((tm, D_MODEL), lambda i: (jnp.maximum(i - n_p, 0), 0))],
        out_shape=[jax.ShapeDtypeStruct((N_P, D_MODEL), F32),
                   jax.ShapeDtypeStruct((N_S, D_MODEL), F32)],
        compiler_params=_params(1),
        name="ffn_down",
    )(act, w_down_bf, y1)


def _rope_tables():
    inv = ROPE_THETA ** (-jnp.arange(0, HEAD_DIM, 2, dtype=F32) / HEAD_DIM)
    inv_t = jnp.tile(inv, LANES // (HEAD_DIM // 2))[None, :]
    sign = jnp.asarray(np.tile(np.repeat([-1.0, 1.0], HEAD_DIM // 2), LANES // HEAD_DIM), F32)[None, :]
    ang_a = (jnp.arange(N_P // LANES, dtype=jnp.int32) * LANES).astype(F32)[:, None] * inv_t
    ang_b = jnp.arange(LANES, dtype=jnp.int32).astype(F32)[:, None] * inv_t
    ca, sa = jnp.cos(ang_a)[:, None, :], jnp.sin(ang_a)[:, None, :]
    cb, sb = jnp.cos(ang_b)[None, :, :], jnp.sin(ang_b)[None, :, :]
    cos_p = (ca * cb - sa * sb).reshape(N_P, LANES)
    sin_p = (sa * cb + ca * sb).reshape(N_P, LANES)
    ang_s = (PAST_LEN + jnp.arange(DEC_SEQ, dtype=jnp.int32)).astype(F32)[:, None] * inv_t
    cos_s = jnp.tile(jnp.cos(ang_s), (DEC_BATCH, 1))
    sin_s = jnp.tile(jnp.sin(ang_s), (DEC_BATCH, 1))
    return (jnp.concatenate([cos_p, cos_s], axis=0),
            jnp.concatenate([sin_p, sin_s], axis=0) * sign)


def kernel(x_prompt, x_sample, cache_k_win, cache_v_win, state_conv, norm_mix, w_in, q_norm, k_norm,
           sinks, conv_w, w_out, norm_ffn, w_gate_up, w_down):
    assert x_prompt.shape == (1, SEQ, D_MODEL) and x_sample.shape == (DEC_BATCH, DEC_SEQ, D_MODEL)
    assert w_in.shape == (1, D_MODEL, D_IN)
    assert cache_k_win.shape == (1, DEC_BATCH, WINDOW, N_KV_HEADS, HEAD_DIM)

    x_p = x_prompt.reshape(N_P, D_MODEL)
    x_s = x_sample.reshape(N_S, D_MODEL)
    w_in2 = w_in.reshape(D_MODEL, D_IN)
    w_out2 = w_out.reshape(D_ATTN, D_MODEL)
    w_gu2 = w_gate_up.reshape(D_MODEL, 2 * D_FF)
    w_dn2 = w_down.reshape(D_FF, D_MODEL)
    conv_w2 = conv_w.reshape(CONV_WIDTH, D_CONV)
    cache_kt = jnp.transpose(cache_k_win.reshape(DEC_BATCH, WINDOW, D_KV), (0, 2, 1))
    cache_vt = jnp.transpose(cache_v_win.reshape(DEC_BATCH, WINDOW, D_KV), (0, 2, 1))

    cos_t, sin_t = _rope_tables()
    q_g128 = jnp.tile(q_norm.reshape(1, HEAD_DIM) * Q_SCALE, (1, LANES // HEAD_DIM))
    k_g128 = jnp.tile(k_norm.reshape(1, HEAD_DIM), (1, LANES // HEAD_DIM))
    rest_scale = jnp.asarray(np.repeat([1.0, 1.0, 1.0, 0.5, 0.5], D_MODEL)[None, :], F32)
    head_of_lane = np.arange(MXU_WIDTH) // HEAD_DIM
    bd = jnp.asarray(np.where(head_of_lane[:, None] == head_of_lane[None, :], 1.0 / HEAD_DIM, 0.0),
                     dtype=BF16)
    sinks2 = sinks.reshape(1, N_HEADS).astype(F32)
    sink_col = jnp.broadcast_to(jnp.tile(sinks2.reshape(N_HEADS) * LOG2E, DEC_SEQ)[:, None],
                                (DEC_SEQ * N_HEADS, LANES))
    rep = jnp.asarray(np.tile(np.eye(HEAD_DIM), (1, N_KV_HEADS)), dtype=BF16)
    st_rows = jnp.pad(state_conv.reshape(DEC_BATCH, CONV_WIDTH - 1, D_CONV),
                      ((0, 0), (0, DEC_SEQ - (CONV_WIDTH - 1)), (0, 0))).reshape(N_S, D_CONV)

    xn, kv = _norm_proj_kv(x_p, x_s, norm_mix.reshape(1, D_MODEL), w_in2, k_g128, cos_t, sin_t, bd)
    q, rest, w_out_bf = _proj_q_rest(xn, w_in2, q_g128, cos_t, sin_t, bd, rest_scale, w_out2)

    mix_p, tail = _prompt_mix(sinks2, q, kv, rest, conv_w2)
    q_s = q[N_P:].reshape(DEC_BATCH, DEC_SEQ * N_HEADS, HEAD_DIM)
    attn_s, kwt, vwt = _sample_attn(q_s, kv, cache_kt, cache_vt, sink_col, rep)
    mix_s, u_s = _sample_merge(attn_s.reshape(N_S, D_ATTN), rest, st_rows, conv_w2)

    y1, yn = _out_proj(mix_p, mix_s, w_out_bf, x_p, x_s, norm_ffn.reshape(1, D_MODEL))
    act, w_down_bf = _ffn_up(yn, w_gu2, w_dn2)
    y_p, y_s = _ffn_down(act, w_down_bf, y1)

    kv_tail = kv[N_P - WINDOW:N_P]
    win_shape = (1, DEC_BATCH, WINDOW, N_KV_HEADS, HEAD_DIM)
    return (y_p.reshape(1, SEQ, D_MODEL),
            y_s.reshape(DEC_BATCH, DEC_SEQ, D_MODEL),
            kv_tail[:, :D_KV].reshape(1, 1, WINDOW, N_KV_HEADS, HEAD_DIM),
            kv_tail[:, D_KV:].reshape(1, 1, WINDOW, N_KV_HEADS, HEAD_DIM),
            tail[8 - (CONV_WIDTH - 1):].reshape(1, 1, CONV_WIDTH - 1, D_CONV),
            jnp.transpose(kwt, (0, 2, 1)).reshape(win_shape),
            jnp.transpose(vwt, (0, 2, 1)).reshape(win_shape),
            u_s.reshape(DEC_BATCH, DEC_SEQ, D_CONV)[:, DEC_SEQ - (CONV_WIDTH - 1):][None])
```

```python
import math

import numpy as np
import jax
import jax.numpy as jnp
from jax import lax
from jax.experimental import pallas as pl
from jax.experimental.pallas import tpu as pltpu

F32 = jnp.float32
BF16 = jnp.bfloat16

D_MODEL = 2048
SEQ = 8192
DEC_BATCH = 128
DEC_SEQ = 4
PAST_LEN = 8192
N_HEADS = 32
HEAD_DIM = 64
N_KV_HEADS = 4
GROUP = N_HEADS // N_KV_HEADS
D_ATTN = N_HEADS * HEAD_DIM
D_KV = N_KV_HEADS * HEAD_DIM
WINDOW = 128
D_CONV = D_MODEL
CONV_WIDTH = 3
D_FF = 5632
ROPE_THETA = 10000.0
EPS = 1e-6
NEG_INF = -1e30
D_IN = 2 * D_ATTN + 2 * D_KV + 3 * D_CONV + D_CONV
D_REST = 5 * D_MODEL
LOG2E = math.log2(math.e)
Q_SCALE = HEAD_DIM ** -0.5 * LOG2E

N_P = SEQ
N_S = DEC_BATCH * DEC_SEQ
N_ALL = N_P + N_S

LANES = 128
VMEM_LIMIT = 56 * 1024 * 1024

ROW_TILE_SMALL = 512
ROW_TILE_BIG = 1088
ROW_TILE_HUGE = 2176
MXU_WIDTH = 256
PROJ_Q_ROW_PIECES = 4
FFN_UP_ROW_PIECES = 4
FFN_DOWN_ROWS = 256
FFN_DOWN_CHUNK = 512
MIX_BLOCKS = 4
MERGE_CHUNK = (32, 256)
SEQ_BLOCK = 8
PAIRS_PER_GROUP = GROUP // 2


def _params(n_axes):
    return pltpu.CompilerParams(
        dimension_semantics=("arbitrary",) * n_axes, vmem_limit_bytes=VMEM_LIMIT)


def _rmsnorm_rows(x, g):
    ms = jnp.mean(x * x, axis=-1, keepdims=True)
    return x * lax.rsqrt(ms + EPS) * g


def _cast_weight_once(w_ref, wbf_ref, axis):
    @pl.when(pl.program_id(axis) == 0)
    def _():
        wbf_ref[...] = w_ref[...].astype(BF16)


def _headnorm_rope(z, g128, cos_t, sin_t, bd):
    ms = jnp.dot((z * z).astype(BF16), bd, preferred_element_type=F32)
    scale = lax.rsqrt(ms + EPS)
    lane = lax.broadcasted_iota(jnp.int32, (z.shape[0], LANES), 1)
    first_half = (lane & (HEAD_DIM // 2)) == 0
    halves = []
    for c in range(0, MXU_WIDTH, LANES):
        y = z[:, c:c + LANES] * scale[:, c:c + LANES] * g128
        partner = jnp.where(first_half,
                            pltpu.roll(y, LANES - HEAD_DIM // 2, 1),
                            pltpu.roll(y, HEAD_DIM // 2, 1))
        halves.append(y * cos_t + partner * sin_t)
    return halves


def _norm_proj_kv_kernel(xp_ref, xs_ref, gm_ref, w_ref, g_ref, cos_ref, sin_ref, bd_ref,
                         xn_ref, kv_ref, wbf_ref):
    _cast_weight_once(w_ref, wbf_ref, 0)
    is_prompt = pl.program_id(0) < N_P // ROW_TILE_SMALL
    g128 = g_ref[...]
    bd = bd_ref[...]
    piece = xn_ref.shape[0] // PROJ_Q_ROW_PIECES
    for r0 in range(0, xn_ref.shape[0], piece):
        rows = slice(r0, r0 + piece)
        x = jnp.where(is_prompt, xp_ref[rows, :], xs_ref[rows, :])
        xn = _rmsnorm_rows(x, gm_ref[...]).astype(BF16)
        xn_ref[rows, :] = xn
        z = jnp.dot(xn, wbf_ref[...], preferred_element_type=F32)
        k_low, k_high = _headnorm_rope(z[:, :D_KV], g128, cos_ref[rows, :], sin_ref[rows, :], bd)
        kv_ref[rows, 0:LANES] = k_low
        kv_ref[rows, LANES:D_KV] = k_high
        kv_ref[rows, D_KV:] = z[:, D_KV:]


def _norm_proj_kv(x_p, x_s, g_mix, w_in, k_g128, cos_t, sin_t, bd):
    tm = ROW_TILE_SMALL
    tn = 2 * D_KV
    last_p = N_P // tm - 1
    const = lambda shape: pl.BlockSpec(shape, lambda i: (0, 0))
    return pl.pallas_call(
        _norm_proj_kv_kernel,
        grid=(N_ALL // tm,),
        in_specs=[pl.BlockSpec((tm, D_MODEL), lambda i: (jnp.minimum(i, last_p), 0)),
                  const((tm, D_MODEL)),
                  const((1, D_MODEL)),
                  pl.BlockSpec((D_MODEL, tn), lambda i: (0, D_ATTN // tn)),
                  const((1, LANES)),
                  pl.BlockSpec((tm, LANES), lambda i: (i, 0)),
                  pl.BlockSpec((tm, LANES), lambda i: (i, 0)),
                  const((MXU_WIDTH, MXU_WIDTH))],
        out_specs=[pl.BlockSpec((tm, D_MODEL), lambda i: (i, 0)),
                   pl.BlockSpec((tm, tn), lambda i: (i, 0))],
        out_shape=[jax.ShapeDtypeStruct((N_ALL, D_MODEL), BF16),
                   jax.ShapeDtypeStruct((N_ALL, tn), F32)],
        scratch_shapes=[pltpu.VMEM((D_MODEL, tn), BF16)],
        compiler_params=_params(1),
        name="norm_proj_kv",
    )(x_p, x_s, g_mix, w_in, k_g128, cos_t, sin_t, bd)


def _proj_rest_kernel(x_ref, w_ref, scale_ref, o_ref, wbf_ref):
    _cast_weight_once(w_ref, wbf_ref, 1)
    for c0 in range(0, o_ref.shape[1], MXU_WIDTH):
        cols = slice(c0, c0 + MXU_WIDTH)
        z = jnp.dot(x_ref[...], wbf_ref[:, cols], preferred_element_type=F32)
        o_ref[:, cols] = (z * scale_ref[:, cols]).astype(o_ref.dtype)


def _proj_q_kernel(x_ref, w_ref, g_ref, cos_ref, sin_ref, bd_ref, wo_ref, o_ref, wo_bf_ref, wbf_ref):
    _cast_weight_once(w_ref, wbf_ref, 1)
    wo_bf_ref[...] = wo_ref[...].astype(BF16)
    g128 = g_ref[...]
    bd = bd_ref[...]
    piece = o_ref.shape[0] // PROJ_Q_ROW_PIECES
    for r0 in range(0, o_ref.shape[0], piece):
        rows = slice(r0, r0 + piece)
        z_all = jnp.dot(x_ref[rows, :], wbf_ref[...], preferred_element_type=F32)
        cos_t = cos_ref[rows, :]
        sin_t = sin_ref[rows, :]
        for c0 in range(0, o_ref.shape[1], MXU_WIDTH):
            low, high = _headnorm_rope(z_all[:, c0:c0 + MXU_WIDTH], g128, cos_t, sin_t, bd)
            o_ref[rows, c0:c0 + LANES] = low.astype(o_ref.dtype)
            o_ref[rows, c0 + LANES:c0 + MXU_WIDTH] = high.astype(o_ref.dtype)


def _proj_q_rest(xn, w_in, q_g128, cos_q, sin_q, bd, rest_scale, w_out):
    tm = ROW_TILE_BIG
    n_row = N_ALL // tm
    k = D_MODEL
    x_spec = pl.BlockSpec((tm, k), lambda j, i: (i, 0))
    rope_specs = [pl.BlockSpec((1, LANES), lambda j, i: (0, 0)),
                  pl.BlockSpec((tm, LANES), lambda j, i: (i, 0)),
                  pl.BlockSpec((tm, LANES), lambda j, i: (i, 0)),
                  pl.BlockSpec((MXU_WIDTH, MXU_WIDTH), lambda j, i: (0, 0))]

    tn_q = 1024
    n_col_q = D_ATTN // tn_q
    wo_rows = D_ATTN // (n_col_q * n_row)
    wo_spec = pl.BlockSpec((wo_rows, D_MODEL), lambda j, i: (j * n_row + i, 0))
    q, w_out_bf = pl.pallas_call(
        _proj_q_kernel,
        grid=(n_col_q, n_row),
        in_specs=[x_spec, pl.BlockSpec((k, tn_q), lambda j, i: (0, j))] + rope_specs + [wo_spec],
        out_specs=[pl.BlockSpec((tm, tn_q), lambda j, i: (i, j)), wo_spec],
        out_shape=[jax.ShapeDtypeStruct((N_ALL, D_ATTN), BF16),
                   jax.ShapeDtypeStruct((D_ATTN, D_MODEL), BF16)],
        scratch_shapes=[pltpu.VMEM((k, tn_q), BF16)],
        compiler_params=_params(2),
        name="proj_q",
    )(xn, w_in, q_g128, cos_q, sin_q, bd, w_out)

    tm_r = ROW_TILE_HUGE
    tn_r = 1024
    rest_start = D_ATTN + 2 * D_KV
    rest = pl.pallas_call(
        _proj_rest_kernel,
        grid=(D_REST // tn_r, N_ALL // tm_r),
        in_specs=[pl.BlockSpec((tm_r, k), lambda j, i: (i, 0)),
                  pl.BlockSpec((pl.Element(k), pl.Element(tn_r)),
                               lambda j, i: (0, pl.multiple_of(rest_start + j * tn_r, LANES))),
                  pl.BlockSpec((1, tn_r), lambda j, i: (0, j))],
        out_specs=pl.BlockSpec((tm_r, tn_r), lambda j, i: (i, j)),
        out_shape=jax.ShapeDtypeStruct((N_ALL, D_REST), BF16),
        scratch_shapes=[pltpu.VMEM((k, tn_r), BF16)],
        compiler_params=_params(2),
        name="proj_rest",
    )(xn, w_in, rest_scale)
    return q, rest, w_out_bf


def _gated_merge(attn_half, u, u1, u2, b_ref, ga_half_ref, gc_half_ref, cw_ref):
    return _gated_merge_values(attn_half, u, u1, u2, b_ref[...].astype(F32),
                               ga_half_ref[...], gc_half_ref[...], 0.5 * cw_ref[...])


def _gated_merge_values(attn_half, u, u1, u2, b, ga_half, gc_half, cw_half):
    conv_half = cw_half[0:1, :] * u2 + cw_half[1:2, :] * u1 + cw_half[2:3, :] * u
    bc = (b * conv_half).astype(BF16)
    a = attn_half.astype(BF16)
    ta = jnp.tanh(ga_half)
    tc = jnp.tanh(gc_half)
    return (ta * a + a) + (tc * bc + bc)


def _prompt_attention_block(j, sinks_ref, q_ref, kv_prev, kv_cur, attn_ref, row0):
    n_keys = 2 * WINDOW
    q_rows = slice(row0, row0 + WINDOW)
    kv = jnp.concatenate([kv_prev, kv_cur], axis=0)
    key_row = lax.broadcasted_iota(jnp.int32, (n_keys, LANES), 0)
    low = lax.broadcasted_iota(jnp.int32, (n_keys, LANES), 1) < HEAD_DIM
    low_v = low & (key_row > 0)
    high_v = jnp.logical_not(low) & (key_row > 0)
    ones_low = jnp.where(low, 2.0, 0.0)
    ones_high = jnp.where(low, 0.0, 2.0)

    r = lax.broadcasted_iota(jnp.int32, (WINDOW, n_keys), 0)
    c = lax.broadcasted_iota(jnp.int32, (WINDOW, n_keys), 1)
    first_key = jnp.where(j > 0, 0, WINDOW)
    mask = jnp.where((c > r) & (c <= r + WINDOW) & (c >= first_key), 0.0, NEG_INF)
    mask = jnp.where(c == 0, 0.0, mask).astype(BF16)
    mask2 = jnp.concatenate([mask, mask], axis=1)
    eye = (lax.broadcasted_iota(jnp.int32, (WINDOW, WINDOW), 0)
           == lax.broadcasted_iota(jnp.int32, (WINDOW, WINDOW), 1))
    eye = jnp.where(eye, 1.0, 0.0).astype(BF16)
    sink_lane = lax.broadcasted_iota(jnp.int32, (1, LANES), 1) == 0
    not_slot0 = lax.broadcasted_iota(jnp.int32, (HEAD_DIM, n_keys), 1) > 0
    no_keys = jnp.zeros((HEAD_DIM, n_keys), F32)

    for g in range(N_KV_HEADS):
        chunk = (g // 2) * LANES
        kc_t = kv[:, chunk:chunk + LANES].T
        kg_t = kc_t[(g % 2) * HEAD_DIM:(g % 2 + 1) * HEAD_DIM]
        kg_t = jnp.where(not_slot0, kg_t, 0.0)
        vc = kv[:, D_KV + chunk:D_KV + chunk + LANES]
        if g % 2 == 0:
            v_low = jnp.where(low_v, vc, 0.0)
            v_high = pltpu.roll(v_low, HEAD_DIM, 1)
        else:
            v_high = jnp.where(high_v, vc, 0.0)
            v_low = pltpu.roll(v_high, HEAD_DIM, 1)
        k_bd_t = jnp.concatenate(
            [jnp.concatenate([kg_t, no_keys], axis=1),
             jnp.concatenate([no_keys, kg_t], axis=1)], axis=0).astype(BF16)
        k_aug = jnp.concatenate([k_bd_t, mask2], axis=0)
        v_bd = jnp.concatenate(
            [jnp.concatenate([v_low, ones_low], axis=1),
             jnp.concatenate([v_high, ones_high], axis=1)], axis=0).astype(BF16)

        for k in range(PAIRS_PER_GROUP):
            pair = g * PAIRS_PER_GROUP + k
            cols = slice(pair * LANES, (pair + 1) * LANES)
            q_aug = jnp.concatenate([q_ref[q_rows, cols], eye], axis=1)
            s = jnp.dot(q_aug, k_aug, preferred_element_type=F32)
            halves = []
            for half in range(2):
                sh = s[:, half * n_keys:(half + 1) * n_keys]
                sink = sinks_ref[0, 2 * pair + half] * LOG2E
                sink_row = jnp.where(sink_lane, sink, 0.0)
                sh = jnp.concatenate([sh[:, :LANES] + sink_row, sh[:, LANES:]], axis=1)
                halves.append(jnp.exp2(sh - jnp.max(sh, axis=1, keepdims=True)))
            p = jnp.concatenate(halves, axis=1).astype(BF16)
            o = jnp.dot(p, v_bd, preferred_element_type=F32)
            attn_ref[q_rows, cols] = (o[:, :LANES] / o[:, LANES:]).astype(attn_ref.dtype)


def _prompt_mix_kernel(sinks_ref, q_ref, kvp_ref, kvc_ref, h_ref, b_ref, c_ref, ga_ref, gc_ref,
                       cw_ref, mix_ref, tail_ref, u_scr, attn_scr):
    j = pl.program_id(0)

    @pl.when(j == 0)
    def _():
        u_scr[0:8, :] = jnp.zeros((8, D_CONV), F32)

    tm = MIX_BLOCKS * WINDOW
    for sb in range(MIX_BLOCKS):
        kv_prev = kvp_ref[...] if sb == 0 else kvc_ref[(sb - 1) * WINDOW:sb * WINDOW, :]
        _prompt_attention_block(j * MIX_BLOCKS + sb, sinks_ref, q_ref, kv_prev,
                                kvc_ref[sb * WINDOW:(sb + 1) * WINDOW, :], attn_scr, sb * WINDOW)

    rb, cb = MERGE_CHUNK
    cw_half = 0.5 * cw_ref[...]
    for r0 in range(0, tm, rb):
        rows = slice(r0, r0 + rb)
        u_scr[8 + r0:8 + r0 + rb, :] = c_ref[rows, :].astype(F32) * h_ref[rows, :].astype(F32)
        for c0 in range(0, D_MODEL, cb):
            cols = slice(c0, c0 + cb)
            x = u_scr[r0:r0 + rb + 8, cols]
            mix = _gated_merge_values(
                attn_scr[rows, cols], x[8:], pltpu.roll(x, 1, 0)[8:], pltpu.roll(x, 2, 0)[8:],
                b_ref[rows, cols].astype(F32), ga_ref[rows, cols], gc_ref[rows, cols],
                cw_half[:, cols])
            mix_ref[rows, cols] = mix.astype(mix_ref.dtype)
    tail = u_scr[tm:tm + 8, :]
    u_scr[0:8, :] = tail
    tail_ref[...] = tail


def _prompt_mix(sinks, q, kv, rest, conv_w):
    tm = MIX_BLOCKS * WINDOW
    blk = lambda c: pl.BlockSpec((tm, D_MODEL), lambda j, c=c: (j, c))
    return pl.pallas_call(
        _prompt_mix_kernel,
        grid=(N_P // tm,),
        in_specs=[pl.BlockSpec(memory_space=pltpu.SMEM),
                  pl.BlockSpec((tm, D_ATTN), lambda j: (j, 0)),
                  pl.BlockSpec((WINDOW, 2 * D_KV), lambda j: (jnp.maximum(j * MIX_BLOCKS - 1, 0), 0)),
                  pl.BlockSpec((tm, 2 * D_KV), lambda j: (j, 0)),
                  blk(0), blk(1), blk(2), blk(3), blk(4),
                  pl.BlockSpec((CONV_WIDTH, D_CONV), lambda j: (0, 0))],
        out_specs=[pl.BlockSpec((tm, D_MODEL), lambda j: (j, 0)),
                   pl.BlockSpec((8, D_CONV), lambda j: (0, 0))],
        out_shape=[jax.ShapeDtypeStruct((N_P, D_MODEL), BF16),
                   jax.ShapeDtypeStruct((8, D_CONV), F32)],
        scratch_shapes=[pltpu.VMEM((8 + tm, D_CONV), F32), pltpu.VMEM((tm, D_ATTN), F32)],
        compiler_params=_params(1),
        name="prompt_mix",
    )(sinks, q, kv, kv, rest, rest, rest, rest, rest, conv_w)


def _sample_attn_kernel(q_ref, kvn_ref, ckt_ref, cvt_ref, sink_ref, rep_ref,
                        o_ref, kwt_ref, vwt_ref, bias_c, bias_n):
    n_new = SEQ_BLOCK * DEC_SEQ
    rows = DEC_SEQ * N_HEADS
    keep = WINDOW - DEC_SEQ

    @pl.when(pl.program_id(0) == 0)
    def _():
        row = lax.broadcasted_iota(jnp.int32, (rows, WINDOW), 0)
        col = lax.broadcasted_iota(jnp.int32, (rows, WINDOW), 1)
        t = row >> 5
        bias_c[...] = jnp.where(col > t, 0.0, NEG_INF)
        for s in range(SEQ_BLOCK):
            valid_new = (col < n_new) & ((col >> 2) == s) & ((col & 3) <= t)
            bias_n[s] = jnp.where(valid_new, 0.0, NEG_INF)

    kvn = jnp.concatenate([kvn_ref[...], jnp.zeros((WINDOW - n_new, 2 * D_KV), F32)], axis=0)
    kn_t = kvn[:, :D_KV].T
    vn_t = kvn[:, D_KV:].T
    kn_t_bf = kn_t.astype(BF16)
    vn_bf = kvn[:, D_KV:].astype(BF16)

    all_rows = SEQ_BLOCK * rows
    row2 = lax.broadcasted_iota(jnp.int32, (all_rows, D_KV), 0)
    col2 = lax.broadcasted_iota(jnp.int32, (all_rows, D_KV), 1)
    same_group = ((row2 & (N_HEADS - 1)) >> 3) == (col2 >> 6)
    win_lane = lax.broadcasted_iota(jnp.int32, (D_KV, WINDOW), 1)
    sink_b = jnp.concatenate([sink_ref[...]] * SEQ_BLOCK, axis=0)
    ones = jnp.ones((2 * WINDOW, LANES), BF16)

    q_all = q_ref[...].reshape(all_rows, HEAD_DIM)
    q4 = jnp.dot(q_all, rep_ref[...], preferred_element_type=F32)
    q4 = jnp.where(same_group, q4, 0.0).astype(BF16)
    s_n = jnp.dot(q4, kn_t_bf, preferred_element_type=F32) + bias_n[...].reshape(all_rows, WINDOW)
    s_c = jnp.concatenate(
        [jnp.dot(q4[s * rows:(s + 1) * rows], ckt_ref[s].astype(BF16), preferred_element_type=F32)
         + bias_c[...] for s in range(SEQ_BLOCK)], axis=0)
    m_b = jnp.maximum(jnp.max(jnp.maximum(s_c, s_n), axis=1, keepdims=True), sink_b)
    p_c = jnp.exp2(s_c - m_b).astype(BF16)
    p_n = jnp.exp2(s_n - m_b).astype(BF16)
    l_b = (jnp.dot(jnp.concatenate([p_c, p_n], axis=1), ones, preferred_element_type=F32)
           + jnp.exp2(sink_b - m_b))
    o4 = jnp.concatenate(
        [lax.dot_general(p_c[s * rows:(s + 1) * rows], cvt_ref[s].astype(BF16),
                         (((1,), (1,)), ((), ())), preferred_element_type=F32)
         for s in range(SEQ_BLOCK)], axis=0) + jnp.dot(p_n, vn_bf, preferred_element_type=F32)
    o4 = jnp.where(same_group, o4, 0.0)
    a = o4[:, 0:LANES] + o4[:, LANES:]
    o = (a + pltpu.roll(a, HEAD_DIM, 1)) / (2.0 * l_b)
    o_ref[...] = o[:, 0:HEAD_DIM].reshape(SEQ_BLOCK, rows, HEAD_DIM)

    for s in range(SEQ_BLOCK):
        k_t = ckt_ref[s]
        v_t = cvt_ref[s]
        new_shift = keep - s * DEC_SEQ
        kwt_ref[s] = jnp.where(win_lane >= keep, pltpu.roll(kn_t, new_shift, 1),
                               pltpu.roll(k_t, keep, 1))
        vwt_ref[s] = jnp.where(win_lane >= keep, pltpu.roll(vn_t, new_shift, 1),
                               pltpu.roll(v_t, keep, 1))


def _sample_attn(q_s, kv, cache_kt, cache_vt, sink_col, rep):
    rows = DEC_SEQ * N_HEADS
    n_new = SEQ_BLOCK * DEC_SEQ
    cache_spec = pl.BlockSpec((SEQ_BLOCK, D_KV, WINDOW), lambda i: (i, 0, 0))
    return pl.pallas_call(
        _sample_attn_kernel,
        grid=(DEC_BATCH // SEQ_BLOCK,),
        in_specs=[pl.BlockSpec((SEQ_BLOCK, rows, HEAD_DIM), lambda i: (i, 0, 0)),
                  pl.BlockSpec((n_new, 2 * D_KV), lambda i: (i + N_P // n_new, 0)),
                  cache_spec, cache_spec,
                  pl.BlockSpec((rows, LANES), lambda i: (0, 0)),
                  pl.BlockSpec((HEAD_DIM, D_KV), lambda i: (0, 0))],
        out_specs=[pl.BlockSpec((SEQ_BLOCK, rows, HEAD_DIM), lambda i: (i, 0, 0)),
                   cache_spec, cache_spec],
        out_shape=[jax.ShapeDtypeStruct((DEC_BATCH, rows, HEAD_DIM), F32),
                   jax.ShapeDtypeStruct((DEC_BATCH, D_KV, WINDOW), F32),
                   jax.ShapeDtypeStruct((DEC_BATCH, D_KV, WINDOW), F32)],
        scratch_shapes=[pltpu.VMEM((rows, WINDOW), F32), pltpu.VMEM((SEQ_BLOCK, rows, WINDOW), F32)],
        compiler_params=_params(1),
        name="sample_attn",
    )(q_s, kv, cache_kt, cache_vt, sink_col, rep)


def _sample_merge_kernel(attn_ref, h_ref, b_ref, c_ref, ga_ref, gc_ref, st_ref, cw_ref,
                         mix_ref, u_ref):
    u = c_ref[...].astype(F32) * h_ref[...].astype(F32)
    u_ref[...] = u
    rows = u.shape[0]
    t = lax.broadcasted_iota(jnp.int32, u.shape, 0) & (DEC_SEQ - 1)
    st = st_ref[...]
    u1 = jnp.where(t == 0, pltpu.roll(st, rows - 1, 0), pltpu.roll(u, 1, 0))
    u2 = jnp.where(t < 2, st, pltpu.roll(u, 2, 0))
    mix = _gated_merge(attn_ref[...], u, u1, u2, b_ref, ga_ref, gc_ref, cw_ref)
    mix_ref[...] = mix.astype(mix_ref.dtype)


def _sample_merge(attn_s, rest, st_rows, conv_w):
    tm = WINDOW
    off = N_P // tm
    blk = lambda c: pl.BlockSpec((tm, D_MODEL), lambda i, c=c: (i + off, c))
    return pl.pallas_call(
        _sample_merge_kernel,
        grid=(N_S // tm,),
        in_specs=[pl.BlockSpec((tm, D_ATTN), lambda i: (i, 0)),
                  blk(0), blk(1), blk(2), blk(3), blk(4),
                  pl.BlockSpec((tm, D_CONV), lambda i: (i, 0)),
                  pl.BlockSpec((CONV_WIDTH, D_CONV), lambda i: (0, 0))],
        out_specs=[pl.BlockSpec((tm, D_MODEL), lambda i: (i, 0)),
                   pl.BlockSpec((tm, D_CONV), lambda i: (i, 0))],
        out_shape=[jax.ShapeDtypeStruct((N_S, D_MODEL), BF16),
                   jax.ShapeDtypeStruct((N_S, D_CONV), F32)],
        compiler_params=_params(1),
        name="sample_merge",
    )(attn_s, rest, rest, rest, rest, rest, st_rows, conv_w)


def _out_proj_kernel(mp_ref, ms_ref, w_ref, xp_ref, xs_ref, gf_ref, y_ref, yn_ref):
    is_prompt = pl.program_id(0) < N_P // ROW_TILE_SMALL
    res = jnp.where(is_prompt, xp_ref[...], xs_ref[...])
    mix = jnp.where(is_prompt, mp_ref[...], ms_ref[...])
    y = res + jnp.dot(mix, w_ref[...], preferred_element_type=F32)
    y_ref[...] = y
    yn_ref[...] = _rmsnorm_rows(y, gf_ref[...]).astype(yn_ref.dtype)


def _out_proj(mix_p, mix_s, w_out_bf, x_p, x_s, g_ffn):
    tm = ROW_TILE_SMALL
    last_p = N_P // tm - 1
    once = pl.Buffered(1)
    return pl.pallas_call(
        _out_proj_kernel,
        grid=(N_ALL // tm,),
        in_specs=[pl.BlockSpec((tm, D_ATTN), lambda i: (jnp.minimum(i, last_p), 0)),
                  pl.BlockSpec((tm, D_ATTN), lambda i: (0, 0), pipeline_mode=once),
                  pl.BlockSpec((D_ATTN, D_MODEL), lambda i: (0, 0), pipeline_mode=once),
                  pl.BlockSpec((tm, D_MODEL), lambda i: (jnp.minimum(i, last_p), 0)),
                  pl.BlockSpec((tm, D_MODEL), lambda i: (0, 0), pipeline_mode=once),
                  pl.BlockSpec((1, D_MODEL), lambda i: (0, 0))],
        out_specs=[pl.BlockSpec((tm, D_MODEL), lambda i: (i, 0)),
                   pl.BlockSpec((tm, D_MODEL), lambda i: (i, 0))],
        out_shape=[jax.ShapeDtypeStruct((N_ALL, D_MODEL), F32),
                   jax.ShapeDtypeStruct((N_ALL, D_MODEL), BF16)],
        compiler_params=_params(1),
        name="out_proj",
    )(mix_p, mix_s, w_out_bf, x_p, x_s, g_ffn)


def _ffn_up_kernel(x_ref, wg_ref, wu_ref, wd_ref, o_ref, wd_bf_ref, wg_bf, wu_bf):
    @pl.when(pl.program_id(1) == 0)
    def _():
        wg_bf[...] = wg_ref[...].astype(BF16)
        wu_bf[...] = wu_ref[...].astype(BF16)

    wd_bf_ref[...] = wd_ref[...].astype(BF16)

    piece = o_ref.shape[0] // FFN_UP_ROW_PIECES
    for c0 in range(0, o_ref.shape[1], MXU_WIDTH):
        cols = slice(c0, c0 + MXU_WIDTH)
        for r0 in range(0, o_ref.shape[0], piece):
            rows = slice(r0, r0 + piece)
            g = jnp.dot(x_ref[rows, :], wg_bf[:, cols], preferred_element_type=F32)
            u = jnp.dot(x_ref[rows, :], wu_bf[:, cols], preferred_element_type=F32)
            o_ref[rows, cols] = ((g * jax.nn.sigmoid(g)) * u).astype(o_ref.dtype)


def _ffn_up(yn, w_gate_up, w_down):
    tm = ROW_TILE_HUGE
    tn = 512
    nt = D_FF // tn
    n_row = N_ALL // tm
    wd_rows = D_FF // (nt * n_row)
    wd_spec = pl.BlockSpec((wd_rows, D_MODEL), lambda j, i: (j * n_row + i, 0))
    return pl.pallas_call(
        _ffn_up_kernel,
        grid=(nt, n_row),
        in_specs=[pl.BlockSpec((tm, D_MODEL), lambda j, i: (i, 0)),
                  pl.BlockSpec((D_MODEL, tn), lambda j, i: (0, j)),
                  pl.BlockSpec((D_MODEL, tn), lambda j, i: (0, j + nt)),
                  wd_spec],
        out_specs=[pl.BlockSpec((tm, tn), lambda j, i: (i, j)), wd_spec],
        out_shape=[jax.ShapeDtypeStruct((N_ALL, D_FF), BF16),
                   jax.ShapeDtypeStruct((D_FF, D_MODEL), BF16)],
        scratch_shapes=[pltpu.VMEM((D_MODEL, tn), BF16), pltpu.VMEM((D_MODEL, tn), BF16)],
        compiler_params=_params(2),
        name="ffn_up",
    )(yn, w_gate_up, w_gate_up, w_down)


def _ffn_down_kernel(a_ref, w_ref, r_ref, yp_ref, ys_ref):
    i = pl.program_id(0)
    n_prompt_tiles = N_P // FFN_DOWN_ROWS
    def project_into(y_ref):
        for c0 in range(0, D_MODEL, FFN_DOWN_CHUNK):
            cols = slice(c0, c0 + FFN_DOWN_CHUNK)
            y_ref[:, cols] = r_ref[:, cols] + jnp.dot(a_ref[...], w_ref[:, cols],
                                                      preferred_element_type=F32)

    @pl.when(i < n_prompt_tiles)
    def _():
        project_into(yp_ref)

    @pl.when(i >= n_prompt_tiles)
    def _():
        project_into(ys_ref)


def _ffn_down(act, w_down_bf, y1):
    tm = FFN_DOWN_ROWS
    n_p = N_P // tm
    return pl.pallas_call(
        _ffn_down_kernel,
        grid=(N_ALL // tm,),
        in_specs=[pl.BlockSpec((tm, D_FF), lambda i: (i, 0)),
                  pl.BlockSpec((D_FF, D_MODEL), lambda i: (0, 0), pipeline_mode=pl.Buffered(1)),
                  pl.BlockSpec((tm, D_MODEL), lambda i: (i, 0))],
        out_specs=[pl.BlockSpec((tm, D_MODEL), lambda i: (jnp.minimum(i, n_p - 1), 0)),
                   pl.BlockSpec((tm, D_MODEL), lambda i: (jnp.maximum(i - n_p, 0), 0))],
        out_shape=[jax.ShapeDtypeStruct((N_P, D_MODEL), F32),
                   jax.ShapeDtypeStruct((N_S, D_MODEL), F32)],
        compiler_params=_params(1),
        name="ffn_down",
    )(act, w_down_bf, y1)


def _rope_tables():
    inv = ROPE_THETA ** (-jnp.arange(0, HEAD_DIM, 2, dtype=F32) / HEAD_DIM)
    inv_t = jnp.tile(inv, LANES // (HEAD_DIM // 2))[None, :]
    sign = jnp.asarray(np.tile(np.repeat([-1.0, 1.0], HEAD_DIM // 2), LANES // HEAD_DIM), F32)[None, :]
    ang_a = (jnp.arange(N_P // LANES, dtype=jnp.int32) * LANES).astype(F32)[:, None] * inv_t
    ang_b = jnp.arange(LANES, dtype=jnp.int32).astype(F32)[:, None] * inv_t
    ca, sa = jnp.cos(ang_a)[:, None, :], jnp.sin(ang_a)[:, None, :]
    cb, sb = jnp.cos(ang_b)[None, :, :], jnp.sin(ang_b)[None, :, :]
    cos_p = (ca * cb - sa * sb).reshape(N_P, LANES)
    sin_p = (sa * cb + ca * sb).reshape(N_P, LANES)
    ang_s = (PAST_LEN + jnp.arange(DEC_SEQ, dtype=jnp.int32)).astype(F32)[:, None] * inv_t
    cos_s = jnp.tile(jnp.cos(ang_s), (DEC_BATCH, 1))
    sin_s = jnp.tile(jnp.sin(ang_s), (DEC_BATCH, 1))
    return (jnp.concatenate([cos_p, cos_s], axis=0),
            jnp.concatenate([sin_p, sin_s], axis=0) * sign)


def kernel(x_prompt, x_sample, cache_k_win, cache_v_win, state_conv, norm_mix, w_in, q_norm, k_norm,
           sinks, conv_w, w_out, norm_ffn, w_gate_up, w_down):
    assert x_prompt.shape == (1, SEQ, D_MODEL) and x_sample.shape == (DEC_BATCH, DEC_SEQ, D_MODEL)
    assert w_in.shape == (1, D_MODEL, D_IN)
    assert cache_k_win.shape == (1, DEC_BATCH, WINDOW, N_KV_HEADS, HEAD_DIM)

    x_p = x_prompt.reshape(N_P, D_MODEL)
    x_s = x_sample.reshape(N_S, D_MODEL)
    w_in2 = w_in.reshape(D_MODEL, D_IN)
    w_out2 = w_out.reshape(D_ATTN, D_MODEL)
    w_gu2 = w_gate_up.reshape(D_MODEL, 2 * D_FF)
    w_dn2 = w_down.reshape(D_FF, D_MODEL)
    conv_w2 = conv_w.reshape(CONV_WIDTH, D_CONV)
    cache_kt = jnp.transpose(cache_k_win.reshape(DEC_BATCH, WINDOW, D_KV), (0, 2, 1))
    cache_vt = jnp.transpose(cache_v_win.reshape(DEC_BATCH, WINDOW, D_KV), (0, 2, 1))

    cos_t, sin_t = _rope_tables()
    q_g128 = jnp.tile(q_norm.reshape(1, HEAD_DIM) * Q_SCALE, (1, LANES // HEAD_DIM))
    k_g128 = jnp.tile(k_norm.reshape(1, HEAD_DIM), (1, LANES // HEAD_DIM))
    rest_scale = jnp.asarray(np.repeat([1.0, 1.0, 1.0, 0.5, 0.5], D_MODEL)[None, :], F32)
    head_of_lane = np.arange(MXU_WIDTH) // HEAD_DIM
    bd = jnp.asarray(np.where(head_of_lane[:, None] == head_of_lane[None, :], 1.0 / HEAD_DIM, 0.0),
                     dtype=BF16)
    sinks2 = sinks.reshape(1, N_HEADS).astype(F32)
    sink_col = jnp.broadcast_to(jnp.tile(sinks2.reshape(N_HEADS) * LOG2E, DEC_SEQ)[:, None],
                                (DEC_SEQ * N_HEADS, LANES))
    rep = jnp.asarray(np.tile(np.eye(HEAD_DIM), (1, N_KV_HEADS)), dtype=BF16)
    st_rows = jnp.pad(state_conv.reshape(DEC_BATCH, CONV_WIDTH - 1, D_CONV),
                      ((0, 0), (0, DEC_SEQ - (CONV_WIDTH - 1)), (0, 0))).reshape(N_S, D_CONV)

    xn, kv = _norm_proj_kv(x_p, x_s, norm_mix.reshape(1, D_MODEL), w_in2, k_g128, cos_t, sin_t, bd)
    q, rest, w_out_bf = _proj_q_rest(xn, w_in2, q_g128, cos_t, sin_t, bd, rest_scale, w_out2)

    mix_p, tail = _prompt_mix(sinks2, q, kv, rest, conv_w2)
    q_s = q[N_P:].reshape(DEC_BATCH, DEC_SEQ * N_HEADS, HEAD_DIM)
    attn_s, kwt, vwt = _sample_attn(q_s, kv, cache_kt, cache_vt, sink_col, rep)
    mix_s, u_s = _sample_merge(attn_s.reshape(N_S, D_ATTN), rest, st_rows, conv_w2)

    y1, yn = _out_proj(mix_p, mix_s, w_out_bf, x_p, x_s, norm_ffn.reshape(1, D_MODEL))
    act, w_down_bf = _ffn_up(yn, w_gu2, w_dn2)
    y_p, y_s = _ffn_down(act, w_down_bf, y1)

    kv_tail = kv[N_P - WINDOW:N_P]
    win_shape = (1, DEC_BATCH, WINDOW, N_KV_HEADS, HEAD_DIM)
    return (y_p.reshape(1, SEQ, D_MODEL),
            y_s.reshape(DEC_BATCH, DEC_SEQ, D_MODEL),
            kv_tail[:, :D_KV].reshape(1, 1, WINDOW, N_KV_HEADS, HEAD_DIM),
            kv_tail[:, D_KV:].reshape(1, 1, WINDOW, N_KV_HEADS, HEAD_DIM),
            tail[8 - (CONV_WIDTH - 1):].reshape(1, 1, CONV_WIDTH - 1, D_CONV),
            jnp.transpose(kwt, (0, 2, 1)).reshape(win_shape),
            jnp.transpose(vwt, (0, 2, 1)).reshape(win_shape),
            u_s.reshape(DEC_BATCH, DEC_SEQ, D_CONV)[:, DEC_SEQ - (CONV_WIDTH - 1):][None])
```

```python
import math

import numpy as np
import jax
import jax.numpy as jnp
from jax import lax
from jax.experimental import pallas as pl
from jax.experimental.pallas import tpu as pltpu

F32 = jnp.float32
BF16 = jnp.bfloat16

D_MODEL = 2048
SEQ = 8192
DEC_BATCH = 128
DEC_SEQ = 4
PAST_LEN = 8192
N_HEADS = 32
HEAD_DIM = 64
N_KV_HEADS = 4
GROUP = N_HEADS // N_KV_HEADS
D_ATTN = N_HEADS * HEAD_DIM
D_KV = N_KV_HEADS * HEAD_DIM
WINDOW = 128
D_CONV = D_MODEL
CONV_WIDTH = 3
D_FF = 5632
ROPE_THETA = 10000.0
EPS = 1e-6
NEG_INF = -1e30
D_IN = 2 * D_ATTN + 2 * D_KV + 3 * D_CONV + D_CONV
D_REST = 5 * D_MODEL
LOG2E = math.log2(math.e)
Q_SCALE = HEAD_DIM ** -0.5 * LOG2E

N_P = SEQ
N_S = DEC_BATCH * DEC_SEQ
N_ALL = N_P + N_S

LANES = 128
VMEM_LIMIT = 56 * 1024 * 1024

ROW_TILE_SMALL = 512
ROW_TILE_BIG = 1088
ROW_TILE_HUGE = 2176
MXU_WIDTH = 256
PROJ_Q_ROW_PIECES = 4
FFN_UP_ROW_PIECES = 4
FFN_DOWN_ROWS = 256
FFN_DOWN_CHUNK = 512
MIX_BLOCKS = 4
MERGE_CHUNK = (32, 256)
SEQ_BLOCK = 8
PAIRS_PER_GROUP = GROUP // 2


def _params(n_axes):
    return pltpu.CompilerParams(
        dimension_semantics=("arbitrary",) * n_axes, vmem_limit_bytes=VMEM_LIMIT)


def _rmsnorm_rows(x, g):
    ms = jnp.mean(x * x, axis=-1, keepdims=True)
    return x * lax.rsqrt(ms + EPS) * g


def _cast_weight_once(w_ref, wbf_ref, axis):
    @pl.when(pl.program_id(axis) == 0)
    def _():
        wbf_ref[...] = w_ref[...].astype(BF16)


def _headnorm_rope(z, g128, cos_t, sin_t, bd):
    ms = jnp.dot((z * z).astype(BF16), bd, preferred_element_type=F32)
    scale = lax.rsqrt(ms + EPS)
    lane = lax.broadcasted_iota(jnp.int32, (z.shape[0], LANES), 1)
    first_half = (lane & (HEAD_DIM // 2)) == 0
    halves = []
    for c in range(0, MXU_WIDTH, LANES):
        y = z[:, c:c + LANES] * scale[:, c:c + LANES] * g128
        partner = jnp.where(first_half,
                            pltpu.roll(y, LANES - HEAD_DIM // 2, 1),
                            pltpu.roll(y, HEAD_DIM // 2, 1))
        halves.append(y * cos_t + partner * sin_t)
    return halves


def _norm_proj_kv_kernel(xp_ref, xs_ref, gm_ref, w_ref, g_ref, cos_ref, sin_ref, bd_ref,
                         xn_ref, kv_ref, wbf_ref):
    _cast_weight_once(w_ref, wbf_ref, 0)
    is_prompt = pl.program_id(0) < N_P // ROW_TILE_SMALL
    g128 = g_ref[...]
    bd = bd_ref[...]
    piece = xn_ref.shape[0] // PROJ_Q_ROW_PIECES
    for r0 in range(0, xn_ref.shape[0], piece):
        rows = slice(r0, r0 + piece)
        x = jnp.where(is_prompt, xp_ref[rows, :], xs_ref[rows, :])
        xn = _rmsnorm_rows(x, gm_ref[...]).astype(BF16)
        xn_ref[rows, :] = xn
        z = jnp.dot(xn, wbf_ref[...], preferred_element_type=F32)
        k_low, k_high = _headnorm_rope(z[:, :D_KV], g128, cos_ref[rows, :], sin_ref[rows, :], bd)
        kv_ref[rows, 0:LANES] = k_low
        kv_ref[rows, LANES:D_KV] = k_high
        kv_ref[rows, D_KV:] = z[:, D_KV:]


def _norm_proj_kv(x_p, x_s, g_mix, w_in, k_g128, cos_t, sin_t, bd):
    tm = ROW_TILE_SMALL
    tn = 2 * D_KV
    last_p = N_P // tm - 1
    const = lambda shape: pl.BlockSpec(shape, lambda i: (0, 0))
    return pl.pallas_call(
        _norm_proj_kv_kernel,
        grid=(N_ALL // tm,),
        in_specs=[pl.BlockSpec((tm, D_MODEL), lambda i: (jnp.minimum(i, last_p), 0)),
                  const((tm, D_MODEL)),
                  const((1, D_MODEL)),
                  pl.BlockSpec((D_MODEL, tn), lambda i: (0, D_ATTN // tn)),
                  const((1, LANES)),
                  pl.BlockSpec((tm, LANES), lambda i: (i, 0)),
                  pl.BlockSpec((tm, LANES), lambda i: (i, 0)),
                  const((MXU_WIDTH, MXU_WIDTH))],
        out_specs=[pl.BlockSpec((tm, D_MODEL), lambda i: (i, 0)),
                   pl.BlockSpec((tm, tn), lambda i: (i, 0))],
        out_shape=[jax.ShapeDtypeStruct((N_ALL, D_MODEL), BF16),
                   jax.ShapeDtypeStruct((N_ALL, tn), F32)],
        scratch_shapes=[pltpu.VMEM((D_MODEL, tn), BF16)],
        compiler_params=_params(1),
        name="norm_proj_kv",
    )(x_p, x_s, g_mix, w_in, k_g128, cos_t, sin_t, bd)


def _proj_rest_kernel(x_ref, w_ref, scale_ref, o_ref, wbf_ref):
    _cast_weight_once(w_ref, wbf_ref, 1)
    for c0 in range(0, o_ref.shape[1], MXU_WIDTH):
        cols = slice(c0, c0 + MXU_WIDTH)
        z = jnp.dot(x_ref[...], wbf_ref[:, cols], preferred_element_type=F32)
        o_ref[:, cols] = (z * scale_ref[:, cols]).astype(o_ref.dtype)


def _proj_q_kernel(x_ref, w_ref, g_ref, cos_ref, sin_ref, bd_ref, wo_ref, o_ref, wo_bf_ref, wbf_ref):
    _cast_weight_once(w_ref, wbf_ref, 1)
    wo_bf_ref[...] = wo_ref[...].astype(BF16)
    g128 = g_ref[...]
    bd = bd_ref[...]
    piece = o_ref.shape[0] // PROJ_Q_ROW_PIECES
    for r0 in range(0, o_ref.shape[0], piece):
        rows = slice(r0, r0 + piece)
        z_all = jnp.dot(x_ref[rows, :], wbf_ref[...], preferred_element_type=F32)
        cos_t = cos_ref[rows, :]
        sin_t = sin_ref[rows, :]
        for c0 in range(0, o_ref.shape[1], MXU_WIDTH):
            low, high = _headnorm_rope(z_all[:, c0:c0 + MXU_WIDTH], g128, cos_t, sin_t, bd)
            o_ref[rows, c0:c0 + LANES] = low.astype(o_ref.dtype)
            o_ref[rows, c0 + LANES:c0 + MXU_WIDTH] = high.astype(o_ref.dtype)


def _proj_q_rest(xn, w_in, q_g128, cos_q, sin_q, bd, rest_scale, w_out):
    tm = ROW_TILE_BIG
    n_row = N_ALL // tm
    k = D_MODEL
    x_spec = pl.BlockSpec((tm, k), lambda j, i: (i, 0))
    rope_specs = [pl.BlockSpec((1, LANES), lambda j, i: (0, 0)),
                  pl.BlockSpec((tm, LANES), lambda j, i: (i, 0)),
                  pl.BlockSpec((tm, LANES), lambda j, i: (i, 0)),
                  pl.BlockSpec((MXU_WIDTH, MXU_WIDTH), lambda j, i: (0, 0))]

    tn_q = 1024
    n_col_q = D_ATTN // tn_q
    wo_rows = D_ATTN // (n_col_q * n_row)
    wo_spec = pl.BlockSpec((wo_rows, D_MODEL), lambda j, i: (j * n_row + i, 0))
    q, w_out_bf = pl.pallas_call(
        _proj_q_kernel,
        grid=(n_col_q, n_row),
        in_specs=[x_spec, pl.BlockSpec((k, tn_q), lambda j, i: (0, j))] + rope_specs + [wo_spec],
        out_specs=[pl.BlockSpec((tm, tn_q), lambda j, i: (i, j)), wo_spec],
        out_shape=[jax.ShapeDtypeStruct((N_ALL, D_ATTN), BF16),
                   jax.ShapeDtypeStruct((D_ATTN, D_MODEL), BF16)],
        scratch_shapes=[pltpu.VMEM((k, tn_q), BF16)],
        compiler_params=_params(2),
        name="proj_q",
    )(xn, w_in, q_g128, cos_q, sin_q, bd, w_out)

    tm_r = ROW_TILE_HUGE
    tn_r = 1024
    rest_start = D_ATTN + 2 * D_KV
    rest = pl.pallas_call(
        _proj_rest_kernel,
        grid=(D_REST // tn_r, N_ALL // tm_r),
        in_specs=[pl.BlockSpec((tm_r, k), lambda j, i: (i, 0)),
                  pl.BlockSpec((pl.Element(k), pl.Element(tn_r)),
                               lambda j, i: (0, pl.multiple_of(rest_start + j * tn_r, LANES))),
                  pl.BlockSpec((1, tn_r), lambda j, i: (0, j))],
        out_specs=pl.BlockSpec((tm_r, tn_r), lambda j, i: (i, j)),
        out_shape=jax.ShapeDtypeStruct((N_ALL, D_REST), BF16),
        scratch_shapes=[pltpu.VMEM((k, tn_r), BF16)],
        compiler_params=_params(2),
        name="proj_rest",
    )(xn, w_in, rest_scale)
    return q, rest, w_out_bf


def _gated_merge(attn_half, u, u1, u2, b_ref, ga_half_ref, gc_half_ref, cw_ref):
    return _gated_merge_values(attn_half, u, u1, u2, b_ref[...].astype(F32),
                               ga_half_ref[...], gc_half_ref[...], 0.5 * cw_ref[...])


def _gated_merge_values(attn_half, u, u1, u2, b, ga_half, gc_half, cw_half):
    conv_half = cw_half[0:1, :] * u2 + cw_half[1:2, :] * u1 + cw_half[2:3, :] * u
    bc = (b * conv_half).astype(BF16)
    a = attn_half.astype(BF16)
    ta = jnp.tanh(ga_half)
    tc = jnp.tanh(gc_half)
    return (ta * a + a) + (tc * bc + bc)


def _prompt_attention_block(j, sinks_ref, q_ref, kv_prev, kv_cur, attn_ref, row0):
    n_keys = 2 * WINDOW
    q_rows = slice(row0, row0 + WINDOW)
    kv = jnp.concatenate([kv_prev, kv_cur], axis=0)
    key_row = lax.broadcasted_iota(jnp.int32, (n_keys, LANES), 0)
    low = lax.broadcasted_iota(jnp.int32, (n_keys, LANES), 1) < HEAD_DIM
    low_v = low & (key_row > 0)
    high_v = jnp.logical_not(low) & (key_row > 0)
    ones_low = jnp.where(low, 2.0, 0.0)
    ones_high = jnp.where(low, 0.0, 2.0)

    r = lax.broadcasted_iota(jnp.int32, (WINDOW, n_keys), 0)
    c = lax.broadcasted_iota(jnp.int32, (WINDOW, n_keys), 1)
    first_key = jnp.where(j > 0, 0, WINDOW)
    mask = jnp.where((c > r) & (c <= r + WINDOW) & (c >= first_key), 0.0, NEG_INF)
    mask = jnp.where(c == 0, 0.0, mask).astype(BF16)
    mask2 = jnp.concatenate([mask, mask], axis=1)
    eye = (lax.broadcasted_iota(jnp.int32, (WINDOW, WINDOW), 0)
           == lax.broadcasted_iota(jnp.int32, (WINDOW, WINDOW), 1))
    eye = jnp.where(eye, 1.0, 0.0).astype(BF16)
    sink_lane = lax.broadcasted_iota(jnp.int32, (1, LANES), 1) == 0
    not_slot0 = lax.broadcasted_iota(jnp.int32, (HEAD_DIM, n_keys), 1) > 0
    no_keys = jnp.zeros((HEAD_DIM, n_keys), F32)

    for g in range(N_KV_HEADS):
        chunk = (g // 2) * LANES
        kc_t = kv[:, chunk:chunk + LANES].T
        kg_t = kc_t[(g % 2) * HEAD_DIM:(g % 2 + 1) * HEAD_DIM]
        kg_t = jnp.where(not_slot0, kg_t, 0.0)
        vc = kv[:, D_KV + chunk:D_KV + chunk + LANES]
        if g % 2 == 0:
            v_low = jnp.where(low_v, vc, 0.0)
            v_high = pltpu.roll(v_low, HEAD_DIM, 1)
        else:
            v_high = jnp.where(high_v, vc, 0.0)
            v_low = pltpu.roll(v_high, HEAD_DIM, 1)
        k_bd_t = jnp.concatenate(
            [jnp.concatenate([kg_t, no_keys], axis=1),
             jnp.concatenate([no_keys, kg_t], axis=1)], axis=0).astype(BF16)
        k_aug = jnp.concatenate([k_bd_t, mask2], axis=0)
        v_bd = jnp.concatenate(
            [jnp.concatenate([v_low, ones_low], axis=1),
             jnp.concatenate([v_high, ones_high], axis=1)], axis=0).astype(BF16)

        for k in range(PAIRS_PER_GROUP):
            pair = g * PAIRS_PER_GROUP + k
            cols = slice(pair * LANES, (pair + 1) * LANES)
            q_aug = jnp.concatenate([q_ref[q_rows, cols], eye], axis=1)
            s = jnp.dot(q_aug, k_aug, preferred_element_type=F32)
            halves = []
            for half in range(2):
                sh = s[:, half * n_keys:(half + 1) * n_keys]
                sink = sinks_ref[0, 2 * pair + half] * LOG2E
                sink_row = jnp.where(sink_lane, sink, 0.0)
                sh = jnp.concatenate([sh[:, :LANES] + sink_row, sh[:, LANES:]], axis=1)
                halves.append(jnp.exp2(sh - jnp.max(sh, axis=1, keepdims=True)))
            p = jnp.concatenate(halves, axis=1).astype(BF16)
            o = jnp.dot(p, v_bd, preferred_element_type=F32)
            attn_ref[q_rows, cols] = (o[:, :LANES] / o[:, LANES:]).astype(attn_ref.dtype)


def _prompt_mix_kernel(sinks_ref, q_ref, kvp_ref, kvc_ref, rest_ref, cw_ref,
                       mix_ref, tail_ref, u_scr, attn_scr):
    j = pl.program_id(0)
    h_off, b_off, c_off, ga_off, gc_off = (n * D_MODEL for n in range(5))

    @pl.when(j == 0)
    def _():
        u_scr[0:8, :] = jnp.zeros((8, D_CONV), F32)

    tm = MIX_BLOCKS * WINDOW
    for sb in range(MIX_BLOCKS):
        kv_prev = kvp_ref[...] if sb == 0 else kvc_ref[(sb - 1) * WINDOW:sb * WINDOW, :]
        _prompt_attention_block(j * MIX_BLOCKS + sb, sinks_ref, q_ref, kv_prev,
                                kvc_ref[sb * WINDOW:(sb + 1) * WINDOW, :], attn_scr, sb * WINDOW)

    rb, cb = MERGE_CHUNK
    cw_half = 0.5 * cw_ref[...]
    for r0 in range(0, tm, rb):
        rows = slice(r0, r0 + rb)
        u_scr[8 + r0:8 + r0 + rb, :] = (rest_ref[rows, c_off:c_off + D_MODEL].astype(F32)
                                        * rest_ref[rows, h_off:h_off + D_MODEL].astype(F32))
        for c0 in range(0, D_MODEL, cb):
            cols = slice(c0, c0 + cb)
            x = u_scr[r0:r0 + rb + 8, cols]
            mix = _gated_merge_values(
                attn_scr[rows, cols], x[8:], pltpu.roll(x, 1, 0)[8:], pltpu.roll(x, 2, 0)[8:],
                rest_ref[rows, b_off + c0:b_off + c0 + cb].astype(F32),
                rest_ref[rows, ga_off + c0:ga_off + c0 + cb],
                rest_ref[rows, gc_off + c0:gc_off + c0 + cb], cw_half[:, cols])
            mix_ref[rows, cols] = mix.astype(mix_ref.dtype)
    tail = u_scr[tm:tm + 8, :]
    u_scr[0:8, :] = tail
    tail_ref[...] = tail


def _prompt_mix(sinks, q, kv, rest, conv_w):
    tm = MIX_BLOCKS * WINDOW
    return pl.pallas_call(
        _prompt_mix_kernel,
        grid=(N_P // tm,),
        in_specs=[pl.BlockSpec(memory_space=pltpu.SMEM),
                  pl.BlockSpec((tm, D_ATTN), lambda j: (j, 0)),
                  pl.BlockSpec((WINDOW, 2 * D_KV), lambda j: (jnp.maximum(j * MIX_BLOCKS - 1, 0), 0)),
                  pl.BlockSpec((tm, 2 * D_KV), lambda j: (j, 0)),
                  pl.BlockSpec((tm, D_REST), lambda j: (j, 0)),
                  pl.BlockSpec((CONV_WIDTH, D_CONV), lambda j: (0, 0))],
        out_specs=[pl.BlockSpec((tm, D_MODEL), lambda j: (j, 0)),
                   pl.BlockSpec((8, D_CONV), lambda j: (0, 0))],
        out_shape=[jax.ShapeDtypeStruct((N_P, D_MODEL), BF16),
                   jax.ShapeDtypeStruct((8, D_CONV), F32)],
        scratch_shapes=[pltpu.VMEM((8 + tm, D_CONV), F32), pltpu.VMEM((tm, D_ATTN), F32)],
        compiler_params=_params(1),
        name="prompt_mix",
    )(sinks, q, kv, kv, rest, conv_w)


def _sample_attn_kernel(q_ref, kvn_ref, ckt_ref, cvt_ref, sink_ref, rep_ref,
                        o_ref, kwt_ref, vwt_ref, bias_c, bias_n):
    n_new = SEQ_BLOCK * DEC_SEQ
    rows = DEC_SEQ * N_HEADS
    keep = WINDOW - DEC_SEQ

    @pl.when(pl.program_id(0) == 0)
    def _():
        row = lax.broadcasted_iota(jnp.int32, (rows, WINDOW), 0)
        col = lax.broadcasted_iota(jnp.int32, (rows, WINDOW), 1)
        t = row >> 5
        bias_c[...] = jnp.where(col > t, 0.0, NEG_INF)
        for s in range(SEQ_BLOCK):
            valid_new = (col < n_new) & ((col >> 2) == s) & ((col & 3) <= t)
            bias_n[s] = jnp.where(valid_new, 0.0, NEG_INF)

    kvn = jnp.concatenate([kvn_ref[...], jnp.zeros((WINDOW - n_new, 2 * D_KV), F32)], axis=0)
    kn_t = kvn[:, :D_KV].T
    vn_t = kvn[:, D_KV:].T
    kn_t_bf = kn_t.astype(BF16)
    vn_bf = kvn[:, D_KV:].astype(BF16)

    all_rows = SEQ_BLOCK * rows
    row2 = lax.broadcasted_iota(jnp.int32, (all_rows, D_KV), 0)
    col2 = lax.broadcasted_iota(jnp.int32, (all_rows, D_KV), 1)
    same_group = ((row2 & (N_HEADS - 1)) >> 3) == (col2 >> 6)
    win_lane = lax.broadcasted_iota(jnp.int32, (D_KV, WINDOW), 1)
    sink_b = jnp.concatenate([sink_ref[...]] * SEQ_BLOCK, axis=0)
    ones = jnp.ones((2 * WINDOW, LANES), BF16)

    q_all = q_ref[...].reshape(all_rows, HEAD_DIM)
    q4 = jnp.dot(q_all, rep_ref[...], preferred_element_type=F32)
    q4 = jnp.where(same_group, q4, 0.0).astype(BF16)
    s_n = jnp.dot(q4, kn_t_bf, preferred_element_type=F32) + bias_n[...].reshape(all_rows, WINDOW)
    s_c = jnp.concatenate(
        [jnp.dot(q4[s * rows:(s + 1) * rows], ckt_ref[s].astype(BF16), preferred_element_type=F32)
         + bias_c[...] for s in range(SEQ_BLOCK)], axis=0)
    m_b = jnp.maximum(jnp.max(jnp.maximum(s_c, s_n), axis=1, keepdims=True), sink_b)
    p_c = jnp.exp2(s_c - m_b).astype(BF16)
    p_n = jnp.exp2(s_n - m_b).astype(BF16)
    l_b = (jnp.dot(jnp.concatenate([p_c, p_n], axis=1), ones, preferred_element_type=F32)
           + jnp.exp2(sink_b - m_b))
    o4 = jnp.concatenate(
        [lax.dot_general(p_c[s * rows:(s + 1) * rows], cvt_ref[s].astype(BF16),
                         (((1,), (1,)), ((), ())), preferred_element_type=F32)
         for s in range(SEQ_BLOCK)], axis=0) + jnp.dot(p_n, vn_bf, preferred_element_type=F32)
    o4 = jnp.where(same_group, o4, 0.0)
    a = o4[:, 0:LANES] + o4[:, LANES:]
    o = (a + pltpu.roll(a, HEAD_DIM, 1)) / (2.0 * l_b)
    o_ref[...] = o[:, 0:HEAD_DIM].reshape(SEQ_BLOCK, rows, HEAD_DIM).astype(o_ref.dtype)

    for s in range(SEQ_BLOCK):
        k_t = ckt_ref[s]
        v_t = cvt_ref[s]
        new_shift = keep - s * DEC_SEQ
        kwt_ref[s] = jnp.where(win_lane >= keep, pltpu.roll(kn_t, new_shift, 1),
                               pltpu.roll(k_t, keep, 1))
        vwt_ref[s] = jnp.where(win_lane >= keep, pltpu.roll(vn_t, new_shift, 1),
                               pltpu.roll(v_t, keep, 1))


def _sample_attn(q_s, kv, cache_kt, cache_vt, sink_col, rep):
    rows = DEC_SEQ * N_HEADS
    n_new = SEQ_BLOCK * DEC_SEQ
    cache_spec = pl.BlockSpec((SEQ_BLOCK, D_KV, WINDOW), lambda i: (i, 0, 0))
    return pl.pallas_call(
        _sample_attn_kernel,
        grid=(DEC_BATCH // SEQ_BLOCK,),
        in_specs=[pl.BlockSpec((SEQ_BLOCK, rows, HEAD_DIM), lambda i: (i, 0, 0)),
                  pl.BlockSpec((n_new, 2 * D_KV), lambda i: (i + N_P // n_new, 0)),
                  cache_spec, cache_spec,
                  pl.BlockSpec((rows, LANES), lambda i: (0, 0)),
                  pl.BlockSpec((HEAD_DIM, D_KV), lambda i: (0, 0))],
        out_specs=[pl.BlockSpec((SEQ_BLOCK, rows, HEAD_DIM), lambda i: (i, 0, 0)),
                   cache_spec, cache_spec],
        out_shape=[jax.ShapeDtypeStruct((DEC_BATCH, rows, HEAD_DIM), BF16),
                   jax.ShapeDtypeStruct((DEC_BATCH, D_KV, WINDOW), F32),
                   jax.ShapeDtypeStruct((DEC_BATCH, D_KV, WINDOW), F32)],
        scratch_shapes=[pltpu.VMEM((rows, WINDOW), F32), pltpu.VMEM((SEQ_BLOCK, rows, WINDOW), F32)],
        compiler_params=_params(1),
        name="sample_attn",
    )(q_s, kv, cache_kt, cache_vt, sink_col, rep)


def _sample_merge_kernel(attn_ref, h_ref, b_ref, c_ref, ga_ref, gc_ref, st_ref, cw_ref,
                         mix_ref, u_ref):
    u = c_ref[...].astype(F32) * h_ref[...].astype(F32)
    u_ref[...] = u
    rows = u.shape[0]
    t = lax.broadcasted_iota(jnp.int32, u.shape, 0) & (DEC_SEQ - 1)
    st = st_ref[...]
    u1 = jnp.where(t == 0, pltpu.roll(st, rows - 1, 0), pltpu.roll(u, 1, 0))
    u2 = jnp.where(t < 2, st, pltpu.roll(u, 2, 0))
    mix = _gated_merge(attn_ref[...], u, u1, u2, b_ref, ga_ref, gc_ref, cw_ref)
    mix_ref[...] = mix.astype(mix_ref.dtype)


def _sample_merge(attn_s, rest, st_rows, conv_w):
    tm = WINDOW
    off = N_P // tm
    blk = lambda c: pl.BlockSpec((tm, D_MODEL), lambda i, c=c: (i + off, c))
    return pl.pallas_call(
        _sample_merge_kernel,
        grid=(N_S // tm,),
        in_specs=[pl.BlockSpec((tm, D_ATTN), lambda i: (i, 0)),
                  blk(0), blk(1), blk(2), blk(3), blk(4),
                  pl.BlockSpec((tm, D_CONV), lambda i: (i, 0)),
                  pl.BlockSpec((CONV_WIDTH, D_CONV), lambda i: (0, 0))],
        out_specs=[pl.BlockSpec((tm, D_MODEL), lambda i: (i, 0)),
                   pl.BlockSpec((tm, D_CONV), lambda i: (i, 0))],
        out_shape=[jax.ShapeDtypeStruct((N_S, D_MODEL), BF16),
                   jax.ShapeDtypeStruct((N_S, D_CONV), F32)],
        compiler_params=_params(1),
        name="sample_merge",
    )(attn_s, rest, rest, rest, rest, rest, st_rows, conv_w)


def _out_proj_kernel(mp_ref, ms_ref, w_ref, xp_ref, xs_ref, gf_ref, y_ref, yn_ref):
    is_prompt = pl.program_id(0) < N_P // ROW_TILE_SMALL
    res = jnp.where(is_prompt, xp_ref[...], xs_ref[...])
    mix = jnp.where(is_prompt, mp_ref[...], ms_ref[...])
    y = res + jnp.dot(mix, w_ref[...], preferred_element_type=F32)
    y_ref[...] = y
    yn_ref[...] = _rmsnorm_rows(y, gf_ref[...]).astype(yn_ref.dtype)


def _out_proj(mix_p, mix_s, w_out_bf, x_p, x_s, g_ffn):
    tm = ROW_TILE_SMALL
    last_p = N_P // tm - 1
    once = pl.Buffered(1)
    return pl.pallas_call(
        _out_proj_kernel,
        grid=(N_ALL // tm,),
        in_specs=[pl.BlockSpec((tm, D_ATTN), lambda i: (jnp.minimum(i, last_p), 0)),
                  pl.BlockSpec((tm, D_ATTN), lambda i: (0, 0), pipeline_mode=once),
                  pl.BlockSpec((D_ATTN, D_MODEL), lambda i: (0, 0), pipeline_mode=once),
                  pl.BlockSpec((tm, D_MODEL), lambda i: (jnp.minimum(i, last_p), 0)),
                  pl.BlockSpec((tm, D_MODEL), lambda i: (0, 0), pipeline_mode=once),
                  pl.BlockSpec((1, D_MODEL), lambda i: (0, 0))],
        out_specs=[pl.BlockSpec((tm, D_MODEL), lambda i: (i, 0)),
                   pl.BlockSpec((tm, D_MODEL), lambda i: (i, 0))],
        out_shape=[jax.ShapeDtypeStruct((N_ALL, D_MODEL), F32),
                   jax.ShapeDtypeStruct((N_ALL, D_MODEL), BF16)],
        compiler_params=_params(1),
        name="out_proj",
    )(mix_p, mix_s, w_out_bf, x_p, x_s, g_ffn)


def _ffn_up_kernel(x_ref, wg_ref, wu_ref, wd_ref, o_ref, wd_bf_ref, wg_bf, wu_bf):
    @pl.when(pl.program_id(1) == 0)
    def _():
        wg_bf[...] = wg_ref[...].astype(BF16)
        wu_bf[...] = wu_ref[...].astype(BF16)

    wd_bf_ref[...] = wd_ref[...].astype(BF16)

    piece = o_ref.shape[0] // FFN_UP_ROW_PIECES
    for c0 in range(0, o_ref.shape[1], MXU_WIDTH):
        cols = slice(c0, c0 + MXU_WIDTH)
        for r0 in range(0, o_ref.shape[0], piece):
            rows = slice(r0, r0 + piece)
            g = jnp.dot(x_ref[rows, :], wg_bf[:, cols], preferred_element_type=F32)
            u = jnp.dot(x_ref[rows, :], wu_bf[:, cols], preferred_element_type=F32)
            o_ref[rows, cols] = ((g * jax.nn.sigmoid(g)) * u).astype(o_ref.dtype)


def _ffn_up(yn, w_gate_up, w_down):
    tm = ROW_TILE_HUGE
    tn = 512
    nt = D_FF // tn
    n_row = N_ALL // tm
    wd_rows = D_FF // (nt * n_row)
    wd_spec = pl.BlockSpec((wd_rows, D_MODEL), lambda j, i: (j * n_row + i, 0))
    return pl.pallas_call(
        _ffn_up_kernel,
        grid=(nt, n_row),
        in_specs=[pl.BlockSpec((tm, D_MODEL), lambda j, i: (i, 0)),
                  pl.BlockSpec((D_MODEL, tn), lambda j, i: (0, j)),
                  pl.BlockSpec((D_MODEL, tn), lambda j, i: (0, j + nt)),
                  wd_spec],
        out_specs=[pl.BlockSpec((tm, tn), lambda j, i: (i, j)), wd_spec],
        out_shape=[jax.ShapeDtypeStruct((N_ALL, D_FF), BF16),
                   jax.ShapeDtypeStruct((D_FF, D_MODEL), BF16)],
        scratch_shapes=[pltpu.VMEM((D_MODEL, tn), BF16), pltpu.VMEM((D_MODEL, tn), BF16)],
        compiler_params=_params(2),
        name="ffn_up",
    )(yn, w_gate_up, w_gate_up, w_down)


def _ffn_down_kernel(a_ref, w_ref, r_ref, yp_ref, ys_ref):
    i = pl.program_id(0)
    n_prompt_tiles = N_P // FFN_DOWN_ROWS
    def project_into(y_ref):
        for c0 in range(0, D_MODEL, FFN_DOWN_CHUNK):
            cols = slice(c0, c0 + FFN_DOWN_CHUNK)
            y_ref[:, cols] = r_ref[:, cols] + jnp.dot(a_ref[...], w_ref[:, cols],
                                                      preferred_element_type=F32)

    @pl.when(i < n_prompt_tiles)
    def _():
        project_into(yp_ref)

    @pl.when(i >= n_prompt_tiles)
    def _():
        project_into(ys_ref)


def _ffn_down(act, w_down_bf, y1):
    tm = FFN_DOWN_ROWS
    n_p = N_P // tm
    return pl.pallas_call(
        _ffn_down_kernel,
        grid=(N_ALL // tm,),
        in_specs=[pl.BlockSpec((tm, D_FF), lambda i: (i, 0)),
                  pl.BlockSpec((D_FF, D_MODEL), lambda i: (0, 0), pipeline_mode=pl.Buffered(1)),
                  pl.BlockSpec((tm, D_MODEL), lambda i: (i, 0))],
        out_specs=[pl.BlockSpec((tm, D_MODEL), lambda i: (jnp.minimum(i, n_p - 1), 0)),
                   pl.BlockSpec((tm, D_MODEL), lambda i: (jnp.maximum(i - n_p, 0), 0))],
        out_shape=[jax.ShapeDtypeStruct((N_P, D_MODEL), F32),
                   jax.ShapeDtypeStruct((N_S, D_MODEL), F32)],
        compiler_params=_params(1),
        name="ffn_down",
    )(act, w_down_bf, y1)


def _rope_tables():
    inv = ROPE_THETA ** (-jnp.arange(0, HEAD_DIM, 2, dtype=F32) / HEAD_DIM)
    inv_t = jnp.tile(inv, LANES // (HEAD_DIM // 2))[None, :]
    sign = jnp.asarray(np.tile(np.repeat([-1.0, 1.0], HEAD_DIM // 2), LANES // HEAD_DIM), F32)[None, :]
    ang_a = (jnp.arange(N_P // LANES, dtype=jnp.int32) * LANES).astype(F32)[:, None] * inv_t
    ang_b = jnp.arange(LANES, dtype=jnp.int32).astype(F32)[:, None] * inv_t
    ca, sa = jnp.cos(ang_a)[:, None, :], jnp.sin(ang_a)[:, None, :]
    cb, sb = jnp.cos(ang_b)[None, :, :], jnp.sin(ang_b)[None, :, :]
    cos_p = (ca * cb - sa * sb).reshape(N_P, LANES)
    sin_p = (sa * cb + ca * sb).reshape(N_P, LANES)
    ang_s = (PAST_LEN + jnp.arange(DEC_SEQ, dtype=jnp.int32)).astype(F32)[:, None] * inv_t
    cos_s = jnp.tile(jnp.cos(ang_s), (DEC_BATCH, 1))
    sin_s = jnp.tile(jnp.sin(ang_s), (DEC_BATCH, 1))
    return (jnp.concatenate([cos_p, cos_s], axis=0),
            jnp.concatenate([sin_p, sin_s], axis=0) * sign)


def kernel(x_prompt, x_sample, cache_k_win, cache_v_win, state_conv, norm_mix, w_in, q_norm, k_norm,
           sinks, conv_w, w_out, norm_ffn, w_gate_up, w_down):
    assert x_prompt.shape == (1, SEQ, D_MODEL) and x_sample.shape == (DEC_BATCH, DEC_SEQ, D_MODEL)
    assert w_in.shape == (1, D_MODEL, D_IN)
    assert cache_k_win.shape == (1, DEC_BATCH, WINDOW, N_KV_HEADS, HEAD_DIM)

    x_p = x_prompt.reshape(N_P, D_MODEL)
    x_s = x_sample.reshape(N_S, D_MODEL)
    w_in2 = w_in.reshape(D_MODEL, D_IN)
    w_out2 = w_out.reshape(D_ATTN, D_MODEL)
    w_gu2 = w_gate_up.reshape(D_MODEL, 2 * D_FF)
    w_dn2 = w_down.reshape(D_FF, D_MODEL)
    conv_w2 = conv_w.reshape(CONV_WIDTH, D_CONV)
    cache_kt = jnp.transpose(cache_k_win.reshape(DEC_BATCH, WINDOW, D_KV), (0, 2, 1))
    cache_vt = jnp.transpose(cache_v_win.reshape(DEC_BATCH, WINDOW, D_KV), (0, 2, 1))

    cos_t, sin_t = _rope_tables()
    q_g128 = jnp.tile(q_norm.reshape(1, HEAD_DIM) * Q_SCALE, (1, LANES // HEAD_DIM))
    k_g128 = jnp.tile(k_norm.reshape(1, HEAD_DIM), (1, LANES // HEAD_DIM))
    rest_scale = jnp.asarray(np.repeat([1.0, 1.0, 1.0, 0.5, 0.5], D_MODEL)[None, :], F32)
    head_of_lane = np.arange(MXU_WIDTH) // HEAD_DIM
    bd = jnp.asarray(np.where(head_of_lane[:, None] == head_of_lane[None, :], 1.0 / HEAD_DIM, 0.0),
                     dtype=BF16)
    sinks2 = sinks.reshape(1, N_HEADS).astype(F32)
    sink_col = jnp.broadcast_to(jnp.tile(sinks2.reshape(N_HEADS) * LOG2E, DEC_SEQ)[:, None],
                                (DEC_SEQ * N_HEADS, LANES))
    rep = jnp.asarray(np.tile(np.eye(HEAD_DIM), (1, N_KV_HEADS)), dtype=BF16)
    st_rows = jnp.pad(state_conv.reshape(DEC_BATCH, CONV_WIDTH - 1, D_CONV),
                      ((0, 0), (0, DEC_SEQ - (CONV_WIDTH - 1)), (0, 0))).reshape(N_S, D_CONV)

    xn, kv = _norm_proj_kv(x_p, x_s, norm_mix.reshape(1, D_MODEL), w_in2, k_g128, cos_t, sin_t, bd)
    q, rest, w_out_bf = _proj_q_rest(xn, w_in2, q_g128, cos_t, sin_t, bd, rest_scale, w_out2)

    mix_p, tail = _prompt_mix(sinks2, q, kv, rest, conv_w2)
    q_s = q[N_P:].reshape(DEC_BATCH, DEC_SEQ * N_HEADS, HEAD_DIM)
    attn_s, kwt, vwt = _sample_attn(q_s, kv, cache_kt, cache_vt, sink_col, rep)
    mix_s, u_s = _sample_merge(attn_s.reshape(N_S, D_ATTN), rest, st_rows, conv_w2)

    y1, yn = _out_proj(mix_p, mix_s, w_out_bf, x_p, x_s, norm_ffn.reshape(1, D_MODEL))
    act, w_down_bf = _ffn_up(yn, w_gu2, w_dn2)
    y_p, y_s = _ffn_down(act, w_down_bf, y1)

    kv_tail = kv[N_P - WINDOW:N_P]
    win_shape = (1, DEC_BATCH, WINDOW, N_KV_HEADS, HEAD_DIM)
    return (y_p.reshape(1, SEQ, D_MODEL),
            y_s.reshape(DEC_BATCH, DEC_SEQ, D_MODEL),
            kv_tail[:, :D_KV].reshape(1, 1, WINDOW, N_KV_HEADS, HEAD_DIM),
            kv_tail[:, D_KV:].reshape(1, 1, WINDOW, N_KV_HEADS, HEAD_DIM),
            tail[8 - (CONV_WIDTH - 1):].reshape(1, 1, CONV_WIDTH - 1, D_CONV),
            jnp.transpose(kwt, (0, 2, 1)).reshape(win_shape),
            jnp.transpose(vwt, (0, 2, 1)).reshape(win_shape),
            u_s.reshape(DEC_BATCH, DEC_SEQ, D_CONV)[:, DEC_SEQ - (CONV_WIDTH - 1):][None])
```

```python
import math

import numpy as np
import jax
import jax.numpy as jnp
from jax import lax
from jax.experimental import pallas as pl
from jax.experimental.pallas import tpu as pltpu

F32 = jnp.float32
BF16 = jnp.bfloat16

D_MODEL = 2048
SEQ = 8192
DEC_BATCH = 128
DEC_SEQ = 4
PAST_LEN = 8192
N_HEADS = 32
HEAD_DIM = 64
N_KV_HEADS = 4
GROUP = N_HEADS // N_KV_HEADS
D_ATTN = N_HEADS * HEAD_DIM
D_KV = N_KV_HEADS * HEAD_DIM
WINDOW = 128
D_CONV = D_MODEL
CONV_WIDTH = 3
D_FF = 5632
ROPE_THETA = 10000.0
EPS = 1e-6
NEG_INF = -1e30
D_IN = 2 * D_ATTN + 2 * D_KV + 3 * D_CONV + D_CONV
D_REST = 5 * D_MODEL
LOG2E = math.log2(math.e)
Q_SCALE = HEAD_DIM ** -0.5 * LOG2E

N_P = SEQ
N_S = DEC_BATCH * DEC_SEQ
N_ALL = N_P + N_S

LANES = 128
VMEM_LIMIT = 56 * 1024 * 1024

ROW_TILE_SMALL = 512
ROW_TILE_BIG = 1088
ROW_TILE_HUGE = 2176
MXU_WIDTH = 256
PROJ_Q_ROW_PIECES = 4
FFN_UP_ROW_PIECES = 4
FFN_DOWN_ROWS = 256
FFN_DOWN_CHUNK = 512
MIX_BLOCKS = 4
MERGE_CHUNK = (32, 256)
SEQ_BLOCK = 8
PAIRS_PER_GROUP = GROUP // 2


def _params(n_axes):
    return pltpu.CompilerParams(
        dimension_semantics=("arbitrary",) * n_axes, vmem_limit_bytes=VMEM_LIMIT)


def _rmsnorm_rows(x, g):
    ms = jnp.mean(x * x, axis=-1, keepdims=True)
    return x * lax.rsqrt(ms + EPS) * g


def _cast_weight_once(w_ref, wbf_ref, axis):
    @pl.when(pl.program_id(axis) == 0)
    def _():
        wbf_ref[...] = w_ref[...].astype(BF16)


def _headnorm_rope(z, g128, cos_t, sin_t, bd):
    ms = jnp.dot((z * z).astype(BF16), bd, preferred_element_type=F32)
    scale = lax.rsqrt(ms + EPS)
    lane = lax.broadcasted_iota(jnp.int32, (z.shape[0], LANES), 1)
    first_half = (lane & (HEAD_DIM // 2)) == 0
    halves = []
    for c in range(0, MXU_WIDTH, LANES):
        y = z[:, c:c + LANES] * scale[:, c:c + LANES] * g128
        partner = jnp.where(first_half,
                            pltpu.roll(y, LANES - HEAD_DIM // 2, 1),
                            pltpu.roll(y, HEAD_DIM // 2, 1))
        halves.append(y * cos_t + partner * sin_t)
    return halves


def _norm_proj_kv_kernel(xp_ref, xs_ref, gm_ref, w_ref, g_ref, cos_ref, sin_ref, bd_ref,
                         xn_ref, kv_ref, wbf_ref):
    _cast_weight_once(w_ref, wbf_ref, 0)
    is_prompt = pl.program_id(0) < N_P // ROW_TILE_SMALL
    g128 = g_ref[...]
    bd = bd_ref[...]
    piece = xn_ref.shape[0] // PROJ_Q_ROW_PIECES
    for r0 in range(0, xn_ref.shape[0], piece):
        rows = slice(r0, r0 + piece)
        x = jnp.where(is_prompt, xp_ref[rows, :], xs_ref[rows, :])
        xn = _rmsnorm_rows(x, gm_ref[...]).astype(BF16)
        xn_ref[rows, :] = xn
        z = jnp.dot(xn, wbf_ref[...], preferred_element_type=F32)
        k_low, k_high = _headnorm_rope(z[:, :D_KV], g128, cos_ref[rows, :], sin_ref[rows, :], bd)
        kv_ref[rows, 0:LANES] = k_low
        kv_ref[rows, LANES:D_KV] = k_high
        kv_ref[rows, D_KV:] = z[:, D_KV:]


def _norm_proj_kv(x_p, x_s, g_mix, w_in, k_g128, cos_t, sin_t, bd):
    tm = ROW_TILE_SMALL
    tn = 2 * D_KV
    last_p = N_P // tm - 1
    const = lambda shape: pl.BlockSpec(shape, lambda i: (0, 0))
    return pl.pallas_call(
        _norm_proj_kv_kernel,
        grid=(N_ALL // tm,),
        in_specs=[pl.BlockSpec((tm, D_MODEL), lambda i: (jnp.minimum(i, last_p), 0)),
                  const((tm, D_MODEL)),
                  const((1, D_MODEL)),
                  pl.BlockSpec((D_MODEL, tn), lambda i: (0, D_ATTN // tn)),
                  const((1, LANES)),
                  pl.BlockSpec((tm, LANES), lambda i: (i, 0)),
                  pl.BlockSpec((tm, LANES), lambda i: (i, 0)),
                  const((MXU_WIDTH, MXU_WIDTH))],
        out_specs=[pl.BlockSpec((tm, D_MODEL), lambda i: (i, 0)),
                   pl.BlockSpec((tm, tn), lambda i: (i, 0))],
        out_shape=[jax.ShapeDtypeStruct((N_ALL, D_MODEL), BF16),
                   jax.ShapeDtypeStruct((N_ALL, tn), F32)],
        scratch_shapes=[pltpu.VMEM((D_MODEL, tn), BF16)],
        compiler_params=_params(1),
        name="norm_proj_kv",
    )(x_p, x_s, g_mix, w_in, k_g128, cos_t, sin_t, bd)


def _proj_rest_kernel(x_ref, w_ref, scale_ref, o_ref, wbf_ref):
    _cast_weight_once(w_ref, wbf_ref, 1)
    for c0 in range(0, o_ref.shape[1], MXU_WIDTH):
        cols = slice(c0, c0 + MXU_WIDTH)
        z = jnp.dot(x_ref[...], wbf_ref[:, cols], preferred_element_type=F32)
        o_ref[:, cols] = (z * scale_ref[:, cols]).astype(o_ref.dtype)


def _proj_q_kernel(x_ref, w_ref, g_ref, cos_ref, sin_ref, bd_ref, wo_ref, o_ref, wo_bf_ref, wbf_ref):
    _cast_weight_once(w_ref, wbf_ref, 1)
    wo_bf_ref[...] = wo_ref[...].astype(BF16)
    g128 = g_ref[...]
    bd = bd_ref[...]
    piece = o_ref.shape[0] // PROJ_Q_ROW_PIECES
    for r0 in range(0, o_ref.shape[0], piece):
        rows = slice(r0, r0 + piece)
        z_all = jnp.dot(x_ref[rows, :], wbf_ref[...], preferred_element_type=F32)
        cos_t = cos_ref[rows, :]
        sin_t = sin_ref[rows, :]
        for c0 in range(0, o_ref.shape[1], MXU_WIDTH):
            low, high = _headnorm_rope(z_all[:, c0:c0 + MXU_WIDTH], g128, cos_t, sin_t, bd)
            o_ref[rows, c0:c0 + LANES] = low.astype(o_ref.dtype)
            o_ref[rows, c0 + LANES:c0 + MXU_WIDTH] = high.astype(o_ref.dtype)


def _proj_q_rest(xn, w_in, q_g128, cos_q, sin_q, bd, rest_scale, w_out):
    tm = ROW_TILE_BIG
    n_row = N_ALL // tm
    k = D_MODEL
    x_spec = pl.BlockSpec((tm, k), lambda j, i: (i, 0))
    rope_specs = [pl.BlockSpec((1, LANES), lambda j, i: (0, 0)),
                  pl.BlockSpec((tm, LANES), lambda j, i: (i, 0)),
                  pl.BlockSpec((tm, LANES), lambda j, i: (i, 0)),
                  pl.BlockSpec((MXU_WIDTH, MXU_WIDTH), lambda j, i: (0, 0))]

    tn_q = 1024
    n_col_q = D_ATTN // tn_q
    wo_rows = D_ATTN // (n_col_q * n_row)
    wo_spec = pl.BlockSpec((wo_rows, D_MODEL), lambda j, i: (j * n_row + i, 0))
    q, w_out_bf = pl.pallas_call(
        _proj_q_kernel,
        grid=(n_col_q, n_row),
        in_specs=[x_spec, pl.BlockSpec((k, tn_q), lambda j, i: (0, j))] + rope_specs + [wo_spec],
        out_specs=[pl.BlockSpec((tm, tn_q), lambda j, i: (i, j)), wo_spec],
        out_shape=[jax.ShapeDtypeStruct((N_ALL, D_ATTN), BF16),
                   jax.ShapeDtypeStruct((D_ATTN, D_MODEL), BF16)],
        scratch_shapes=[pltpu.VMEM((k, tn_q), BF16)],
        compiler_params=_params(2),
        name="proj_q",
    )(xn, w_in, q_g128, cos_q, sin_q, bd, w_out)

    tm_r = ROW_TILE_HUGE
    tn_r = 1024
    rest_start = D_ATTN + 2 * D_KV
    rest = pl.pallas_call(
        _proj_rest_kernel,
        grid=(D_REST // tn_r, N_ALL // tm_r),
        in_specs=[pl.BlockSpec((tm_r, k), lambda j, i: (i, 0)),
                  pl.BlockSpec((pl.Element(k), pl.Element(tn_r)),
                               lambda j, i: (0, pl.multiple_of(rest_start + j * tn_r, LANES))),
                  pl.BlockSpec((1, tn_r), lambda j, i: (0, j))],
        out_specs=pl.BlockSpec((tm_r, tn_r), lambda j, i: (i, j)),
        out_shape=jax.ShapeDtypeStruct((N_ALL, D_REST), BF16),
        scratch_shapes=[pltpu.VMEM((k, tn_r), BF16)],
        compiler_params=_params(2),
        name="proj_rest",
    )(xn, w_in, rest_scale)
    return q, rest, w_out_bf


def _gated_merge(attn_half, u, u1, u2, b_ref, ga_half_ref, gc_half_ref, cw_ref):
    return _gated_merge_values(attn_half, u, u1, u2, b_ref[...].astype(F32),
                               ga_half_ref[...], gc_half_ref[...], 0.5 * cw_ref[...])


def _gated_merge_values(attn_half, u, u1, u2, b, ga_half, gc_half, cw_half):
    conv_half = cw_half[0:1, :] * u2 + cw_half[1:2, :] * u1 + cw_half[2:3, :] * u
    bc = (b * conv_half).astype(BF16)
    a = attn_half.astype(BF16)
    ta = jnp.tanh(ga_half)
    tc = jnp.tanh(gc_half)
    return (ta * a + a) + (tc * bc + bc)


def _prompt_attention_block(j, sinks_ref, q_ref, kv_prev, kv_cur, attn_ref, row0):
    n_keys = 2 * WINDOW
    q_rows = slice(row0, row0 + WINDOW)
    kv = jnp.concatenate([kv_prev, kv_cur], axis=0)
    key_row = lax.broadcasted_iota(jnp.int32, (n_keys, LANES), 0)
    low = lax.broadcasted_iota(jnp.int32, (n_keys, LANES), 1) < HEAD_DIM
    low_v = low & (key_row > 0)
    high_v = jnp.logical_not(low) & (key_row > 0)
    ones_low = jnp.where(low, 2.0, 0.0)
    ones_high = jnp.where(low, 0.0, 2.0)

    r = lax.broadcasted_iota(jnp.int32, (WINDOW, n_keys), 0)
    c = lax.broadcasted_iota(jnp.int32, (WINDOW, n_keys), 1)
    first_key = jnp.where(j > 0, 0, WINDOW)
    mask = jnp.where((c > r) & (c <= r + WINDOW) & (c >= first_key), 0.0, NEG_INF)
    mask = jnp.where(c == 0, 0.0, mask).astype(BF16)
    mask2 = jnp.concatenate([mask, mask], axis=1)
    eye = (lax.broadcasted_iota(jnp.int32, (WINDOW, WINDOW), 0)
           == lax.broadcasted_iota(jnp.int32, (WINDOW, WINDOW), 1))
    eye = jnp.where(eye, 1.0, 0.0).astype(BF16)
    sink_lane = lax.broadcasted_iota(jnp.int32, (1, LANES), 1) == 0
    not_slot0 = lax.broadcasted_iota(jnp.int32, (HEAD_DIM, n_keys), 1) > 0
    no_keys = jnp.zeros((HEAD_DIM, n_keys), F32)

    for g in range(N_KV_HEADS):
        chunk = (g // 2) * LANES
        kc_t = kv[:, chunk:chunk + LANES].T
        kg_t = kc_t[(g % 2) * HEAD_DIM:(g % 2 + 1) * HEAD_DIM]
        kg_t = jnp.where(not_slot0, kg_t, 0.0)
        vc = kv[:, D_KV + chunk:D_KV + chunk + LANES]
        if g % 2 == 0:
            v_low = jnp.where(low_v, vc, 0.0)
            v_high = pltpu.roll(v_low, HEAD_DIM, 1)
        else:
            v_high = jnp.where(high_v, vc, 0.0)
            v_low = pltpu.roll(v_high, HEAD_DIM, 1)
        k_bd_t = jnp.concatenate(
            [jnp.concatenate([kg_t, no_keys], axis=1),
             jnp.concatenate([no_keys, kg_t], axis=1)], axis=0).astype(BF16)
        k_aug = jnp.concatenate([k_bd_t, mask2], axis=0)
        v_bd = jnp.concatenate(
            [jnp.concatenate([v_low, ones_low], axis=1),
             jnp.concatenate([v_high, ones_high], axis=1)], axis=0).astype(BF16)

        for k in range(PAIRS_PER_GROUP):
            pair = g * PAIRS_PER_GROUP + k
            cols = slice(pair * LANES, (pair + 1) * LANES)
            q_aug = jnp.concatenate([q_ref[q_rows, cols], eye], axis=1)
            s = jnp.dot(q_aug, k_aug, preferred_element_type=F32)
            halves = []
            for half in range(2):
                sh = s[:, half * n_keys:(half + 1) * n_keys]
                sink = sinks_ref[0, 2 * pair + half] * LOG2E
                sink_row = jnp.where(sink_lane, sink, 0.0)
                sh = jnp.concatenate([sh[:, :LANES] + sink_row, sh[:, LANES:]], axis=1)
                halves.append(jnp.exp2(sh - jnp.max(sh, axis=1, keepdims=True)))
            p = jnp.concatenate(halves, axis=1).astype(BF16)
            o = jnp.dot(p, v_bd, preferred_element_type=F32)
            attn_ref[q_rows, cols] = (o[:, :LANES] / o[:, LANES:]).astype(attn_ref.dtype)


def _prompt_mix_kernel(sinks_ref, q_ref, kvp_ref, kvc_ref, rest_ref, cw_ref,
                       mix_ref, tail_ref, u_scr, attn_scr):
    j = pl.program_id(0)
    h_off, b_off, c_off, ga_off, gc_off = (n * D_MODEL for n in range(5))

    @pl.when(j == 0)
    def _():
        u_scr[0:8, :] = jnp.zeros((8, D_CONV), F32)

    tm = MIX_BLOCKS * WINDOW
    for sb in range(MIX_BLOCKS):
        kv_prev = kvp_ref[...] if sb == 0 else kvc_ref[(sb - 1) * WINDOW:sb * WINDOW, :]
        _prompt_attention_block(j * MIX_BLOCKS + sb, sinks_ref, q_ref, kv_prev,
                                kvc_ref[sb * WINDOW:(sb + 1) * WINDOW, :], attn_scr, sb * WINDOW)

    rb, cb = MERGE_CHUNK
    cw_half = 0.5 * cw_ref[...]
    for r0 in range(0, tm, rb):
        rows = slice(r0, r0 + rb)
        u_scr[8 + r0:8 + r0 + rb, :] = (rest_ref[rows, c_off:c_off + D_MODEL].astype(F32)
                                        * rest_ref[rows, h_off:h_off + D_MODEL].astype(F32))
        for c0 in range(0, D_MODEL, cb):
            cols = slice(c0, c0 + cb)
            x = u_scr[r0:r0 + rb + 8, cols]
            mix = _gated_merge_values(
                attn_scr[rows, cols], x[8:], pltpu.roll(x, 1, 0)[8:], pltpu.roll(x, 2, 0)[8:],
                rest_ref[rows, b_off + c0:b_off + c0 + cb].astype(F32),
                rest_ref[rows, ga_off + c0:ga_off + c0 + cb],
                rest_ref[rows, gc_off + c0:gc_off + c0 + cb], cw_half[:, cols])
            mix_ref[rows, cols] = mix.astype(mix_ref.dtype)
    tail = u_scr[tm:tm + 8, :]
    u_scr[0:8, :] = tail
    tail_ref[...] = tail


def _prompt_mix(sinks, q, kv, rest, conv_w):
    tm = MIX_BLOCKS * WINDOW
    return pl.pallas_call(
        _prompt_mix_kernel,
        grid=(N_P // tm,),
        in_specs=[pl.BlockSpec(memory_space=pltpu.SMEM),
                  pl.BlockSpec((tm, D_ATTN), lambda j: (j, 0)),
                  pl.BlockSpec((WINDOW, 2 * D_KV), lambda j: (jnp.maximum(j * MIX_BLOCKS - 1, 0), 0)),
                  pl.BlockSpec((tm, 2 * D_KV), lambda j: (j, 0)),
                  pl.BlockSpec((tm, D_REST), lambda j: (j, 0)),
                  pl.BlockSpec((CONV_WIDTH, D_CONV), lambda j: (0, 0))],
        out_specs=[pl.BlockSpec((tm, D_MODEL), lambda j: (j, 0)),
                   pl.BlockSpec((8, D_CONV), lambda j: (0, 0))],
        out_shape=[jax.ShapeDtypeStruct((N_P, D_MODEL), BF16),
                   jax.ShapeDtypeStruct((8, D_CONV), F32)],
        scratch_shapes=[pltpu.VMEM((8 + tm, D_CONV), F32), pltpu.VMEM((tm, D_ATTN), F32)],
        compiler_params=_params(1),
        name="prompt_mix",
    )(sinks, q, kv, kv, rest, conv_w)


def _sample_attn_kernel(q_ref, kvn_ref, ckt_ref, cvt_ref, sink_ref, rep_ref,
                        o_ref, kwt_ref, vwt_ref, bias_c, bias_n):
    n_new = SEQ_BLOCK * DEC_SEQ
    rows = DEC_SEQ * N_HEADS
    keep = WINDOW - DEC_SEQ

    @pl.when(pl.program_id(0) == 0)
    def _():
        row = lax.broadcasted_iota(jnp.int32, (rows, WINDOW), 0)
        col = lax.broadcasted_iota(jnp.int32, (rows, WINDOW), 1)
        t = row >> 5
        bias_c[...] = jnp.where(col > t, 0.0, NEG_INF)
        for s in range(SEQ_BLOCK):
            valid_new = (col < n_new) & ((col >> 2) == s) & ((col & 3) <= t)
            bias_n[s] = jnp.where(valid_new, 0.0, NEG_INF)

    kvn = jnp.concatenate([kvn_ref[...], jnp.zeros((WINDOW - n_new, 2 * D_KV), F32)], axis=0)
    kn_t = kvn[:, :D_KV].T
    vn_t = kvn[:, D_KV:].T
    kn_t_bf = kn_t.astype(BF16)
    vn_bf = kvn[:, D_KV:].astype(BF16)

    all_rows = SEQ_BLOCK * rows
    row2 = lax.broadcasted_iota(jnp.int32, (all_rows, D_KV), 0)
    col2 = lax.broadcasted_iota(jnp.int32, (all_rows, D_KV), 1)
    same_group = ((row2 & (N_HEADS - 1)) >> 3) == (col2 >> 6)
    win_lane = lax.broadcasted_iota(jnp.int32, (D_KV, WINDOW), 1)
    sink_b = jnp.concatenate([sink_ref[...]] * SEQ_BLOCK, axis=0)
    ones = jnp.ones((2 * WINDOW, LANES), BF16)

    q_all = q_ref[...].reshape(all_rows, HEAD_DIM)
    q4 = jnp.dot(q_all, rep_ref[...], preferred_element_type=F32)
    q4 = jnp.where(same_group, q4, 0.0).astype(BF16)
    s_n = jnp.dot(q4, kn_t_bf, preferred_element_type=F32) + bias_n[...].reshape(all_rows, WINDOW)
    s_c = jnp.concatenate(
        [jnp.dot(q4[s * rows:(s + 1) * rows], ckt_ref[s].astype(BF16), preferred_element_type=F32)
         + bias_c[...] for s in range(SEQ_BLOCK)], axis=0)
    m_b = jnp.maximum(jnp.max(jnp.maximum(s_c, s_n), axis=1, keepdims=True), sink_b)
    p_c = jnp.exp2(s_c - m_b).astype(BF16)
    p_n = jnp.exp2(s_n - m_b).astype(BF16)
    l_b = (jnp.dot(jnp.concatenate([p_c, p_n], axis=1), ones, preferred_element_type=F32)
           + jnp.exp2(sink_b - m_b))
    o4 = jnp.concatenate(
        [lax.dot_general(p_c[s * rows:(s + 1) * rows], cvt_ref[s].astype(BF16),
                         (((1,), (1,)), ((), ())), preferred_element_type=F32)
         for s in range(SEQ_BLOCK)], axis=0) + jnp.dot(p_n, vn_bf, preferred_element_type=F32)
    o4 = jnp.where(same_group, o4, 0.0)
    a = o4[:, 0:LANES] + o4[:, LANES:]
    o = (a + pltpu.roll(a, HEAD_DIM, 1)) / (2.0 * l_b)
    o_ref[...] = o[:, 0:HEAD_DIM].reshape(SEQ_BLOCK, rows, HEAD_DIM).astype(o_ref.dtype)

    for s in range(SEQ_BLOCK):
        k_t = ckt_ref[s]
        v_t = cvt_ref[s]
        new_shift = keep - s * DEC_SEQ
        kwt_ref[s] = jnp.where(win_lane >= keep, pltpu.roll(kn_t, new_shift, 1),
                               pltpu.roll(k_t, keep, 1))
        vwt_ref[s] = jnp.where(win_lane >= keep, pltpu.roll(vn_t, new_shift, 1),
                               pltpu.roll(v_t, keep, 1))


def _sample_attn(q_s, kv, cache_kt, cache_vt, sink_col, rep):
    rows = DEC_SEQ * N_HEADS
    n_new = SEQ_BLOCK * DEC_SEQ
    cache_spec = pl.BlockSpec((SEQ_BLOCK, D_KV, WINDOW), lambda i: (i, 0, 0))
    return pl.pallas_call(
        _sample_attn_kernel,
        grid=(DEC_BATCH // SEQ_BLOCK,),
        in_specs=[pl.BlockSpec((SEQ_BLOCK, rows, HEAD_DIM), lambda i: (i, 0, 0)),
                  pl.BlockSpec((n_new, 2 * D_KV), lambda i: (i + N_P // n_new, 0)),
                  cache_spec, cache_spec,
                  pl.BlockSpec((rows, LANES), lambda i: (0, 0)),
                  pl.BlockSpec((HEAD_DIM, D_KV), lambda i: (0, 0))],
        out_specs=[pl.BlockSpec((SEQ_BLOCK, rows, HEAD_DIM), lambda i: (i, 0, 0)),
                   cache_spec, cache_spec],
        out_shape=[jax.ShapeDtypeStruct((DEC_BATCH, rows, HEAD_DIM), BF16),
                   jax.ShapeDtypeStruct((DEC_BATCH, D_KV, WINDOW), F32),
                   jax.ShapeDtypeStruct((DEC_BATCH, D_KV, WINDOW), F32)],
        scratch_shapes=[pltpu.VMEM((rows, WINDOW), F32), pltpu.VMEM((SEQ_BLOCK, rows, WINDOW), F32)],
        compiler_params=_params(1),
        name="sample_attn",
    )(q_s, kv, cache_kt, cache_vt, sink_col, rep)


def _sample_merge_kernel(attn_ref, h_ref, b_ref, c_ref, ga_ref, gc_ref, st_ref, cw_ref,
                         mix_ref, state_a_ref, state_b_ref, u_scr):
    u = c_ref[...].astype(F32) * h_ref[...].astype(F32)
    rows = u.shape[0]
    n_seq = rows // DEC_SEQ
    for c in range(D_CONV // LANES):
        cols = slice(c * LANES, (c + 1) * LANES)
        u_scr[c] = u[:, cols]
        state_a_ref[:, cols] = u_scr[c, pl.ds(DEC_SEQ - 2, n_seq, stride=DEC_SEQ), :]
        state_b_ref[:, cols] = u_scr[c, pl.ds(DEC_SEQ - 1, n_seq, stride=DEC_SEQ), :]
    t = lax.broadcasted_iota(jnp.int32, u.shape, 0) & (DEC_SEQ - 1)
    st = st_ref[...]
    u1 = jnp.where(t == 0, pltpu.roll(st, rows - 1, 0), pltpu.roll(u, 1, 0))
    u2 = jnp.where(t < 2, st, pltpu.roll(u, 2, 0))
    mix = _gated_merge(attn_ref[...], u, u1, u2, b_ref, ga_ref, gc_ref, cw_ref)
    mix_ref[...] = mix.astype(mix_ref.dtype)


def _sample_merge(attn_s, rest, st_rows, conv_w):
    tm = 2 * WINDOW
    off = N_P // tm
    n_seq = tm // DEC_SEQ
    blk = lambda c: pl.BlockSpec((tm, D_MODEL), lambda i, c=c: (i + off, c))
    state_spec = pl.BlockSpec((n_seq, D_CONV), lambda i: (i, 0))
    return pl.pallas_call(
        _sample_merge_kernel,
        grid=(N_S // tm,),
        in_specs=[pl.BlockSpec((tm, D_ATTN), lambda i: (i, 0)),
                  blk(0), blk(1), blk(2), blk(3), blk(4),
                  pl.BlockSpec((tm, D_CONV), lambda i: (i, 0)),
                  pl.BlockSpec((CONV_WIDTH, D_CONV), lambda i: (0, 0))],
        out_specs=[pl.BlockSpec((tm, D_MODEL), lambda i: (i, 0)), state_spec, state_spec],
        out_shape=[jax.ShapeDtypeStruct((N_S, D_MODEL), BF16),
                   jax.ShapeDtypeStruct((DEC_BATCH, D_CONV), F32),
                   jax.ShapeDtypeStruct((DEC_BATCH, D_CONV), F32)],
        scratch_shapes=[pltpu.VMEM((D_CONV // LANES, tm, LANES), F32)],
        compiler_params=_params(1),
        name="sample_merge",
    )(attn_s, rest, rest, rest, rest, rest, st_rows, conv_w)


def _out_proj_kernel(mp_ref, ms_ref, w_ref, xp_ref, xs_ref, gf_ref, y_ref, yn_ref):
    is_prompt = pl.program_id(0) < N_P // ROW_TILE_SMALL
    res = jnp.where(is_prompt, xp_ref[...], xs_ref[...])
    mix = jnp.where(is_prompt, mp_ref[...], ms_ref[...])
    y = res + jnp.dot(mix, w_ref[...], preferred_element_type=F32)
    y_ref[...] = y
    yn_ref[...] = _rmsnorm_rows(y, gf_ref[...]).astype(yn_ref.dtype)


def _out_proj(mix_p, mix_s, w_out_bf, x_p, x_s, g_ffn):
    tm = ROW_TILE_SMALL
    last_p = N_P // tm - 1
    once = pl.Buffered(1)
    return pl.pallas_call(
        _out_proj_kernel,
        grid=(N_ALL // tm,),
        in_specs=[pl.BlockSpec((tm, D_ATTN), lambda i: (jnp.minimum(i, last_p), 0)),
                  pl.BlockSpec((tm, D_ATTN), lambda i: (0, 0), pipeline_mode=once),
                  pl.BlockSpec((D_ATTN, D_MODEL), lambda i: (0, 0), pipeline_mode=once),
                  pl.BlockSpec((tm, D_MODEL), lambda i: (jnp.minimum(i, last_p), 0)),
                  pl.BlockSpec((tm, D_MODEL), lambda i: (0, 0), pipeline_mode=once),
                  pl.BlockSpec((1, D_MODEL), lambda i: (0, 0))],
        out_specs=[pl.BlockSpec((tm, D_MODEL), lambda i: (i, 0)),
                   pl.BlockSpec((tm, D_MODEL), lambda i: (i, 0))],
        out_shape=[jax.ShapeDtypeStruct((N_ALL, D_MODEL), F32),
                   jax.ShapeDtypeStruct((N_ALL, D_MODEL), BF16)],
        compiler_params=_params(1),
        name="out_proj",
    )(mix_p, mix_s, w_out_bf, x_p, x_s, g_ffn)


def _ffn_up_kernel(x_ref, wg_ref, wu_ref, wd_ref, o_ref, wd_bf_ref, wg_bf, wu_bf):
    @pl.when(pl.program_id(1) == 0)
    def _():
        wg_bf[...] = wg_ref[...].astype(BF16)
        wu_bf[...] = wu_ref[...].astype(BF16)

    wd_bf_ref[...] = wd_ref[...].astype(BF16)

    piece = o_ref.shape[0] // FFN_UP_ROW_PIECES
    for c0 in range(0, o_ref.shape[1], MXU_WIDTH):
        cols = slice(c0, c0 + MXU_WIDTH)
        for r0 in range(0, o_ref.shape[0], piece):
            rows = slice(r0, r0 + piece)
            g = jnp.dot(x_ref[rows, :], wg_bf[:, cols], preferred_element_type=F32)
            u = jnp.dot(x_ref[rows, :], wu_bf[:, cols], preferred_element_type=F32)
            o_ref[rows, cols] = ((g * jax.nn.sigmoid(g)) * u).astype(o_ref.dtype)


def _ffn_up(yn, w_gate_up, w_down):
    tm = ROW_TILE_HUGE
    tn = 512
    nt = D_FF // tn
    n_row = N_ALL // tm
    wd_rows = D_FF // (nt * n_row)
    wd_spec = pl.BlockSpec((wd_rows, D_MODEL), lambda j, i: (j * n_row + i, 0))
    return pl.pallas_call(
        _ffn_up_kernel,
        grid=(nt, n_row),
        in_specs=[pl.BlockSpec((tm, D_MODEL), lambda j, i: (i, 0)),
                  pl.BlockSpec((D_MODEL, tn), lambda j, i: (0, j)),
                  pl.BlockSpec((D_MODEL, tn), lambda j, i: (0, j + nt)),
                  wd_spec],
        out_specs=[pl.BlockSpec((tm, tn), lambda j, i: (i, j)), wd_spec],
        out_shape=[jax.ShapeDtypeStruct((N_ALL, D_FF), BF16),
                   jax.ShapeDtypeStruct((D_FF, D_MODEL), BF16)],
        scratch_shapes=[pltpu.VMEM((D_MODEL, tn), BF16), pltpu.VMEM((D_MODEL, tn), BF16)],
        compiler_params=_params(2),
        name="ffn_up",
    )(yn, w_gate_up, w_gate_up, w_down)


def _ffn_down_kernel(a_ref, w_ref, r_ref, yp_ref, ys_ref):
    i = pl.program_id(0)
    n_prompt_tiles = N_P // FFN_DOWN_ROWS
    def project_into(y_ref):
        for c0 in range(0, D_MODEL, FFN_DOWN_CHUNK):
            cols = slice(c0, c0 + FFN_DOWN_CHUNK)
            y_ref[:, cols] = r_ref[:, cols] + jnp.dot(a_ref[...], w_ref[:, cols],
                                                      preferred_element_type=F32)

    @pl.when(i < n_prompt_tiles)
    def _():
        project_into(yp_ref)

    @pl.when(i >= n_prompt_tiles)
    def _():
        project_into(ys_ref)


def _ffn_down(act, w_down_bf, y1):
    tm = FFN_DOWN_ROWS
    n_p = N_P // tm
    return pl.pallas_call(
        _ffn_down_kernel,
        grid=(N_ALL // tm,),
        in_specs=[pl.BlockSpec((tm, D_FF), lambda i: (i, 0)),
                  pl.BlockSpec((D_FF, D_MODEL), lambda i: (0, 0), pipeline_mode=pl.Buffered(1)),
                  pl.BlockSpec((tm, D_MODEL), lambda i: (i, 0))],
        out_specs=[pl.BlockSpec((tm, D_MODEL), lambda i: (jnp.minimum(i, n_p - 1), 0)),
                   pl.BlockSpec((tm, D_MODEL), lambda i: (jnp.maximum(i - n_p, 0), 0))],
        out_shape=[jax.ShapeDtypeStruct((N_P, D_MODEL), F32),
                   jax.ShapeDtypeStruct((N_S, D_MODEL), F32)],
        compiler_params=_params(1),
        name="ffn_down",
    )(act, w_down_bf, y1)


def _rope_tables():
    inv = ROPE_THETA ** (-jnp.arange(0, HEAD_DIM, 2, dtype=F32) / HEAD_DIM)
    inv_t = jnp.tile(inv, LANES // (HEAD_DIM // 2))[None, :]
    sign = jnp.asarray(np.tile(np.repeat([-1.0, 1.0], HEAD_DIM // 2), LANES // HEAD_DIM), F32)[None, :]
    ang_a = (jnp.arange(N_P // LANES, dtype=jnp.int32) * LANES).astype(F32)[:, None] * inv_t
    ang_b = jnp.arange(LANES, dtype=jnp.int32).astype(F32)[:, None] * inv_t
    ca, sa = jnp.cos(ang_a)[:, None, :], jnp.sin(ang_a)[:, None, :]
    cb, sb = jnp.cos(ang_b)[None, :, :], jnp.sin(ang_b)[None, :, :]
    cos_p = (ca * cb - sa * sb).reshape(N_P, LANES)
    sin_p = (sa * cb + ca * sb).reshape(N_P, LANES)
    ang_s = (PAST_LEN + jnp.arange(DEC_SEQ, dtype=jnp.int32)).astype(F32)[:, None] * inv_t
    cos_s = jnp.tile(jnp.cos(ang_s), (DEC_BATCH, 1))
    sin_s = jnp.tile(jnp.sin(ang_s), (DEC_BATCH, 1))
    return (jnp.concatenate([cos_p, cos_s], axis=0),
            jnp.concatenate([sin_p, sin_s], axis=0) * sign)


def kernel(x_prompt, x_sample, cache_k_win, cache_v_win, state_conv, norm_mix, w_in, q_norm, k_norm,
           sinks, conv_w, w_out, norm_ffn, w_gate_up, w_down):
    assert x_prompt.shape == (1, SEQ, D_MODEL) and x_sample.shape == (DEC_BATCH, DEC_SEQ, D_MODEL)
    assert w_in.shape == (1, D_MODEL, D_IN)
    assert cache_k_win.shape == (1, DEC_BATCH, WINDOW, N_KV_HEADS, HEAD_DIM)

    x_p = x_prompt.reshape(N_P, D_MODEL)
    x_s = x_sample.reshape(N_S, D_MODEL)
    w_in2 = w_in.reshape(D_MODEL, D_IN)
    w_out2 = w_out.reshape(D_ATTN, D_MODEL)
    w_gu2 = w_gate_up.reshape(D_MODEL, 2 * D_FF)
    w_dn2 = w_down.reshape(D_FF, D_MODEL)
    conv_w2 = conv_w.reshape(CONV_WIDTH, D_CONV)
    cache_kt = jnp.transpose(cache_k_win.reshape(DEC_BATCH, WINDOW, D_KV), (0, 2, 1))
    cache_vt = jnp.transpose(cache_v_win.reshape(DEC_BATCH, WINDOW, D_KV), (0, 2, 1))

    cos_t, sin_t = _rope_tables()
    q_g128 = jnp.tile(q_norm.reshape(1, HEAD_DIM) * Q_SCALE, (1, LANES // HEAD_DIM))
    k_g128 = jnp.tile(k_norm.reshape(1, HEAD_DIM), (1, LANES // HEAD_DIM))
    rest_scale = jnp.asarray(np.repeat([1.0, 1.0, 1.0, 0.5, 0.5], D_MODEL)[None, :], F32)
    head_of_lane = np.arange(MXU_WIDTH) // HEAD_DIM
    bd = jnp.asarray(np.where(head_of_lane[:, None] == head_of_lane[None, :], 1.0 / HEAD_DIM, 0.0),
                     dtype=BF16)
    sinks2 = sinks.reshape(1, N_HEADS).astype(F32)
    sink_col = jnp.broadcast_to(jnp.tile(sinks2.reshape(N_HEADS) * LOG2E, DEC_SEQ)[:, None],
                                (DEC_SEQ * N_HEADS, LANES))
    rep = jnp.asarray(np.tile(np.eye(HEAD_DIM), (1, N_KV_HEADS)), dtype=BF16)
    st_rows = jnp.pad(state_conv.reshape(DEC_BATCH, CONV_WIDTH - 1, D_CONV),
                      ((0, 0), (0, DEC_SEQ - (CONV_WIDTH - 1)), (0, 0))).reshape(N_S, D_CONV)

    xn, kv = _norm_proj_kv(x_p, x_s, norm_mix.reshape(1, D_MODEL), w_in2, k_g128, cos_t, sin_t, bd)
    q, rest, w_out_bf = _proj_q_rest(xn, w_in2, q_g128, cos_t, sin_t, bd, rest_scale, w_out2)

    mix_p, tail = _prompt_mix(sinks2, q, kv, rest, conv_w2)
    q_s = q[N_P:].reshape(DEC_BATCH, DEC_SEQ * N_HEADS, HEAD_DIM)
    attn_s, kwt, vwt = _sample_attn(q_s, kv, cache_kt, cache_vt, sink_col, rep)
    mix_s, state_a, state_b = _sample_merge(attn_s.reshape(N_S, D_ATTN), rest, st_rows, conv_w2)

    y1, yn = _out_proj(mix_p, mix_s, w_out_bf, x_p, x_s, norm_ffn.reshape(1, D_MODEL))
    act, w_down_bf = _ffn_up(yn, w_gu2, w_dn2)
    y_p, y_s = _ffn_down(act, w_down_bf, y1)

    kv_tail = kv[N_P - WINDOW:N_P]
    win_shape = (1, DEC_BATCH, WINDOW, N_KV_HEADS, HEAD_DIM)
    return (y_p.reshape(1, SEQ, D_MODEL),
            y_s.reshape(DEC_BATCH, DEC_SEQ, D_MODEL),
            kv_tail[:, :D_KV].reshape(1, 1, WINDOW, N_KV_HEADS, HEAD_DIM),
            kv_tail[:, D_KV:].reshape(1, 1, WINDOW, N_KV_HEADS, HEAD_DIM),
            tail[8 - (CONV_WIDTH - 1):].reshape(1, 1, CONV_WIDTH - 1, D_CONV),
            jnp.transpose(kwt, (0, 2, 1)).reshape(win_shape),
            jnp.transpose(vwt, (0, 2, 1)).reshape(win_shape),
            jnp.stack([state_a, state_b], axis=1)[None])
```

```python
import math

import numpy as np
import jax
import jax.numpy as jnp
from jax import lax
from jax.experimental import pallas as pl
from jax.experimental.pallas import tpu as pltpu

F32 = jnp.float32
BF16 = jnp.bfloat16

D_MODEL = 2048
SEQ = 8192
DEC_BATCH = 128
DEC_SEQ = 4
PAST_LEN = 8192
N_HEADS = 32
HEAD_DIM = 64
N_KV_HEADS = 4
GROUP = N_HEADS // N_KV_HEADS
D_ATTN = N_HEADS * HEAD_DIM
D_KV = N_KV_HEADS * HEAD_DIM
WINDOW = 128
D_CONV = D_MODEL
CONV_WIDTH = 3
D_FF = 5632
ROPE_THETA = 10000.0
EPS = 1e-6
NEG_INF = -1e30
D_IN = 2 * D_ATTN + 2 * D_KV + 3 * D_CONV + D_CONV
D_REST = 5 * D_MODEL
LOG2E = math.log2(math.e)
Q_SCALE = HEAD_DIM ** -0.5 * LOG2E

N_P = SEQ
N_S = DEC_BATCH * DEC_SEQ
N_ALL = N_P + N_S

LANES = 128
VMEM_LIMIT = 56 * 1024 * 1024

ROW_TILE_SMALL = 512
ROW_TILE_BIG = 1088
ROW_TILE_HUGE = 2176
MXU_WIDTH = 256
PROJ_Q_ROW_PIECES = 4
FFN_UP_ROW_PIECES = 4
FFN_DOWN_ROWS = 256
FFN_DOWN_CHUNK = 512
MIX_BLOCKS = 4
MERGE_CHUNK = (32, 256)
SEQ_BLOCK = 8
PAIRS_PER_GROUP = GROUP // 2


def _params(n_axes):
    return pltpu.CompilerParams(
        dimension_semantics=("arbitrary",) * n_axes, vmem_limit_bytes=VMEM_LIMIT)


def _rmsnorm_rows(x, g):
    ms = jnp.mean(x * x, axis=-1, keepdims=True)
    return x * lax.rsqrt(ms + EPS) * g


def _cast_weight_once(w_ref, wbf_ref, axis):
    @pl.when(pl.program_id(axis) == 0)
    def _():
        wbf_ref[...] = w_ref[...].astype(BF16)


def _headnorm_rope(z, g128, cos_t, sin_t, bd):
    ms = jnp.dot((z * z).astype(BF16), bd, preferred_element_type=F32)
    scale = lax.rsqrt(ms + EPS)
    lane = lax.broadcasted_iota(jnp.int32, (z.shape[0], LANES), 1)
    first_half = (lane & (HEAD_DIM // 2)) == 0
    halves = []
    for c in range(0, MXU_WIDTH, LANES):
        y = z[:, c:c + LANES] * scale[:, c:c + LANES] * g128
        partner = jnp.where(first_half,
                            pltpu.roll(y, LANES - HEAD_DIM // 2, 1),
                            pltpu.roll(y, HEAD_DIM // 2, 1))
        halves.append(y * cos_t + partner * sin_t)
    return halves


def _norm_proj_kv_kernel(xp_ref, xs_ref, gm_ref, w_ref, g_ref, cos_ref, sin_ref, bd_ref,
                         xn_ref, kv_ref, wbf_ref):
    _cast_weight_once(w_ref, wbf_ref, 0)
    is_prompt = pl.program_id(0) < N_P // ROW_TILE_SMALL
    g128 = g_ref[...]
    bd = bd_ref[...]
    piece = xn_ref.shape[0] // PROJ_Q_ROW_PIECES
    for r0 in range(0, xn_ref.shape[0], piece):
        rows = slice(r0, r0 + piece)
        x = jnp.where(is_prompt, xp_ref[rows, :], xs_ref[rows, :])
        xn = _rmsnorm_rows(x, gm_ref[...]).astype(BF16)
        xn_ref[rows, :] = xn
        z = jnp.dot(xn, wbf_ref[...], preferred_element_type=F32)
        k_low, k_high = _headnorm_rope(z[:, :D_KV], g128, cos_ref[rows, :], sin_ref[rows, :], bd)
        kv_ref[rows, 0:LANES] = k_low
        kv_ref[rows, LANES:D_KV] = k_high
        kv_ref[rows, D_KV:] = z[:, D_KV:]


def _norm_proj_kv(x_p, x_s, g_mix, w_in, k_g128, cos_t, sin_t, bd):
    tm = ROW_TILE_SMALL
    tn = 2 * D_KV
    last_p = N_P // tm - 1
    const = lambda shape: pl.BlockSpec(shape, lambda i: (0, 0))
    return pl.pallas_call(
        _norm_proj_kv_kernel,
        grid=(N_ALL // tm,),
        in_specs=[pl.BlockSpec((tm, D_MODEL), lambda i: (jnp.minimum(i, last_p), 0)),
                  const((tm, D_MODEL)),
                  const((1, D_MODEL)),
                  pl.BlockSpec((D_MODEL, tn), lambda i: (0, D_ATTN // tn)),
                  const((1, LANES)),
                  pl.BlockSpec((tm, LANES), lambda i: (i, 0)),
                  pl.BlockSpec((tm, LANES), lambda i: (i, 0)),
                  const((MXU_WIDTH, MXU_WIDTH))],
        out_specs=[pl.BlockSpec((tm, D_MODEL), lambda i: (i, 0)),
                   pl.BlockSpec((tm, tn), lambda i: (i, 0))],
        out_shape=[jax.ShapeDtypeStruct((N_ALL, D_MODEL), BF16),
                   jax.ShapeDtypeStruct((N_ALL, tn), F32)],
        scratch_shapes=[pltpu.VMEM((D_MODEL, tn), BF16)],
        compiler_params=_params(1),
        name="norm_proj_kv",
    )(x_p, x_s, g_mix, w_in, k_g128, cos_t, sin_t, bd)


def _proj_rest_kernel(x_ref, w_ref, scale_ref, o_ref, wbf_ref):
    def project(weight_chunk):
        for c0 in range(0, o_ref.shape[1], MXU_WIDTH):
            cols = slice(c0, c0 + MXU_WIDTH)
            z = jnp.dot(x_ref[...], weight_chunk(cols), preferred_element_type=F32)
            o_ref[:, cols] = (z * scale_ref[:, cols]).astype(o_ref.dtype)

    def cast_and_keep(cols):
        chunk = w_ref[:, cols].astype(BF16)
        wbf_ref[:, cols] = chunk
        return chunk

    @pl.when(pl.program_id(1) == 0)
    def _():
        project(cast_and_keep)

    @pl.when(pl.program_id(1) != 0)
    def _():
        project(lambda cols: wbf_ref[:, cols])


def _proj_q_kernel(x_ref, w_ref, g_ref, cos_ref, sin_ref, bd_ref, wo_ref, o_ref, wo_bf_ref, wbf_ref):
    _cast_weight_once(w_ref, wbf_ref, 1)
    wo_bf_ref[...] = wo_ref[...].astype(BF16)
    g128 = g_ref[...]
    bd = bd_ref[...]
    piece = o_ref.shape[0] // PROJ_Q_ROW_PIECES
    for r0 in range(0, o_ref.shape[0], piece):
        rows = slice(r0, r0 + piece)
        z_all = jnp.dot(x_ref[rows, :], wbf_ref[...], preferred_element_type=F32)
        cos_t = cos_ref[rows, :]
        sin_t = sin_ref[rows, :]
        for c0 in range(0, o_ref.shape[1], MXU_WIDTH):
            low, high = _headnorm_rope(z_all[:, c0:c0 + MXU_WIDTH], g128, cos_t, sin_t, bd)
            o_ref[rows, c0:c0 + LANES] = low.astype(o_ref.dtype)
            o_ref[rows, c0 + LANES:c0 + MXU_WIDTH] = high.astype(o_ref.dtype)


def _proj_q_rest(xn, w_in, q_g128, cos_q, sin_q, bd, rest_scale, w_out):
    tm = ROW_TILE_BIG
    n_row = N_ALL // tm
    k = D_MODEL
    x_spec = pl.BlockSpec((tm, k), lambda j, i: (i, 0))
    rope_specs = [pl.BlockSpec((1, LANES), lambda j, i: (0, 0)),
                  pl.BlockSpec((tm, LANES), lambda j, i: (i, 0)),
                  pl.BlockSpec((tm, LANES), lambda j, i: (i, 0)),
                  pl.BlockSpec((MXU_WIDTH, MXU_WIDTH), lambda j, i: (0, 0))]

    tn_q = 1024
    n_col_q = D_ATTN // tn_q
    wo_rows = D_ATTN // (n_col_q * n_row)
    wo_spec = pl.BlockSpec((wo_rows, D_MODEL), lambda j, i: (j * n_row + i, 0))
    q, w_out_bf = pl.pallas_call(
        _proj_q_kernel,
        grid=(n_col_q, n_row),
        in_specs=[x_spec, pl.BlockSpec((k, tn_q), lambda j, i: (0, j))] + rope_specs + [wo_spec],
        out_specs=[pl.BlockSpec((tm, tn_q), lambda j, i: (i, j)), wo_spec],
        out_shape=[jax.ShapeDtypeStruct((N_ALL, D_ATTN), BF16),
                   jax.ShapeDtypeStruct((D_ATTN, D_MODEL), BF16)],
        scratch_shapes=[pltpu.VMEM((k, tn_q), BF16)],
        compiler_params=_params(2),
        name="proj_q",
    )(xn, w_in, q_g128, cos_q, sin_q, bd, w_out)

    tm_r = ROW_TILE_HUGE
    tn_r = 1024
    rest_start = D_ATTN + 2 * D_KV
    rest = pl.pallas_call(
        _proj_rest_kernel,
        grid=(D_REST // tn_r, N_ALL // tm_r),
        in_specs=[pl.BlockSpec((tm_r, k), lambda j, i: (i, 0)),
                  pl.BlockSpec((pl.Element(k), pl.Element(tn_r)),
                               lambda j, i: (0, pl.multiple_of(rest_start + j * tn_r, LANES))),
                  pl.BlockSpec((1, tn_r), lambda j, i: (0, j))],
        out_specs=pl.BlockSpec((tm_r, tn_r), lambda j, i: (i, j)),
        out_shape=jax.ShapeDtypeStruct((N_ALL, D_REST), BF16),
        scratch_shapes=[pltpu.VMEM((k, tn_r), BF16)],
        compiler_params=_params(2),
        name="proj_rest",
    )(xn, w_in, rest_scale)
    return q, rest, w_out_bf


def _gated_merge(attn_half, u, u1, u2, b_ref, ga_half_ref, gc_half_ref, cw_ref):
    return _gated_merge_values(attn_half, u, u1, u2, b_ref[...].astype(F32),
                               ga_half_ref[...], gc_half_ref[...], 0.5 * cw_ref[...])


def _gated_merge_values(attn_half, u, u1, u2, b, ga_half, gc_half, cw_half):
    conv_half = cw_half[0:1, :] * u2 + cw_half[1:2, :] * u1 + cw_half[2:3, :] * u
    bc = (b * conv_half).astype(BF16)
    a = attn_half.astype(BF16)
    ta = jnp.tanh(ga_half)
    tc = jnp.tanh(gc_half)
    return (ta * a + a) + (tc * bc + bc)


def _prompt_attention_block(j, sinks_ref, q_ref, kv_prev, kv_cur, attn_ref, row0):
    n_keys = 2 * WINDOW
    q_rows = slice(row0, row0 + WINDOW)
    kv = jnp.concatenate([kv_prev, kv_cur], axis=0)
    key_row = lax.broadcasted_iota(jnp.int32, (n_keys, LANES), 0)
    low = lax.broadcasted_iota(jnp.int32, (n_keys, LANES), 1) < HEAD_DIM
    low_v = low & (key_row > 0)
    high_v = jnp.logical_not(low) & (key_row > 0)
    ones_low = jnp.where(low, 2.0, 0.0)
    ones_high = jnp.where(low, 0.0, 2.0)

    r = lax.broadcasted_iota(jnp.int32, (WINDOW, n_keys), 0)
    c = lax.broadcasted_iota(jnp.int32, (WINDOW, n_keys), 1)
    first_key = jnp.where(j > 0, 0, WINDOW)
    mask = jnp.where((c > r) & (c <= r + WINDOW) & (c >= first_key), 0.0, NEG_INF)
    mask = jnp.where(c == 0, 0.0, mask).astype(BF16)
    mask2 = jnp.concatenate([mask, mask], axis=1)
    eye = (lax.broadcasted_iota(jnp.int32, (WINDOW, WINDOW), 0)
           == lax.broadcasted_iota(jnp.int32, (WINDOW, WINDOW), 1))
    eye = jnp.where(eye, 1.0, 0.0).astype(BF16)
    sink_lane = lax.broadcasted_iota(jnp.int32, (1, LANES), 1) == 0
    not_slot0 = lax.broadcasted_iota(jnp.int32, (HEAD_DIM, n_keys), 1) > 0
    no_keys = jnp.zeros((HEAD_DIM, n_keys), F32)

    for g in range(N_KV_HEADS):
        chunk = (g // 2) * LANES
        kc_t = kv[:, chunk:chunk + LANES].T
        kg_t = kc_t[(g % 2) * HEAD_DIM:(g % 2 + 1) * HEAD_DIM]
        kg_t = jnp.where(not_slot0, kg_t, 0.0)
        vc = kv[:, D_KV + chunk:D_KV + chunk + LANES]
        if g % 2 == 0:
            v_low = jnp.where(low_v, vc, 0.0)
            v_high = pltpu.roll(v_low, HEAD_DIM, 1)
        else:
            v_high = jnp.where(high_v, vc, 0.0)
            v_low = pltpu.roll(v_high, HEAD_DIM, 1)
        k_bd_t = jnp.concatenate(
            [jnp.concatenate([kg_t, no_keys], axis=1),
             jnp.concatenate([no_keys, kg_t], axis=1)], axis=0).astype(BF16)
        k_aug = jnp.concatenate([k_bd_t, mask2], axis=0)
        v_bd = jnp.concatenate(
            [jnp.concatenate([v_low, ones_low], axis=1),
             jnp.concatenate([v_high, ones_high], axis=1)], axis=0).astype(BF16)

        for k in range(PAIRS_PER_GROUP):
            pair = g * PAIRS_PER_GROUP + k
            cols = slice(pair * LANES, (pair + 1) * LANES)
            q_aug = jnp.concatenate([q_ref[q_rows, cols], eye], axis=1)
            s = jnp.dot(q_aug, k_aug, preferred_element_type=F32)
            halves = []
            for half in range(2):
                sh = s[:, half * n_keys:(half + 1) * n_keys]
                sink = sinks_ref[0, 2 * pair + half] * LOG2E
                sink_row = jnp.where(sink_lane, sink, 0.0)
                sh = jnp.concatenate([sh[:, :LANES] + sink_row, sh[:, LANES:]], axis=1)
                halves.append(jnp.exp2(sh - jnp.max(sh, axis=1, keepdims=True)))
            p = jnp.concatenate(halves, axis=1).astype(BF16)
            o = jnp.dot(p, v_bd, preferred_element_type=F32)
            attn_ref[q_rows, cols] = (o[:, :LANES] / o[:, LANES:]).astype(attn_ref.dtype)


def _prompt_mix_kernel(sinks_ref, q_ref, kvp_ref, kvc_ref, rest_ref, cw_ref, wd_ref,
                       mix_ref, tail_ref, wd_bf_ref, u_scr, attn_scr):
    j = pl.program_id(0)
    wd_bf_ref[...] = wd_ref[...].astype(BF16)
    h_off, b_off, c_off, ga_off, gc_off = (n * D_MODEL for n in range(5))

    @pl.when(j == 0)
    def _():
        u_scr[0:8, :] = jnp.zeros((8, D_CONV), F32)

    tm = MIX_BLOCKS * WINDOW
    for sb in range(MIX_BLOCKS):
        kv_prev = kvp_ref[...] if sb == 0 else kvc_ref[(sb - 1) * WINDOW:sb * WINDOW, :]
        _prompt_attention_block(j * MIX_BLOCKS + sb, sinks_ref, q_ref, kv_prev,
                                kvc_ref[sb * WINDOW:(sb + 1) * WINDOW, :], attn_scr, sb * WINDOW)

    rb, cb = MERGE_CHUNK
    cw_half = 0.5 * cw_ref[...]
    for r0 in range(0, tm, rb):
        rows = slice(r0, r0 + rb)
        u_scr[8 + r0:8 + r0 + rb, :] = (rest_ref[rows, c_off:c_off + D_MODEL].astype(F32)
                                        * rest_ref[rows, h_off:h_off + D_MODEL].astype(F32))
        for c0 in range(0, D_MODEL, cb):
            cols = slice(c0, c0 + cb)
            x = u_scr[r0:r0 + rb + 8, cols]
            mix = _gated_merge_values(
                attn_scr[rows, cols], x[8:], pltpu.roll(x, 1, 0)[8:], pltpu.roll(x, 2, 0)[8:],
                rest_ref[rows, b_off + c0:b_off + c0 + cb].astype(F32),
                rest_ref[rows, ga_off + c0:ga_off + c0 + cb],
                rest_ref[rows, gc_off + c0:gc_off + c0 + cb], cw_half[:, cols])
            mix_ref[rows, cols] = mix.astype(mix_ref.dtype)
    tail = u_scr[tm:tm + 8, :]
    u_scr[0:8, :] = tail
    tail_ref[...] = tail


def _prompt_mix(sinks, q, kv, rest, conv_w, w_down):
    tm = MIX_BLOCKS * WINDOW
    n_steps = N_P // tm
    wd_spec = pl.BlockSpec((D_FF // n_steps, D_MODEL), lambda j: (j, 0))
    return pl.pallas_call(
        _prompt_mix_kernel,
        grid=(n_steps,),
        in_specs=[pl.BlockSpec(memory_space=pltpu.SMEM),
                  pl.BlockSpec((tm, D_ATTN), lambda j: (j, 0)),
                  pl.BlockSpec((WINDOW, 2 * D_KV), lambda j: (jnp.maximum(j * MIX_BLOCKS - 1, 0), 0)),
                  pl.BlockSpec((tm, 2 * D_KV), lambda j: (j, 0)),
                  pl.BlockSpec((tm, D_REST), lambda j: (j, 0)),
                  pl.BlockSpec((CONV_WIDTH, D_CONV), lambda j: (0, 0)),
                  wd_spec],
        out_specs=[pl.BlockSpec((tm, D_MODEL), lambda j: (j, 0)),
                   pl.BlockSpec((8, D_CONV), lambda j: (0, 0)),
                   wd_spec],
        out_shape=[jax.ShapeDtypeStruct((N_P, D_MODEL), BF16),
                   jax.ShapeDtypeStruct((8, D_CONV), F32),
                   jax.ShapeDtypeStruct((D_FF, D_MODEL), BF16)],
        scratch_shapes=[pltpu.VMEM((8 + tm, D_CONV), F32), pltpu.VMEM((tm, D_ATTN), F32)],
        compiler_params=_params(1),
        name="prompt_mix",
    )(sinks, q, kv, kv, rest, conv_w, w_down)


def _sample_attn_kernel(q_ref, kvn_ref, ckt_ref, cvt_ref, sink_ref, rep_ref,
                        o_ref, kwt_ref, vwt_ref, bias_c, bias_n):
    n_new = SEQ_BLOCK * DEC_SEQ
    rows = DEC_SEQ * N_HEADS
    keep = WINDOW - DEC_SEQ

    @pl.when(pl.program_id(0) == 0)
    def _():
        row = lax.broadcasted_iota(jnp.int32, (rows, WINDOW), 0)
        col = lax.broadcasted_iota(jnp.int32, (rows, WINDOW), 1)
        t = row >> 5
        bias_c[...] = jnp.where(col > t, 0.0, NEG_INF)
        for s in range(SEQ_BLOCK):
            valid_new = (col < n_new) & ((col >> 2) == s) & ((col & 3) <= t)
            bias_n[s] = jnp.where(valid_new, 0.0, NEG_INF)

    kvn = jnp.concatenate([kvn_ref[...], jnp.zeros((WINDOW - n_new, 2 * D_KV), F32)], axis=0)
    kn_t = kvn[:, :D_KV].T
    vn_t = kvn[:, D_KV:].T
    kn_t_bf = kn_t.astype(BF16)
    vn_bf = kvn[:, D_KV:].astype(BF16)

    all_rows = SEQ_BLOCK * rows
    row2 = lax.broadcasted_iota(jnp.int32, (all_rows, D_KV), 0)
    col2 = lax.broadcasted_iota(jnp.int32, (all_rows, D_KV), 1)
    same_group = ((row2 & (N_HEADS - 1)) >> 3) == (col2 >> 6)
    win_lane = lax.broadcasted_iota(jnp.int32, (D_KV, WINDOW), 1)
    sink_b = jnp.concatenate([sink_ref[...]] * SEQ_BLOCK, axis=0)
    ones = jnp.ones((2 * WINDOW, LANES), BF16)

    q_all = q_ref[...].reshape(all_rows, HEAD_DIM)
    q4 = jnp.dot(q_all, rep_ref[...], preferred_element_type=F32)
    q4 = jnp.where(same_group, q4, 0.0).astype(BF16)
    s_n = jnp.dot(q4, kn_t_bf, preferred_element_type=F32) + bias_n[...].reshape(all_rows, WINDOW)
    s_c = jnp.concatenate(
        [jnp.dot(q4[s * rows:(s + 1) * rows], ckt_ref[s].astype(BF16), preferred_element_type=F32)
         + bias_c[...] for s in range(SEQ_BLOCK)], axis=0)
    m_b = jnp.maximum(jnp.max(jnp.maximum(s_c, s_n), axis=1, keepdims=True), sink_b)
    p_c = jnp.exp2(s_c - m_b).astype(BF16)
    p_n = jnp.exp2(s_n - m_b).astype(BF16)
    l_b = (jnp.dot(jnp.concatenate([p_c, p_n], axis=1), ones, preferred_element_type=F32)
           + jnp.exp2(sink_b - m_b))
    o4 = jnp.concatenate(
        [lax.dot_general(p_c[s * rows:(s + 1) * rows], cvt_ref[s].astype(BF16),
                         (((1,), (1,)), ((), ())), preferred_element_type=F32)
         for s in range(SEQ_BLOCK)], axis=0) + jnp.dot(p_n, vn_bf, preferred_element_type=F32)
    o4 = jnp.where(same_group, o4, 0.0)
    a = o4[:, 0:LANES] + o4[:, LANES:]
    o = (a + pltpu.roll(a, HEAD_DIM, 1)) / (2.0 * l_b)
    o_ref[...] = o[:, 0:HEAD_DIM].reshape(SEQ_BLOCK, rows, HEAD_DIM).astype(o_ref.dtype)

    for s in range(SEQ_BLOCK):
        k_t = ckt_ref[s]
        v_t = cvt_ref[s]
        new_shift = keep - s * DEC_SEQ
        kwt_ref[s] = jnp.where(win_lane >= keep, pltpu.roll(kn_t, new_shift, 1),
                               pltpu.roll(k_t, keep, 1))
        vwt_ref[s] = jnp.where(win_lane >= keep, pltpu.roll(vn_t, new_shift, 1),
                               pltpu.roll(v_t, keep, 1))


def _sample_attn(q_s, kv, cache_kt, cache_vt, sink_col, rep):
    rows = DEC_SEQ * N_HEADS
    n_new = SEQ_BLOCK * DEC_SEQ
    cache_spec = pl.BlockSpec((SEQ_BLOCK, D_KV, WINDOW), lambda i: (i, 0, 0))
    return pl.pallas_call(
        _sample_attn_kernel,
        grid=(DEC_BATCH // SEQ_BLOCK,),
        in_specs=[pl.BlockSpec((SEQ_BLOCK, rows, HEAD_DIM), lambda i: (i, 0, 0)),
                  pl.BlockSpec((n_new, 2 * D_KV), lambda i: (i + N_P // n_new, 0)),
                  cache_spec, cache_spec,
                  pl.BlockSpec((rows, LANES), lambda i: (0, 0)),
                  pl.BlockSpec((HEAD_DIM, D_KV), lambda i: (0, 0))],
        out_specs=[pl.BlockSpec((SEQ_BLOCK, rows, HEAD_DIM), lambda i: (i, 0, 0)),
                   cache_spec, cache_spec],
        out_shape=[jax.ShapeDtypeStruct((DEC_BATCH, rows, HEAD_DIM), BF16),
                   jax.ShapeDtypeStruct((DEC_BATCH, D_KV, WINDOW), F32),
                   jax.ShapeDtypeStruct((DEC_BATCH, D_KV, WINDOW), F32)],
        scratch_shapes=[pltpu.VMEM((rows, WINDOW), F32), pltpu.VMEM((SEQ_BLOCK, rows, WINDOW), F32)],
        compiler_params=_params(1),
        name="sample_attn",
    )(q_s, kv, cache_kt, cache_vt, sink_col, rep)


def _sample_merge_kernel(attn_ref, h_ref, b_ref, c_ref, ga_ref, gc_ref, st_ref, cw_ref,
                         mix_ref, state_a_ref, state_b_ref, u_scr):
    u = c_ref[...].astype(F32) * h_ref[...].astype(F32)
    rows = u.shape[0]
    n_seq = rows // DEC_SEQ
    for c in range(D_CONV // LANES):
        cols = slice(c * LANES, (c + 1) * LANES)
        u_scr[c] = u[:, cols]
        state_a_ref[:, cols] = u_scr[c, pl.ds(DEC_SEQ - 2, n_seq, stride=DEC_SEQ), :]
        state_b_ref[:, cols] = u_scr[c, pl.ds(DEC_SEQ - 1, n_seq, stride=DEC_SEQ), :]
    t = lax.broadcasted_iota(jnp.int32, u.shape, 0) & (DEC_SEQ - 1)
    st = st_ref[...]
    u1 = jnp.where(t == 0, pltpu.roll(st, rows - 1, 0), pltpu.roll(u, 1, 0))
    u2 = jnp.where(t < 2, st, pltpu.roll(u, 2, 0))
    mix = _gated_merge(attn_ref[...], u, u1, u2, b_ref, ga_ref, gc_ref, cw_ref)
    mix_ref[...] = mix.astype(mix_ref.dtype)


def _sample_merge(attn_s, rest, st_rows, conv_w):
    tm = 2 * WINDOW
    off = N_P // tm
    n_seq = tm // DEC_SEQ
    blk = lambda c: pl.BlockSpec((tm, D_MODEL), lambda i, c=c: (i + off, c))
    state_spec = pl.BlockSpec((n_seq, D_CONV), lambda i: (i, 0))
    return pl.pallas_call(
        _sample_merge_kernel,
        grid=(N_S // tm,),
        in_specs=[pl.BlockSpec((tm, D_ATTN), lambda i: (i, 0)),
                  blk(0), blk(1), blk(2), blk(3), blk(4),
                  pl.BlockSpec((tm, D_CONV), lambda i: (i, 0)),
                  pl.BlockSpec((CONV_WIDTH, D_CONV), lambda i: (0, 0))],
        out_specs=[pl.BlockSpec((tm, D_MODEL), lambda i: (i, 0)), state_spec, state_spec],
        out_shape=[jax.ShapeDtypeStruct((N_S, D_MODEL), BF16),
                   jax.ShapeDtypeStruct((DEC_BATCH, D_CONV), F32),
                   jax.ShapeDtypeStruct((DEC_BATCH, D_CONV), F32)],
        scratch_shapes=[pltpu.VMEM((D_CONV // LANES, tm, LANES), F32)],
        compiler_params=_params(1),
        name="sample_merge",
    )(attn_s, rest, rest, rest, rest, rest, st_rows, conv_w)


def _out_proj_kernel(mp_ref, ms_ref, w_ref, xp_ref, xs_ref, gf_ref, y_ref, yn_ref):
    is_prompt = pl.program_id(0) < N_P // ROW_TILE_SMALL
    res = jnp.where(is_prompt, xp_ref[...], xs_ref[...])
    mix = jnp.where(is_prompt, mp_ref[...], ms_ref[...])
    y = res + jnp.dot(mix, w_ref[...], preferred_element_type=F32)
    y_ref[...] = y
    yn_ref[...] = _rmsnorm_rows(y, gf_ref[...]).astype(yn_ref.dtype)


def _out_proj(mix_p, mix_s, w_out_bf, x_p, x_s, g_ffn):
    tm = ROW_TILE_SMALL
    last_p = N_P // tm - 1
    once = pl.Buffered(1)
    return pl.pallas_call(
        _out_proj_kernel,
        grid=(N_ALL // tm,),
        in_specs=[pl.BlockSpec((tm, D_ATTN), lambda i: (jnp.minimum(i, last_p), 0)),
                  pl.BlockSpec((tm, D_ATTN), lambda i: (0, 0), pipeline_mode=once),
                  pl.BlockSpec((D_ATTN, D_MODEL), lambda i: (0, 0), pipeline_mode=once),
                  pl.BlockSpec((tm, D_MODEL), lambda i: (jnp.minimum(i, last_p), 0)),
                  pl.BlockSpec((tm, D_MODEL), lambda i: (0, 0), pipeline_mode=once),
                  pl.BlockSpec((1, D_MODEL), lambda i: (0, 0))],
        out_specs=[pl.BlockSpec((tm, D_MODEL), lambda i: (i, 0)),
                   pl.BlockSpec((tm, D_MODEL), lambda i: (i, 0))],
        out_shape=[jax.ShapeDtypeStruct((N_ALL, D_MODEL), F32),
                   jax.ShapeDtypeStruct((N_ALL, D_MODEL), BF16)],
        compiler_params=_params(1),
        name="out_proj",
    )(mix_p, mix_s, w_out_bf, x_p, x_s, g_ffn)


def _ffn_up_kernel(x_ref, wg_ref, wu_ref, o_ref, wg_bf, wu_bf):
    def project(weight_chunks):
        piece = o_ref.shape[0] // FFN_UP_ROW_PIECES
        for c0 in range(0, o_ref.shape[1], MXU_WIDTH):
            cols = slice(c0, c0 + MXU_WIDTH)
            wg, wu = weight_chunks(cols)
            for r0 in range(0, o_ref.shape[0], piece):
                rows = slice(r0, r0 + piece)
                g = jnp.dot(x_ref[rows, :], wg, preferred_element_type=F32)
                u = jnp.dot(x_ref[rows, :], wu, preferred_element_type=F32)
                o_ref[rows, cols] = ((g * jax.nn.sigmoid(g)) * u).astype(o_ref.dtype)

    def cast_and_keep(cols):
        wg = wg_ref[:, cols].astype(BF16)
        wu = wu_ref[:, cols].astype(BF16)
        wg_bf[:, cols] = wg
        wu_bf[:, cols] = wu
        return wg, wu

    @pl.when(pl.program_id(1) == 0)
    def _():
        project(cast_and_keep)

    @pl.when(pl.program_id(1) != 0)
    def _():
        project(lambda cols: (wg_bf[:, cols], wu_bf[:, cols]))


def _ffn_up(yn, w_gate_up):
    tm = ROW_TILE_HUGE
    tn = 512
    nt = D_FF // tn
    return pl.pallas_call(
        _ffn_up_kernel,
        grid=(nt, N_ALL // tm),
        in_specs=[pl.BlockSpec((tm, D_MODEL), lambda j, i: (i, 0)),
                  pl.BlockSpec((D_MODEL, tn), lambda j, i: (0, j)),
                  pl.BlockSpec((D_MODEL, tn), lambda j, i: (0, j + nt))],
        out_specs=pl.BlockSpec((tm, tn), lambda j, i: (i, j)),
        out_shape=jax.ShapeDtypeStruct((N_ALL, D_FF), BF16),
        scratch_shapes=[pltpu.VMEM((D_MODEL, tn), BF16), pltpu.VMEM((D_MODEL, tn), BF16)],
        compiler_params=_params(2),
        name="ffn_up",
    )(yn, w_gate_up, w_gate_up)


def _ffn_down_kernel(a_ref, w_ref, r_ref, yp_ref, ys_ref):
    i = pl.program_id(0)
    n_prompt_tiles = N_P // FFN_DOWN_ROWS
    def project_into(y_ref):
        for c0 in range(0, D_MODEL, FFN_DOWN_CHUNK):
            cols = slice(c0, c0 + FFN_DOWN_CHUNK)
            y_ref[:, cols] = r_ref[:, cols] + jnp.dot(a_ref[...], w_ref[:, cols],
                                                      preferred_element_type=F32)

    @pl.when(i < n_prompt_tiles)
    def _():
        project_into(yp_ref)

    @pl.when(i >= n_prompt_tiles)
    def _():
        project_into(ys_ref)


def _ffn_down(act, w_down_bf, y1):
    tm = FFN_DOWN_ROWS
    n_p = N_P // tm
    return pl.pallas_call(
        _ffn_down_kernel,
        grid=(N_ALL // tm,),
        in_specs=[pl.BlockSpec((tm, D_FF), lambda i: (i, 0)),
                  pl.BlockSpec((D_FF, D_MODEL), lambda i: (0, 0), pipeline_mode=pl.Buffered(1)),
                  pl.BlockSpec((tm, D_MODEL), lambda i: (i, 0))],
        out_specs=[pl.BlockSpec((tm, D_MODEL), lambda i: (jnp.minimum(i, n_p - 1), 0)),
                   pl.BlockSpec((tm, D_MODEL), lambda i: (jnp.maximum(i - n_p, 0), 0))],
        out_shape=[jax.ShapeDtypeStruct((N_P, D_MODEL), F32),
                   jax.ShapeDtypeStruct((N_S, D_MODEL), F32)],
        compiler_params=_params(1),
        name="ffn_down",
    )(act, w_down_bf, y1)


def _rope_tables():
    inv = ROPE_THETA ** (-jnp.arange(0, HEAD_DIM, 2, dtype=F32) / HEAD_DIM)
    inv_t = jnp.tile(inv, LANES // (HEAD_DIM // 2))[None, :]
    sign = jnp.asarray(np.tile(np.repeat([-1.0, 1.0], HEAD_DIM // 2), LANES // HEAD_DIM), F32)[None, :]
    ang_a = (jnp.arange(N_P // LANES, dtype=jnp.int32) * LANES).astype(F32)[:, None] * inv_t
    ang_b = jnp.arange(LANES, dtype=jnp.int32).astype(F32)[:, None] * inv_t
    ca, sa = jnp.cos(ang_a)[:, None, :], jnp.sin(ang_a)[:, None, :]
    cb, sb = jnp.cos(ang_b)[None, :, :], jnp.sin(ang_b)[None, :, :]
    cos_p = (ca * cb - sa * sb).reshape(N_P, LANES)
    sin_p = (sa * cb + ca * sb).reshape(N_P, LANES)
    ang_s = (PAST_LEN + jnp.arange(DEC_SEQ, dtype=jnp.int32)).astype(F32)[:, None] * inv_t
    cos_s = jnp.tile(jnp.cos(ang_s), (DEC_BATCH, 1))
    sin_s = jnp.tile(jnp.sin(ang_s), (DEC_BATCH, 1))
    return (jnp.concatenate([cos_p, cos_s], axis=0),
            jnp.concatenate([sin_p, sin_s], axis=0) * sign)


def kernel(x_prompt, x_sample, cache_k_win, cache_v_win, state_conv, norm_mix, w_in, q_norm, k_norm,
           sinks, conv_w, w_out, norm_ffn, w_gate_up, w_down):
    assert x_prompt.shape == (1, SEQ, D_MODEL) and x_sample.shape == (DEC_BATCH, DEC_SEQ, D_MODEL)
    assert w_in.shape == (1, D_MODEL, D_IN)
    assert cache_k_win.shape == (1, DEC_BATCH, WINDOW, N_KV_HEADS, HEAD_DIM)

    x_p = x_prompt.reshape(N_P, D_MODEL)
    x_s = x_sample.reshape(N_S, D_MODEL)
    w_in2 = w_in.reshape(D_MODEL, D_IN)
    w_out2 = w_out.reshape(D_ATTN, D_MODEL)
    w_gu2 = w_gate_up.reshape(D_MODEL, 2 * D_FF)
    w_dn2 = w_down.reshape(D_FF, D_MODEL)
    conv_w2 = conv_w.reshape(CONV_WIDTH, D_CONV)
    cache_kt = jnp.transpose(cache_k_win.reshape(DEC_BATCH, WINDOW, D_KV), (0, 2, 1))
    cache_vt = jnp.transpose(cache_v_win.reshape(DEC_BATCH, WINDOW, D_KV), (0, 2, 1))

    cos_t, sin_t = _rope_tables()
    q_g128 = jnp.tile(q_norm.reshape(1, HEAD_DIM) * Q_SCALE, (1, LANES // HEAD_DIM))
    k_g128 = jnp.tile(k_norm.reshape(1, HEAD_DIM), (1, LANES // HEAD_DIM))
    rest_scale = jnp.asarray(np.repeat([1.0, 1.0, 1.0, 0.5, 0.5], D_MODEL)[None, :], F32)
    head_of_lane = np.arange(MXU_WIDTH) // HEAD_DIM
    bd = jnp.asarray(np.where(head_of_lane[:, None] == head_of_lane[None, :], 1.0 / HEAD_DIM, 0.0),
                     dtype=BF16)
    sinks2 = sinks.reshape(1, N_HEADS).astype(F32)
    sink_col = jnp.broadcast_to(jnp.tile(sinks2.reshape(N_HEADS) * LOG2E, DEC_SEQ)[:, None],
                                (DEC_SEQ * N_HEADS, LANES))
    rep = jnp.asarray(np.tile(np.eye(HEAD_DIM), (1, N_KV_HEADS)), dtype=BF16)
    st_rows = jnp.pad(state_conv.reshape(DEC_BATCH, CONV_WIDTH - 1, D_CONV),
                      ((0, 0), (0, DEC_SEQ - (CONV_WIDTH - 1)), (0, 0))).reshape(N_S, D_CONV)

    xn, kv = _norm_proj_kv(x_p, x_s, norm_mix.reshape(1, D_MODEL), w_in2, k_g128, cos_t, sin_t, bd)
    q, rest, w_out_bf = _proj_q_rest(xn, w_in2, q_g128, cos_t, sin_t, bd, rest_scale, w_out2)

    mix_p, tail, w_down_bf = _prompt_mix(sinks2, q, kv, rest, conv_w2, w_dn2)
    q_s = q[N_P:].reshape(DEC_BATCH, DEC_SEQ * N_HEADS, HEAD_DIM)
    attn_s, kwt, vwt = _sample_attn(q_s, kv, cache_kt, cache_vt, sink_col, rep)
    mix_s, state_a, state_b = _sample_merge(attn_s.reshape(N_S, D_ATTN), rest, st_rows, conv_w2)

    y1, yn = _out_proj(mix_p, mix_s, w_out_bf, x_p, x_s, norm_ffn.reshape(1, D_MODEL))
    act = _ffn_up(yn, w_gu2)
    y_p, y_s = _ffn_down(act, w_down_bf, y1)

    kv_tail = kv[N_P - WINDOW:N_P]
    win_shape = (1, DEC_BATCH, WINDOW, N_KV_HEADS, HEAD_DIM)
    return (y_p.reshape(1, SEQ, D_MODEL),
            y_s.reshape(DEC_BATCH, DEC_SEQ, D_MODEL),
            kv_tail[:, :D_KV].reshape(1, 1, WINDOW, N_KV_HEADS, HEAD_DIM),
            kv_tail[:, D_KV:].reshape(1, 1, WINDOW, N_KV_HEADS, HEAD_DIM),
            tail[8 - (CONV_WIDTH - 1):].reshape(1, 1, CONV_WIDTH - 1, D_CONV),
            jnp.transpose(kwt, (0, 2, 1)).reshape(win_shape),
            jnp.transpose(vwt, (0, 2, 1)).reshape(win_shape),
            jnp.stack([state_a, state_b], axis=1)[None])
```

```python
import math

import numpy as np
import jax
import jax.numpy as jnp
from jax import lax
from jax.experimental import pallas as pl
from jax.experimental.pallas import tpu as pltpu

F32 = jnp.float32
BF16 = jnp.bfloat16

D_MODEL = 2048
SEQ = 8192
DEC_BATCH = 128
DEC_SEQ = 4
PAST_LEN = 8192
N_HEADS = 32
HEAD_DIM = 64
N_KV_HEADS = 4
GROUP = N_HEADS // N_KV_HEADS
D_ATTN = N_HEADS * HEAD_DIM
D_KV = N_KV_HEADS * HEAD_DIM
WINDOW = 128
D_CONV = D_MODEL
CONV_WIDTH = 3
D_FF = 5632
ROPE_THETA = 10000.0
EPS = 1e-6
NEG_INF = -1e30
D_IN = 2 * D_ATTN + 2 * D_KV + 3 * D_CONV + D_CONV
D_REST = 5 * D_MODEL
LOG2E = math.log2(math.e)
Q_SCALE = HEAD_DIM ** -0.5 * LOG2E

N_P = SEQ
N_S = DEC_BATCH * DEC_SEQ
N_ALL = N_P + N_S

LANES = 128
VMEM_LIMIT = 56 * 1024 * 1024

ROW_TILE_SMALL = 512
ROW_TILE_BIG = 1088
ROW_TILE_HUGE = 2176
MXU_WIDTH = 256
X_RING = 3
PROJ_Q_ROW_PIECES = 4
FFN_UP_ROW_PIECES = 4
FFN_DOWN_ROWS = 256
FFN_DOWN_CHUNK = 512
MIX_BLOCKS = 4
MERGE_CHUNK = (32, 256)
SEQ_BLOCK = 8
PAIRS_PER_GROUP = GROUP // 2


def _params(n_axes):
    return pltpu.CompilerParams(
        dimension_semantics=("arbitrary",) * n_axes, vmem_limit_bytes=VMEM_LIMIT)


def _rmsnorm_rows(x, g):
    ms = jnp.mean(x * x, axis=-1, keepdims=True)
    return x * lax.rsqrt(ms + EPS) * g


def _cast_weight_once(w_ref, wbf_ref, axis):
    @pl.when(pl.program_id(axis) == 0)
    def _():
        wbf_ref[...] = w_ref[...].astype(BF16)


def _headnorm_rope(z, g128, cos_t, sin_t, bd):
    ms = jnp.dot((z * z).astype(BF16), bd, preferred_element_type=F32)
    scale = lax.rsqrt(ms + EPS)
    lane = lax.broadcasted_iota(jnp.int32, (z.shape[0], LANES), 1)
    first_half = (lane & (HEAD_DIM // 2)) == 0
    halves = []
    for c in range(0, MXU_WIDTH, LANES):
        y = z[:, c:c + LANES] * scale[:, c:c + LANES] * g128
        partner = jnp.where(first_half,
                            pltpu.roll(y, LANES - HEAD_DIM // 2, 1),
                            pltpu.roll(y, HEAD_DIM // 2, 1))
        halves.append(y * cos_t + partner * sin_t)
    return halves


def _x_tile_copy(xp_hbm, xbuf, sems, tile, slot):
    tm = ROW_TILE_SMALL
    return pltpu.make_async_copy(xp_hbm.at[pl.ds(tile * tm, tm), :], xbuf.at[slot], sems.at[slot])


def _norm_proj_kv_kernel(xp_hbm, xs_ref, gm_ref, w_ref, g_ref, cos_ref, sin_ref, bd_ref,
                         xn_ref, kv_ref, wbf_ref, xbuf, sems):
    i = pl.program_id(0)
    n_p = N_P // ROW_TILE_SMALL
    slot = lax.rem(i, X_RING)
    is_prompt = i < n_p

    @pl.when(i == 0)
    def _():
        for t in range(X_RING):
            _x_tile_copy(xp_hbm, xbuf, sems, t, t).start()

    @pl.when(is_prompt)
    def _():
        _x_tile_copy(xp_hbm, xbuf, sems, i, slot).wait()

    _cast_weight_once(w_ref, wbf_ref, 0)
    g128 = g_ref[...]
    bd = bd_ref[...]
    piece = xn_ref.shape[0] // PROJ_Q_ROW_PIECES
    for r0 in range(0, xn_ref.shape[0], piece):
        rows = slice(r0, r0 + piece)
        x = jnp.where(is_prompt, xbuf[slot, pl.ds(r0, piece), :], xs_ref[rows, :])
        xn = _rmsnorm_rows(x, gm_ref[...]).astype(BF16)
        xn_ref[rows, :] = xn
        z = jnp.dot(xn, wbf_ref[...], preferred_element_type=F32)
        k_low, k_high = _headnorm_rope(z[:, :D_KV], g128, cos_ref[rows, :], sin_ref[rows, :], bd)
        kv_ref[rows, 0:LANES] = k_low
        kv_ref[rows, LANES:D_KV] = k_high
        kv_ref[rows, D_KV:] = z[:, D_KV:]

    @pl.when(i + X_RING < n_p)
    def _():
        _x_tile_copy(xp_hbm, xbuf, sems, i + X_RING, slot).start()


def _norm_proj_kv(x_p, x_s, g_mix, w_in, k_g128, cos_t, sin_t, bd):
    tm = ROW_TILE_SMALL
    tn = 2 * D_KV
    const = lambda shape: pl.BlockSpec(shape, lambda i: (0, 0))
    return pl.pallas_call(
        _norm_proj_kv_kernel,
        grid=(N_ALL // tm,),
        in_specs=[pl.BlockSpec(memory_space=pl.ANY),
                  const((tm, D_MODEL)),
                  const((1, D_MODEL)),
                  pl.BlockSpec((D_MODEL, tn), lambda i: (0, D_ATTN // tn)),
                  const((1, LANES)),
                  pl.BlockSpec((tm, LANES), lambda i: (i, 0)),
                  pl.BlockSpec((tm, LANES), lambda i: (i, 0)),
                  const((MXU_WIDTH, MXU_WIDTH))],
        out_specs=[pl.BlockSpec((tm, D_MODEL), lambda i: (i, 0)),
                   pl.BlockSpec((tm, tn), lambda i: (i, 0))],
        out_shape=[jax.ShapeDtypeStruct((N_ALL, D_MODEL), BF16),
                   jax.ShapeDtypeStruct((N_ALL, tn), F32)],
        scratch_shapes=[pltpu.VMEM((D_MODEL, tn), BF16),
                        pltpu.VMEM((X_RING, tm, D_MODEL), F32),
                        pltpu.SemaphoreType.DMA((X_RING,))],
        compiler_params=_params(1),
        name="norm_proj_kv",
    )(x_p, x_s, g_mix, w_in, k_g128, cos_t, sin_t, bd)


def _proj_rest_kernel(x_ref, w_ref, scale_ref, o_ref, wbf_ref):
    def project(weight_chunk):
        for c0 in range(0, o_ref.shape[1], MXU_WIDTH):
            cols = slice(c0, c0 + MXU_WIDTH)
            z = jnp.dot(x_ref[...], weight_chunk(cols), preferred_element_type=F32)
            o_ref[:, cols] = (z * scale_ref[:, cols]).astype(o_ref.dtype)

    def cast_and_keep(cols):
        chunk = w_ref[:, cols].astype(BF16)
        wbf_ref[:, cols] = chunk
        return chunk

    @pl.when(pl.program_id(1) == 0)
    def _():
        project(cast_and_keep)

    @pl.when(pl.program_id(1) != 0)
    def _():
        project(lambda cols: wbf_ref[:, cols])


def _proj_q_kernel(x_ref, w_ref, g_ref, cos_ref, sin_ref, bd_ref, wo_ref, o_ref, wo_bf_ref, wbf_ref):
    _cast_weight_once(w_ref, wbf_ref, 1)
    wo_bf_ref[...] = wo_ref[...].astype(BF16)
    g128 = g_ref[...]
    bd = bd_ref[...]
    piece = o_ref.shape[0] // PROJ_Q_ROW_PIECES
    for r0 in range(0, o_ref.shape[0], piece):
        rows = slice(r0, r0 + piece)
        z_all = jnp.dot(x_ref[rows, :], wbf_ref[...], preferred_element_type=F32)
        cos_t = cos_ref[rows, :]
        sin_t = sin_ref[rows, :]
        for c0 in range(0, o_ref.shape[1], MXU_WIDTH):
            low, high = _headnorm_rope(z_all[:, c0:c0 + MXU_WIDTH], g128, cos_t, sin_t, bd)
            o_ref[rows, c0:c0 + LANES] = low.astype(o_ref.dtype)
            o_ref[rows, c0 + LANES:c0 + MXU_WIDTH] = high.astype(o_ref.dtype)


def _proj_q_rest(xn, w_in, q_g128, cos_q, sin_q, bd, rest_scale, w_out):
    tm = ROW_TILE_BIG
    n_row = N_ALL // tm
    k = D_MODEL
    x_spec = pl.BlockSpec((tm, k), lambda j, i: (i, 0))
    rope_specs = [pl.BlockSpec((1, LANES), lambda j, i: (0, 0)),
                  pl.BlockSpec((tm, LANES), lambda j, i: (i, 0)),
                  pl.BlockSpec((tm, LANES), lambda j, i: (i, 0)),
                  pl.BlockSpec((MXU_WIDTH, MXU_WIDTH), lambda j, i: (0, 0))]

    tn_q = 1024
    n_col_q = D_ATTN // tn_q
    wo_rows = D_ATTN // (n_col_q * n_row)
    wo_spec = pl.BlockSpec((wo_rows, D_MODEL), lambda j, i: (j * n_row + i, 0))
    q, w_out_bf = pl.pallas_call(
        _proj_q_kernel,
        grid=(n_col_q, n_row),
        in_specs=[x_spec, pl.BlockSpec((k, tn_q), lambda j, i: (0, j))] + rope_specs + [wo_spec],
        out_specs=[pl.BlockSpec((tm, tn_q), lambda j, i: (i, j)), wo_spec],
        out_shape=[jax.ShapeDtypeStruct((N_ALL, D_ATTN), BF16),
                   jax.ShapeDtypeStruct((D_ATTN, D_MODEL), BF16)],
        scratch_shapes=[pltpu.VMEM((k, tn_q), BF16)],
        compiler_params=_params(2),
        name="proj_q",
    )(xn, w_in, q_g128, cos_q, sin_q, bd, w_out)

    tm_r = ROW_TILE_HUGE
    tn_r = 1024
    rest_start = D_ATTN + 2 * D_KV
    rest = pl.pallas_call(
        _proj_rest_kernel,
        grid=(D_REST // tn_r, N_ALL // tm_r),
        in_specs=[pl.BlockSpec((tm_r, k), lambda j, i: (i, 0)),
                  pl.BlockSpec((pl.Element(k), pl.Element(tn_r)),
                               lambda j, i: (0, pl.multiple_of(rest_start + j * tn_r, LANES))),
                  pl.BlockSpec((1, tn_r), lambda j, i: (0, j))],
        out_specs=pl.BlockSpec((tm_r, tn_r), lambda j, i: (i, j)),
        out_shape=jax.ShapeDtypeStruct((N_ALL, D_REST), BF16),
        scratch_shapes=[pltpu.VMEM((k, tn_r), BF16)],
        compiler_params=_params(2),
        name="proj_rest",
    )(xn, w_in, rest_scale)
    return q, rest, w_out_bf


def _gated_merge(attn_half, u, u1, u2, b_ref, ga_half_ref, gc_half_ref, cw_ref):
    return _gated_merge_values(attn_half, u, u1, u2, b_ref[...].astype(F32),
                               ga_half_ref[...], gc_half_ref[...], 0.5 * cw_ref[...])


def _gated_merge_values(attn_half, u, u1, u2, b, ga_half, gc_half, cw_half):
    conv_half = cw_half[0:1, :] * u2 + cw_half[1:2, :] * u1 + cw_half[2:3, :] * u
    bc = (b * conv_half).astype(BF16)
    a = attn_half.astype(BF16)
    ta = jnp.tanh(ga_half)
    tc = jnp.tanh(gc_half)
    return (ta * a + a) + (tc * bc + bc)


def _prompt_attention_block(j, sinks_ref, q_ref, kv_prev, kv_cur, attn_ref, row0):
    n_keys = 2 * WINDOW
    q_rows = slice(row0, row0 + WINDOW)
    kv = jnp.concatenate([kv_prev, kv_cur], axis=0)
    key_row = lax.broadcasted_iota(jnp.int32, (n_keys, LANES), 0)
    low = lax.broadcasted_iota(jnp.int32, (n_keys, LANES), 1) < HEAD_DIM
    low_v = low & (key_row > 0)
    high_v = jnp.logical_not(low) & (key_row > 0)
    ones_low = jnp.where(low, 2.0, 0.0)
    ones_high = jnp.where(low, 0.0, 2.0)

    r = lax.broadcasted_iota(jnp.int32, (WINDOW, n_keys), 0)
    c = lax.broadcasted_iota(jnp.int32, (WINDOW, n_keys), 1)
    first_key = jnp.where(j > 0, 0, WINDOW)
    mask = jnp.where((c > r) & (c <= r + WINDOW) & (c >= first_key), 0.0, NEG_INF)
    mask = jnp.where(c == 0, 0.0, mask).astype(BF16)
    mask2 = jnp.concatenate([mask, mask], axis=1)
    eye = (lax.broadcasted_iota(jnp.int32, (WINDOW, WINDOW), 0)
           == lax.broadcasted_iota(jnp.int32, (WINDOW, WINDOW), 1))
    eye = jnp.where(eye, 1.0, 0.0).astype(BF16)
    sink_lane = lax.broadcasted_iota(jnp.int32, (1, LANES), 1) == 0
    not_slot0 = lax.broadcasted_iota(jnp.int32, (HEAD_DIM, n_keys), 1) > 0
    no_keys = jnp.zeros((HEAD_DIM, n_keys), F32)

    for g in range(N_KV_HEADS):
        chunk = (g // 2) * LANES
        kc_t = kv[:, chunk:chunk + LANES].T
        kg_t = kc_t[(g % 2) * HEAD_DIM:(g % 2 + 1) * HEAD_DIM]
        kg_t = jnp.where(not_slot0, kg_t, 0.0)
        vc = kv[:, D_KV + chunk:D_KV + chunk + LANES]
        if g % 2 == 0:
            v_low = jnp.where(low_v, vc, 0.0)
            v_high = pltpu.roll(v_low, HEAD_DIM, 1)
        else:
            v_high = jnp.where(high_v, vc, 0.0)
            v_low = pltpu.roll(v_high, HEAD_DIM, 1)
        k_bd_t = jnp.concatenate(
            [jnp.concatenate([kg_t, no_keys], axis=1),
             jnp.concatenate([no_keys, kg_t], axis=1)], axis=0).astype(BF16)
        k_aug = jnp.concatenate([k_bd_t, mask2], axis=0)
        v_bd = jnp.concatenate(
            [jnp.concatenate([v_low, ones_low], axis=1),
             jnp.concatenate([v_high, ones_high], axis=1)], axis=0).astype(BF16)

        for k in range(PAIRS_PER_GROUP):
            pair = g * PAIRS_PER_GROUP + k
            cols = slice(pair * LANES, (pair + 1) * LANES)
            q_aug = jnp.concatenate([q_ref[q_rows, cols], eye], axis=1)
            s = jnp.dot(q_aug, k_aug, preferred_element_type=F32)
            halves = []
            for half in range(2):
                sh = s[:, half * n_keys:(half + 1) * n_keys]
                sink = sinks_ref[0, 2 * pair + half] * LOG2E
                sink_row = jnp.where(sink_lane, sink, 0.0)
                sh = jnp.concatenate([sh[:, :LANES] + sink_row, sh[:, LANES:]], axis=1)
                halves.append(jnp.exp2(sh - jnp.max(sh, axis=1, keepdims=True)))
            p = jnp.concatenate(halves, axis=1).astype(BF16)
            o = jnp.dot(p, v_bd, preferred_element_type=F32)
            attn_ref[q_rows, cols] = (o[:, :LANES] / o[:, LANES:]).astype(attn_ref.dtype)


def _prompt_mix_kernel(sinks_ref, q_ref, kvp_ref, kvc_ref, rest_ref, cw_ref, wd_ref,
                       mix_ref, tail_ref, wd_bf_ref, u_scr, attn_scr):
    j = pl.program_id(0)
    wd_bf_ref[...] = wd_ref[...].astype(BF16)
    h_off, b_off, c_off, ga_off, gc_off = (n * D_MODEL for n in range(5))

    @pl.when(j == 0)
    def _():
        u_scr[0:8, :] = jnp.zeros((8, D_CONV), F32)

    tm = MIX_BLOCKS * WINDOW
    for sb in range(MIX_BLOCKS):
        kv_prev = kvp_ref[...] if sb == 0 else kvc_ref[(sb - 1) * WINDOW:sb * WINDOW, :]
        _prompt_attention_block(j * MIX_BLOCKS + sb, sinks_ref, q_ref, kv_prev,
                                kvc_ref[sb * WINDOW:(sb + 1) * WINDOW, :], attn_scr, sb * WINDOW)

    rb, cb = MERGE_CHUNK
    cw_half = 0.5 * cw_ref[...]
    for r0 in range(0, tm, rb):
        rows = slice(r0, r0 + rb)
        u_scr[8 + r0:8 + r0 + rb, :] = (rest_ref[rows, c_off:c_off + D_MODEL].astype(F32)
                                        * rest_ref[rows, h_off:h_off + D_MODEL].astype(F32))
        for c0 in range(0, D_MODEL, cb):
            cols = slice(c0, c0 + cb)
            x = u_scr[r0:r0 + rb + 8, cols]
            mix = _gated_merge_values(
                attn_scr[rows, cols], x[8:], pltpu.roll(x, 1, 0)[8:], pltpu.roll(x, 2, 0)[8:],
                rest_ref[rows, b_off + c0:b_off + c0 + cb].astype(F32),
                rest_ref[rows, ga_off + c0:ga_off + c0 + cb],
                rest_ref[rows, gc_off + c0:gc_off + c0 + cb], cw_half[:, cols])
            mix_ref[rows, cols] = mix.astype(mix_ref.dtype)
    tail = u_scr[tm:tm + 8, :]
    u_scr[0:8, :] = tail
    tail_ref[...] = tail


def _prompt_mix(sinks, q, kv, rest, conv_w, w_down):
    tm = MIX_BLOCKS * WINDOW
    n_steps = N_P // tm
    wd_spec = pl.BlockSpec((D_FF // n_steps, D_MODEL), lambda j: (j, 0))
    return pl.pallas_call(
        _prompt_mix_kernel,
        grid=(n_steps,),
        in_specs=[pl.BlockSpec(memory_space=pltpu.SMEM),
                  pl.BlockSpec((tm, D_ATTN), lambda j: (j, 0)),
                  pl.BlockSpec((WINDOW, 2 * D_KV), lambda j: (jnp.maximum(j * MIX_BLOCKS - 1, 0), 0)),
                  pl.BlockSpec((tm, 2 * D_KV), lambda j: (j, 0)),
                  pl.BlockSpec((tm, D_REST), lambda j: (j, 0)),
                  pl.BlockSpec((CONV_WIDTH, D_CONV), lambda j: (0, 0)),
                  wd_spec],
        out_specs=[pl.BlockSpec((tm, D_MODEL), lambda j: (j, 0)),
                   pl.BlockSpec((8, D_CONV), lambda j: (0, 0)),
                   wd_spec],
        out_shape=[jax.ShapeDtypeStruct((N_P, D_MODEL), BF16),
                   jax.ShapeDtypeStruct((8, D_CONV), F32),
                   jax.ShapeDtypeStruct((D_FF, D_MODEL), BF16)],
        scratch_shapes=[pltpu.VMEM((8 + tm, D_CONV), F32), pltpu.VMEM((tm, D_ATTN), F32)],
        compiler_params=_params(1),
        name="prompt_mix",
    )(sinks, q, kv, kv, rest, conv_w, w_down)


def _sample_attn_kernel(q_ref, kvn_ref, ckt_ref, cvt_ref, sink_ref, rep_ref,
                        o_ref, kwt_ref, vwt_ref, bias_c, bias_n):
    n_new = SEQ_BLOCK * DEC_SEQ
    rows = DEC_SEQ * N_HEADS
    keep = WINDOW - DEC_SEQ

    @pl.when(pl.program_id(0) == 0)
    def _():
        row = lax.broadcasted_iota(jnp.int32, (rows, WINDOW), 0)
        col = lax.broadcasted_iota(jnp.int32, (rows, WINDOW), 1)
        t = row >> 5
        bias_c[...] = jnp.where(col > t, 0.0, NEG_INF)
        for s in range(SEQ_BLOCK):
            valid_new = (col < n_new) & ((col >> 2) == s) & ((col & 3) <= t)
            bias_n[s] = jnp.where(valid_new, 0.0, NEG_INF)

    kvn = jnp.concatenate([kvn_ref[...], jnp.zeros((WINDOW - n_new, 2 * D_KV), F32)], axis=0)
    kn_t = kvn[:, :D_KV].T
    vn_t = kvn[:, D_KV:].T
    kn_t_bf = kn_t.astype(BF16)
    vn_bf = kvn[:, D_KV:].astype(BF16)

    all_rows = SEQ_BLOCK * rows
    row2 = lax.broadcasted_iota(jnp.int32, (all_rows, D_KV), 0)
    col2 = lax.broadcasted_iota(jnp.int32, (all_rows, D_KV), 1)
    same_group = ((row2 & (N_HEADS - 1)) >> 3) == (col2 >> 6)
    win_lane = lax.broadcasted_iota(jnp.int32, (D_KV, WINDOW), 1)
    sink_b = jnp.concatenate([sink_ref[...]] * SEQ_BLOCK, axis=0)
    ones = jnp.ones((2 * WINDOW, LANES), BF16)

    q_all = q_ref[...].reshape(all_rows, HEAD_DIM)
    q4 = jnp.dot(q_all, rep_ref[...], preferred_element_type=F32)
    q4 = jnp.where(same_group, q4, 0.0).astype(BF16)
    s_n = jnp.dot(q4, kn_t_bf, preferred_element_type=F32) + bias_n[...].reshape(all_rows, WINDOW)
    s_c = jnp.concatenate(
        [jnp.dot(q4[s * rows:(s + 1) * rows], ckt_ref[s].astype(BF16), preferred_element_type=F32)
         + bias_c[...] for s in range(SEQ_BLOCK)], axis=0)
    m_b = jnp.maximum(jnp.max(jnp.maximum(s_c, s_n), axis=1, keepdims=True), sink_b)
    p_c = jnp.exp2(s_c - m_b).astype(BF16)
    p_n = jnp.exp2(s_n - m_b).astype(BF16)
    l_b = (jnp.dot(jnp.concatenate([p_c, p_n], axis=1), ones, preferred_element_type=F32)
           + jnp.exp2(sink_b - m_b))
    o4 = jnp.concatenate(
        [lax.dot_general(p_c[s * rows:(s + 1) * rows], cvt_ref[s].astype(BF16),
                         (((1,), (1,)), ((), ())), preferred_element_type=F32)
         for s in range(SEQ_BLOCK)], axis=0) + jnp.dot(p_n, vn_bf, preferred_element_type=F32)
    o4 = jnp.where(same_group, o4, 0.0)
    a = o4[:, 0:LANES] + o4[:, LANES:]
    o = (a + pltpu.roll(a, HEAD_DIM, 1)) / (2.0 * l_b)
    o_ref[...] = o[:, 0:HEAD_DIM].reshape(SEQ_BLOCK, rows, HEAD_DIM).astype(o_ref.dtype)

    for s in range(SEQ_BLOCK):
        k_t = ckt_ref[s]
        v_t = cvt_ref[s]
        new_shift = keep - s * DEC_SEQ
        kwt_ref[s] = jnp.where(win_lane >= keep, pltpu.roll(kn_t, new_shift, 1),
                               pltpu.roll(k_t, keep, 1))
        vwt_ref[s] = jnp.where(win_lane >= keep, pltpu.roll(vn_t, new_shift, 1),
                               pltpu.roll(v_t, keep, 1))


def _sample_attn(q_s, kv, cache_kt, cache_vt, sink_col, rep):
    rows = DEC_SEQ * N_HEADS
    n_new = SEQ_BLOCK * DEC_SEQ
    cache_spec = pl.BlockSpec((SEQ_BLOCK, D_KV, WINDOW), lambda i: (i, 0, 0))
    return pl.pallas_call(
        _sample_attn_kernel,
        grid=(DEC_BATCH // SEQ_BLOCK,),
        in_specs=[pl.BlockSpec((SEQ_BLOCK, rows, HEAD_DIM), lambda i: (i, 0, 0)),
                  pl.BlockSpec((n_new, 2 * D_KV), lambda i: (i + N_P // n_new, 0)),
                  cache_spec, cache_spec,
                  pl.BlockSpec((rows, LANES), lambda i: (0, 0)),
                  pl.BlockSpec((HEAD_DIM, D_KV), lambda i: (0, 0))],
        out_specs=[pl.BlockSpec((SEQ_BLOCK, rows, HEAD_DIM), lambda i: (i, 0, 0)),
                   cache_spec, cache_spec],
        out_shape=[jax.ShapeDtypeStruct((DEC_BATCH, rows, HEAD_DIM), BF16),
                   jax.ShapeDtypeStruct((DEC_BATCH, D_KV, WINDOW), F32),
                   jax.ShapeDtypeStruct((DEC_BATCH, D_KV, WINDOW), F32)],
        scratch_shapes=[pltpu.VMEM((rows, WINDOW), F32), pltpu.VMEM((SEQ_BLOCK, rows, WINDOW), F32)],
        compiler_params=_params(1),
        name="sample_attn",
    )(q_s, kv, cache_kt, cache_vt, sink_col, rep)


def _sample_merge_kernel(attn_ref, h_ref, b_ref, c_ref, ga_ref, gc_ref, st_ref, cw_ref,
                         mix_ref, state_a_ref, state_b_ref, u_scr):
    u = c_ref[...].astype(F32) * h_ref[...].astype(F32)
    rows = u.shape[0]
    n_seq = rows // DEC_SEQ
    for c in range(D_CONV // LANES):
        cols = slice(c * LANES, (c + 1) * LANES)
        u_scr[c] = u[:, cols]
        state_a_ref[:, cols] = u_scr[c, pl.ds(DEC_SEQ - 2, n_seq, stride=DEC_SEQ), :]
        state_b_ref[:, cols] = u_scr[c, pl.ds(DEC_SEQ - 1, n_seq, stride=DEC_SEQ), :]
    t = lax.broadcasted_iota(jnp.int32, u.shape, 0) & (DEC_SEQ - 1)
    st = st_ref[...]
    u1 = jnp.where(t == 0, pltpu.roll(st, rows - 1, 0), pltpu.roll(u, 1, 0))
    u2 = jnp.where(t < 2, st, pltpu.roll(u, 2, 0))
    mix = _gated_merge(attn_ref[...], u, u1, u2, b_ref, ga_ref, gc_ref, cw_ref)
    mix_ref[...] = mix.astype(mix_ref.dtype)


def _sample_merge(attn_s, rest, st_rows, conv_w):
    tm = 2 * WINDOW
    off = N_P // tm
    n_seq = tm // DEC_SEQ
    blk = lambda c: pl.BlockSpec((tm, D_MODEL), lambda i, c=c: (i + off, c))
    state_spec = pl.BlockSpec((n_seq, D_CONV), lambda i: (i, 0))
    return pl.pallas_call(
        _sample_merge_kernel,
        grid=(N_S // tm,),
        in_specs=[pl.BlockSpec((tm, D_ATTN), lambda i: (i, 0)),
                  blk(0), blk(1), blk(2), blk(3), blk(4),
                  pl.BlockSpec((tm, D_CONV), lambda i: (i, 0)),
                  pl.BlockSpec((CONV_WIDTH, D_CONV), lambda i: (0, 0))],
        out_specs=[pl.BlockSpec((tm, D_MODEL), lambda i: (i, 0)), state_spec, state_spec],
        out_shape=[jax.ShapeDtypeStruct((N_S, D_MODEL), BF16),
                   jax.ShapeDtypeStruct((DEC_BATCH, D_CONV), F32),
                   jax.ShapeDtypeStruct((DEC_BATCH, D_CONV), F32)],
        scratch_shapes=[pltpu.VMEM((D_CONV // LANES, tm, LANES), F32)],
        compiler_params=_params(1),
        name="sample_merge",
    )(attn_s, rest, rest, rest, rest, rest, st_rows, conv_w)


def _out_proj_kernel(mp_ref, ms_ref, w_ref, xp_ref, xs_ref, gf_ref, y_ref, yn_ref):
    is_prompt = pl.program_id(0) < N_P // ROW_TILE_SMALL
    res = jnp.where(is_prompt, xp_ref[...], xs_ref[...])
    mix = jnp.where(is_prompt, mp_ref[...], ms_ref[...])
    y = res + jnp.dot(mix, w_ref[...], preferred_element_type=F32)
    y_ref[...] = y
    yn_ref[...] = _rmsnorm_rows(y, gf_ref[...]).astype(yn_ref.dtype)


def _out_proj(mix_p, mix_s, w_out_bf, x_p, x_s, g_ffn):
    tm = ROW_TILE_SMALL
    last_p = N_P // tm - 1
    once = pl.Buffered(1)
    return pl.pallas_call(
        _out_proj_kernel,
        grid=(N_ALL // tm,),
        in_specs=[pl.BlockSpec((tm, D_ATTN), lambda i: (jnp.minimum(i, last_p), 0)),
                  pl.BlockSpec((tm, D_ATTN), lambda i: (0, 0), pipeline_mode=once),
                  pl.BlockSpec((D_ATTN, D_MODEL), lambda i: (0, 0), pipeline_mode=once),
                  pl.BlockSpec((tm, D_MODEL), lambda i: (jnp.minimum(i, last_p), 0)),
                  pl.BlockSpec((tm, D_MODEL), lambda i: (0, 0), pipeline_mode=once),
                  pl.BlockSpec((1, D_MODEL), lambda i: (0, 0))],
        out_specs=[pl.BlockSpec((tm, D_MODEL), lambda i: (i, 0)),
                   pl.BlockSpec((tm, D_MODEL), lambda i: (i, 0))],
        out_shape=[jax.ShapeDtypeStruct((N_ALL, D_MODEL), F32),
                   jax.ShapeDtypeStruct((N_ALL, D_MODEL), BF16)],
        compiler_params=_params(1),
        name="out_proj",
    )(mix_p, mix_s, w_out_bf, x_p, x_s, g_ffn)


def _ffn_up_kernel(x_ref, wg_ref, wu_ref, o_ref, wg_bf, wu_bf):
    def project(weight_chunks):
        piece = o_ref.shape[0] // FFN_UP_ROW_PIECES
        for c0 in range(0, o_ref.shape[1], MXU_WIDTH):
            cols = slice(c0, c0 + MXU_WIDTH)
            wg, wu = weight_chunks(cols)
            for r0 in range(0, o_ref.shape[0], piece):
                rows = slice(r0, r0 + piece)
                g = jnp.dot(x_ref[rows, :], wg, preferred_element_type=F32)
                u = jnp.dot(x_ref[rows, :], wu, preferred_element_type=F32)
                o_ref[rows, cols] = ((g * jax.nn.sigmoid(g)) * u).astype(o_ref.dtype)

    def cast_and_keep(cols):
        wg = wg_ref[:, cols].astype(BF16)
        wu = wu_ref[:, cols].astype(BF16)
        wg_bf[:, cols] = wg
        wu_bf[:, cols] = wu
        return wg, wu

    @pl.when(pl.program_id(1) == 0)
    def _():
        project(cast_and_keep)

    @pl.when(pl.program_id(1) != 0)
    def _():
        project(lambda cols: (wg_bf[:, cols], wu_bf[:, cols]))


def _ffn_up(yn, w_gate_up):
    tm = ROW_TILE_HUGE
    tn = 512
    nt = D_FF // tn
    return pl.pallas_call(
        _ffn_up_kernel,
        grid=(nt, N_ALL // tm),
        in_specs=[pl.BlockSpec((tm, D_MODEL), lambda j, i: (i, 0)),
                  pl.BlockSpec((D_MODEL, tn), lambda j, i: (0, j)),
                  pl.BlockSpec((D_MODEL, tn), lambda j, i: (0, j + nt))],
        out_specs=pl.BlockSpec((tm, tn), lambda j, i: (i, j)),
        out_shape=jax.ShapeDtypeStruct((N_ALL, D_FF), BF16),
        scratch_shapes=[pltpu.VMEM((D_MODEL, tn), BF16), pltpu.VMEM((D_MODEL, tn), BF16)],
        compiler_params=_params(2),
        name="ffn_up",
    )(yn, w_gate_up, w_gate_up)


def _ffn_down_kernel(a_ref, w_ref, r_ref, yp_ref, ys_ref):
    i = pl.program_id(0)
    n_prompt_tiles = N_P // FFN_DOWN_ROWS
    def project_into(y_ref):
        for c0 in range(0, D_MODEL, FFN_DOWN_CHUNK):
            cols = slice(c0, c0 + FFN_DOWN_CHUNK)
            y_ref[:, cols] = r_ref[:, cols] + jnp.dot(a_ref[...], w_ref[:, cols],
                                                      preferred_element_type=F32)

    @pl.when(i < n_prompt_tiles)
    def _():
        project_into(yp_ref)

    @pl.when(i >= n_prompt_tiles)
    def _():
        project_into(ys_ref)


def _ffn_down(act, w_down_bf, y1):
    tm = FFN_DOWN_ROWS
    n_p = N_P // tm
    return pl.pallas_call(
        _ffn_down_kernel,
        grid=(N_ALL // tm,),
        in_specs=[pl.BlockSpec((tm, D_FF), lambda i: (i, 0)),
                  pl.BlockSpec((D_FF, D_MODEL), lambda i: (0, 0), pipeline_mode=pl.Buffered(1)),
                  pl.BlockSpec((tm, D_MODEL), lambda i: (i, 0))],
        out_specs=[pl.BlockSpec((tm, D_MODEL), lambda i: (jnp.minimum(i, n_p - 1), 0)),
                   pl.BlockSpec((tm, D_MODEL), lambda i: (jnp.maximum(i - n_p, 0), 0))],
        out_shape=[jax.ShapeDtypeStruct((N_P, D_MODEL), F32),
                   jax.ShapeDtypeStruct((N_S, D_MODEL), F32)],
        compiler_params=_params(1),
        name="ffn_down",
    )(act, w_down_bf, y1)


def _rope_tables():
    inv = ROPE_THETA ** (-jnp.arange(0, HEAD_DIM, 2, dtype=F32) / HEAD_DIM)
    inv_t = jnp.tile(inv, LANES // (HEAD_DIM // 2))[None, :]
    sign = jnp.asarray(np.tile(np.repeat([-1.0, 1.0], HEAD_DIM // 2), LANES // HEAD_DIM), F32)[None, :]
    ang_a = (jnp.arange(N_P // LANES, dtype=jnp.int32) * LANES).astype(F32)[:, None] * inv_t
    ang_b = jnp.arange(LANES, dtype=jnp.int32).astype(F32)[:, None] * inv_t
    ca, sa = jnp.cos(ang_a)[:, None, :], jnp.sin(ang_a)[:, None, :]
    cb, sb = jnp.cos(ang_b)[None, :, :], jnp.sin(ang_b)[None, :, :]
    cos_p = (ca * cb - sa * sb).reshape(N_P, LANES)
    sin_p = (sa * cb + ca * sb).reshape(N_P, LANES)
    ang_s = (PAST_LEN + jnp.arange(DEC_SEQ, dtype=jnp.int32)).astype(F32)[:, None] * inv_t
    cos_s = jnp.tile(jnp.cos(ang_s), (DEC_BATCH, 1))
    sin_s = jnp.tile(jnp.sin(ang_s), (DEC_BATCH, 1))
    return (jnp.concatenate([cos_p, cos_s], axis=0),
            jnp.concatenate([sin_p, sin_s], axis=0) * sign)


def kernel(x_prompt, x_sample, cache_k_win, cache_v_win, state_conv, norm_mix, w_in, q_norm, k_norm,
           sinks, conv_w, w_out, norm_ffn, w_gate_up, w_down):
    assert x_prompt.shape == (1, SEQ, D_MODEL) and x_sample.shape == (DEC_BATCH, DEC_SEQ, D_MODEL)
    assert w_in.shape == (1, D_MODEL, D_IN)
    assert cache_k_win.shape == (1, DEC_BATCH, WINDOW, N_KV_HEADS, HEAD_DIM)

    x_p = x_prompt.reshape(N_P, D_MODEL)
    x_s = x_sample.reshape(N_S, D_MODEL)
    w_in2 = w_in.reshape(D_MODEL, D_IN)
    w_out2 = w_out.reshape(D_ATTN, D_MODEL)
    w_gu2 = w_gate_up.reshape(D_MODEL, 2 * D_FF)
    w_dn2 = w_down.reshape(D_FF, D_MODEL)
    conv_w2 = conv_w.reshape(CONV_WIDTH, D_CONV)
    cache_kt = jnp.transpose(cache_k_win.reshape(DEC_BATCH, WINDOW, D_KV), (0, 2, 1))
    cache_vt = jnp.transpose(cache_v_win.reshape(DEC_BATCH, WINDOW, D_KV), (0, 2, 1))

    cos_t, sin_t = _rope_tables()
    q_g128 = jnp.tile(q_norm.reshape(1, HEAD_DIM) * Q_SCALE, (1, LANES // HEAD_DIM))
    k_g128 = jnp.tile(k_norm.reshape(1, HEAD_DIM), (1, LANES // HEAD_DIM))
    rest_scale = jnp.asarray(np.repeat([1.0, 1.0, 1.0, 0.5, 0.5], D_MODEL)[None, :], F32)
    head_of_lane = np.arange(MXU_WIDTH) // HEAD_DIM
    bd = jnp.asarray(np.where(head_of_lane[:, None] == head_of_lane[None, :], 1.0 / HEAD_DIM, 0.0),
                     dtype=BF16)
    sinks2 = sinks.reshape(1, N_HEADS).astype(F32)
    sink_col = jnp.broadcast_to(jnp.tile(sinks2.reshape(N_HEADS) * LOG2E, DEC_SEQ)[:, None],
                                (DEC_SEQ * N_HEADS, LANES))
    rep = jnp.asarray(np.tile(np.eye(HEAD_DIM), (1, N_KV_HEADS)), dtype=BF16)
    st_rows = jnp.pad(state_conv.reshape(DEC_BATCH, CONV_WIDTH - 1, D_CONV),
                      ((0, 0), (0, DEC_SEQ - (CONV_WIDTH - 1)), (0, 0))).reshape(N_S, D_CONV)

    xn, kv = _norm_proj_kv(x_p, x_s, norm_mix.reshape(1, D_MODEL), w_in2, k_g128, cos_t, sin_t, bd)
    q, rest, w_out_bf = _proj_q_rest(xn, w_in2, q_g128, cos_t, sin_t, bd, rest_scale, w_out2)

    mix_p, tail, w_down_bf = _prompt_mix(sinks2, q, kv, rest, conv_w2, w_dn2)
    q_s = q[N_P:].reshape(DEC_BATCH, DEC_SEQ * N_HEADS, HEAD_DIM)
    attn_s, kwt, vwt = _sample_attn(q_s, kv, cache_kt, cache_vt, sink_col, rep)
    mix_s, state_a, state_b = _sample_merge(attn_s.reshape(N_S, D_ATTN), rest, st_rows, conv_w2)

    y1, yn = _out_proj(mix_p, mix_s, w_out_bf, x_p, x_s, norm_ffn.reshape(1, D_MODEL))
    act = _ffn_up(yn, w_gu2)
    y_p, y_s = _ffn_down(act, w_down_bf, y1)

    kv_tail = kv[N_P - WINDOW:N_P]
    win_shape = (1, DEC_BATCH, WINDOW, N_KV_HEADS, HEAD_DIM)
    return (y_p.reshape(1, SEQ, D_MODEL),
            y_s.reshape(DEC_BATCH, DEC_SEQ, D_MODEL),
            kv_tail[:, :D_KV].reshape(1, 1, WINDOW, N_KV_HEADS, HEAD_DIM),
            kv_tail[:, D_KV:].reshape(1, 1, WINDOW, N_KV_HEADS, HEAD_DIM),
            tail[8 - (CONV_WIDTH - 1):].reshape(1, 1, CONV_WIDTH - 1, D_CONV),
            jnp.transpose(kwt, (0, 2, 1)).reshape(win_shape),
            jnp.transpose(vwt, (0, 2, 1)).reshape(win_shape),
            jnp.stack([state_a, state_b], axis=1)[None])
```
